```python
import jax, jax.numpy as jnp
from jax import lax
import numpy as np

D_MODEL = 1024
BATCH = 16
SEQ = 4096
DEPTH = 1

HEAD_DIM = 64
N_HEADS_A = 8
N_HEADS_B = 8
WIDTH_A = N_HEADS_A * HEAD_DIM
WIDTH_B = N_HEADS_B * HEAD_DIM
IN_COLS = 3 * WIDTH_A + 3 * WIDTH_B
DILATED_GROUPS = ((128, 1), (512, 4), (2048, 16))
ROPE_THETA = 500000.0
ROT_DIM = HEAD_DIM // 4
Q_BLOCK = 128
N_EXPERTS = 32
TOP_K = 4
D_FF = D_MODEL
SWIGLU_ALPHA = 1.702
SWIGLU_LIMIT = 7.0
EXPERT_BLOCK = 512
NORM_EPS = 1e-5

kernel_name = "hybrid_dilated_stickbreaking_gated_moe"


def rmsnorm(x, g):
    xf = x.astype(jnp.float32)
    y = xf * lax.rsqrt(jnp.mean(xf * xf, axis=-1, keepdims=True) + NORM_EPS)
    return (y * g.astype(jnp.float32)).astype(x.dtype)


def partial_rotary(a, pos):
    half = ROT_DIM // 2
    inv_freq = jnp.float32(ROPE_THETA) ** (-jnp.arange(0, ROT_DIM, 2, dtype=jnp.float32) / ROT_DIM)
    ang = pos.astype(jnp.float32)[:, None] * inv_freq[None, :]
    cos, sin = jnp.cos(ang), jnp.sin(ang)
    af = a.astype(jnp.float32)
    x1, x2, rest = af[..., :half], af[..., half:ROT_DIM], af[..., ROT_DIM:]
    out = jnp.concatenate([x1 * cos - x2 * sin, x2 * cos + x1 * sin, rest], axis=-1)
    return out.astype(a.dtype)


def dilated_group_attn(q, k, v, window, dil):
    Bsz, H, S, Dh = q.shape
    n = window // dil
    span = dil * n
    S_pad = -(-S // span) * span
    L = S_pad // dil
    nb = L // n
    pad = ((0, 0), (0, 0), (0, S_pad - S), (0, 0))

    def to_blocks(a):
        a = jnp.pad(a, pad).reshape(Bsz, H, L, dil, Dh).transpose(0, 1, 3, 2, 4)
        return a.reshape(Bsz, H, dil, nb, n, Dh)

    def with_prev(a):
        prev = jnp.pad(a[:, :, :, :-1], ((0, 0), (0, 0), (0, 0), (1, 0), (0, 0), (0, 0)))
        return jnp.concatenate([prev, a], axis=-2)

    qb = to_blocks(q)
    kk = with_prev(to_blocks(k))
    vv = with_prev(to_blocks(v))
    s = jnp.einsum('bhrnqd,bhrnkd->bhrnqk', qb, kk).astype(jnp.float32) * (HEAD_DIM ** -0.5)
    qi = jnp.arange(n)[:, None]
    ki = jnp.arange(2 * n)[None, :]
    dist = n + qi - ki
    blk = jnp.arange(nb)[:, None, None]
    valid = (dist >= 0) & (dist <= n) & ((blk > 0) | (ki >= n))[...]
    s = jnp.where(valid, s, -jnp.inf)
    m = jnp.max(s, axis=-1)
    p = jnp.exp(s - m[..., None])
    l = jnp.sum(p, axis=-1)
    num = jnp.einsum('bhrnqk,bhrnkd->bhrnqd', p, vv.astype(jnp.float32))
    num = num.reshape(Bsz, H, dil, L, Dh).transpose(0, 1, 3, 2, 4).reshape(Bsz, H, S_pad, Dh)[:, :, :S]
    m = m.reshape(Bsz, H, dil, L).transpose(0, 1, 3, 2).reshape(Bsz, H, S_pad)[:, :, :S]
    l = l.reshape(Bsz, H, dil, L).transpose(0, 1, 3, 2).reshape(Bsz, H, S_pad)[:, :, :S]
    return num, m, l


def dilated_attention(q, k, v):
    outs = [dilated_group_attn(q, k, v, w, d) for (w, d) in DILATED_GROUPS]
    m_all = jnp.max(jnp.stack([o[1] for o in outs], axis=0), axis=0)
    num = sum(o[0] * jnp.exp(o[1] - m_all)[..., None] for o in outs)
    den = sum(o[2] * jnp.exp(o[1] - m_all) for o in outs)
    return num / den[..., None]


def stick_breaking_attention(q, k, v):
    Bsz, H, S, Dh = q.shape
    nq = S // Q_BLOCK
    qb = q.reshape(Bsz, H, nq, Q_BLOCK, Dh).transpose(2, 0, 1, 3, 4)
    spos = jnp.arange(S)
    vf = v.astype(jnp.float32)

    def block(args):
        qblk, i = args
        z = jnp.einsum('bhqd,bhkd->bhqk', qblk, k).astype(jnp.float32) * (HEAD_DIM ** -0.5)
        tpos = i * Q_BLOCK + jnp.arange(Q_BLOCK)
        mask = spos[None, :] < tpos[:, None]
        log_keep = jnp.where(mask, jax.nn.log_sigmoid(-z), 0.0)
        after = lax.cumsum(log_keep, axis=3, reverse=True) - log_keep
        a = jnp.where(mask, jnp.exp(jax.nn.log_sigmoid(z) + after), 0.0)
        return jnp.einsum('bhqk,bhkd->bhqd', a, vf)

    o = lax.map(block, (qb, jnp.arange(nq)))
    return o.transpose(1, 2, 0, 3, 4).reshape(Bsz, H, S, Dh)


def moe_ffn(h, w_router, b_router, w1, b1, w2, b2):
    Bsz, S, D = h.shape
    N = Bsz * S
    t = h.reshape(N, D)
    logits = (t @ w_router + b_router).astype(jnp.float32)
    top_v, top_e = lax.top_k(logits, TOP_K)
    gates = jax.nn.softmax(top_v, axis=-1)
    NK = N * TOP_K
    e_flat = top_e.reshape(NK).astype(jnp.int32)
    g_flat = gates.reshape(NK)
    tok_flat = (jnp.arange(NK, dtype=jnp.int32) // TOP_K)
    order = jnp.argsort(e_flat)
    e_sorted = e_flat[order]
    counts = jnp.bincount(e_flat, length=N_EXPERTS).astype(jnp.int32)
    padded = ((counts + EXPERT_BLOCK - 1) // EXPERT_BLOCK) * EXPERT_BLOCK
    start = jnp.cumsum(counts) - counts
    pstart = jnp.cumsum(padded) - padded
    dest = pstart[e_sorted] + (jnp.arange(NK, dtype=jnp.int32) - start[e_sorted])
    n_blocks = -(-NK // EXPERT_BLOCK) + N_EXPERTS
    R = n_blocks * EXPERT_BLOCK
    row_tok = jnp.full((R,), N, dtype=jnp.int32).at[dest].set(tok_flat[order])
    row_gate = jnp.zeros((R,), jnp.float32).at[dest].set(g_flat[order])
    pend = jnp.cumsum(padded)
    block_e = jnp.minimum(jnp.searchsorted(pend, jnp.arange(n_blocks, dtype=jnp.int32) * EXPERT_BLOCK,
                                           side='right'), N_EXPERTS - 1).astype(jnp.int32)
    t_pad = jnp.concatenate([t, jnp.zeros((1, D), t.dtype)], axis=0)

    def expert_block(args):
        rows, e = args
        xb = t_pad[rows]
        hid = xb @ w1[e] + b1[e]
        glu = jnp.minimum(hid[:, :D_FF], SWIGLU_LIMIT)
        lin = jnp.clip(hid[:, D_FF:], -SWIGLU_LIMIT, SWIGLU_LIMIT)
        act = glu * jax.nn.sigmoid(SWIGLU_ALPHA * glu) * (lin + 1.0)
        return act @ w2[e] + b2[e]

    y_rows = lax.map(expert_block, (row_tok.reshape(n_blocks, EXPERT_BLOCK), block_e))
    y_rows = y_rows.astype(jnp.float32) * row_gate.reshape(n_blocks, EXPERT_BLOCK, 1)
    y = jnp.zeros((N + 1, D), jnp.float32).at[row_tok].add(y_rows.reshape(R, D))[:N]
    return y.reshape(Bsz, S, D).astype(h.dtype)


def setup_inputs(seed: int = 0) -> dict:
    key = jax.random.key(seed)
    ks = jax.random.split(key, 20)
    D, E, F = D_MODEL, N_EXPERTS, D_FF
    nrm = lambda k, shape, fan: jax.random.normal(k, shape, jnp.float32) * (fan ** -0.5)
    return {
        "x": jax.random.normal(ks[0], (BATCH, SEQ, D), jnp.float32),
        "norm1_g": 1.0 + 0.02 * jax.random.normal(ks[1], (DEPTH, D), jnp.float32),
        "w_in": nrm(ks[2], (DEPTH, D, IN_COLS), D),
        "w_proj_a": nrm(ks[3], (DEPTH, WIDTH_A, D), WIDTH_A),
        "w_proj_b": nrm(ks[4], (DEPTH, WIDTH_B, D), WIDTH_B),
        "w_gate": nrm(ks[5], (DEPTH, D, 2 * D), D),
        "b_gate": 0.02 * jax.random.normal(ks[6], (DEPTH, 2 * D), jnp.float32),
        "w_out": nrm(ks[7], (DEPTH, D, D), D),
        "norm2_g": 1.0 + 0.02 * jax.random.normal(ks[8], (DEPTH, D), jnp.float32),
        "w_router": nrm(ks[9], (DEPTH, D, E), D),
        "b_router": 0.01 * jax.random.normal(ks[10], (DEPTH, E), jnp.float32),
        "w1": nrm(ks[11], (DEPTH, E, D, 2 * F), D),
        "b1": 0.02 * jax.random.normal(ks[12], (DEPTH, E, 2 * F), jnp.float32),
        "w2": nrm(ks[13], (DEPTH, E, F, D), F),
        "b2": 0.02 * jax.random.normal(ks[14], (DEPTH, E, D), jnp.float32),
        "norm_f_g": 1.0 + 0.02 * jax.random.normal(ks[15], (D,), jnp.float32),
    }


def reference(x, norm1_g, w_in, w_proj_a, w_proj_b, w_gate, b_gate, w_out, norm2_g,
              w_router, b_router, w1, b1, w2, b2, norm_f_g):
    Bsz, S, D = x.shape
    pos = jnp.arange(S, dtype=jnp.int32)

    def heads(a, n_heads):
        return a.reshape(Bsz, S, n_heads, HEAD_DIM).transpose(0, 2, 1, 3)

    for l in range(DEPTH):
        h = rmsnorm(x, norm1_g[l])
        proj = h @ w_in[l]
        qa, ka, va, qb, kb, vb = jnp.split(
            proj, [WIDTH_A, 2 * WIDTH_A, 3 * WIDTH_A, 3 * WIDTH_A + WIDTH_B,
                   3 * WIDTH_A + 2 * WIDTH_B], axis=-1)
        qa = partial_rotary(heads(qa, N_HEADS_A), pos)
        ka = partial_rotary(heads(ka, N_HEADS_A), pos)
        o_a = dilated_attention(qa, ka, heads(va, N_HEADS_A))
        o_a = o_a.transpose(0, 2, 1, 3).reshape(Bsz, S, WIDTH_A).astype(x.dtype)
        o_b = stick_breaking_attention(heads(qb, N_HEADS_B), heads(kb, N_HEADS_B), heads(vb, N_HEADS_B))
        o_b = o_b.transpose(0, 2, 1, 3).reshape(Bsz, S, WIDTH_B).astype(x.dtype)
        g = jax.nn.sigmoid((h @ w_gate[l] + b_gate[l]).astype(jnp.float32))
        merged = g[..., :D] * (o_a @ w_proj_a[l]) + g[..., D:] * (o_b @ w_proj_b[l])
        x = x + (merged.astype(x.dtype) @ w_out[l])
        h2 = rmsnorm(x, norm2_g[l])
        x = x + moe_ffn(h2, w_router[l], b_router[l], w1[l], b1[l], w2[l], b2[l])
    return rmsnorm(x, norm_f_g)
```

```python
import functools

import jax
import jax.numpy as jnp
from jax import lax
from jax.experimental import pallas as pl
from jax.experimental.pallas import tpu as pltpu

F32 = jnp.float32
BF16 = jnp.bfloat16
I32 = jnp.int32

HEAD_DIM = 64
WIDTH = 512
QKV = 3 * WIDTH
DILATED_GROUPS = ((128, 1), (512, 4), (2048, 16))
BAND = 128
ROPE_THETA = 500000.0
ROT_DIM = HEAD_DIM // 4
N_EXPERTS = 32
TOP_K = 4
SWIGLU_ALPHA = 1.702
SWIGLU_LIMIT = 7.0
NORM_EPS = 1e-5

LANES = 128
VMEM_LIMIT = 56 * 1024 * 1024

PROJ_ROWS = 512
MERGE_ROWS = 256
STICK_ROWS = 256
EXPERT_ROWS = 512
DISPATCH_ROWS = 512
COMBINE_ROWS = 256
DEST_ROWS = 2048
DIL_LANES = {1: 128, 4: 256, 16: 512}


def _rms(x, g):
    return x * lax.rsqrt(jnp.mean(x * x, axis=-1, keepdims=True) + NORM_EPS) * g


def _params(*sem):
    return pltpu.CompilerParams(dimension_semantics=sem, vmem_limit_bytes=VMEM_LIMIT)


def _const_spec(shape):
    return pl.BlockSpec(shape, lambda *_: (0,) * len(shape), pipeline_mode=pl.Buffered(1))


def _proj_body(x_ref, g_ref, w_ref, cos_ref, sa_ref, sb_ref, a_ref, b_ref):
    h = _rms(x_ref[...], g_ref[...]).astype(BF16)
    rep = WIDTH // LANES
    cos = jnp.concatenate([cos_ref[...]] * rep, axis=1)
    sa = jnp.concatenate([sa_ref[...]] * rep, axis=1)
    sb = jnp.concatenate([sb_ref[...]] * rep, axis=1)
    half = ROT_DIM // 2
    for c in range(6):
        acc = jnp.dot(h, w_ref[:, c * WIDTH:(c + 1) * WIDTH], preferred_element_type=F32)
        if c in (0, 1):
            acc = (acc * cos + pltpu.roll(acc, WIDTH - half, 1) * sa
                   + pltpu.roll(acc, half, 1) * sb)
        if c in (0, 3):
            acc = acc * (HEAD_DIM ** -0.5)
        dst = a_ref if c < 3 else b_ref
        dst[:, (c % 3) * WIDTH:(c % 3 + 1) * WIDTH] = acc.astype(BF16)


def _rotary_tables(seq):
    half = ROT_DIM // 2
    inv_freq = jnp.float32(ROPE_THETA) ** (-jnp.arange(0, ROT_DIM, 2, dtype=F32) / ROT_DIM)
    ang = jnp.arange(seq, dtype=jnp.int32).astype(F32)[:, None] * inv_freq[None, :]
    cos, sin = jnp.cos(ang), jnp.sin(ang)
    ones = jnp.ones((seq, HEAD_DIM - ROT_DIM), F32)
    zeros_h = jnp.zeros((seq, half), F32)
    zeros_r = jnp.zeros((seq, HEAD_DIM - ROT_DIM), F32)
    cos_t = jnp.concatenate([cos, cos, ones], axis=1)
    sa_t = jnp.concatenate([-sin, zeros_h, zeros_r], axis=1)
    sb_t = jnp.concatenate([zeros_h, sin, zeros_r], axis=1)
    rep = LANES // HEAD_DIM
    return tuple(jnp.tile(t, (1, rep)) for t in (cos_t, sa_t, sb_t))


def _proj(x2, g1, w_in, seq):
    n, d = x2.shape
    tm = PROJ_ROWS
    tabs = _rotary_tables(seq)
    per_seq = seq // tm
    tab_spec = pl.BlockSpec((tm, LANES), lambda i: (i % per_seq, 0))
    out = jax.ShapeDtypeStruct((n, QKV), BF16)
    return pl.pallas_call(
        _proj_body,
        out_shape=(out, out),
        grid=(n // tm,),
        in_specs=[pl.BlockSpec((tm, d), lambda i: (i, 0)),
                  _const_spec((1, d)),
                  _const_spec((d, 2 * QKV)),
                  tab_spec, tab_spec, tab_spec],
        out_specs=(pl.BlockSpec((tm, QKV), lambda i: (i, 0)),
                   pl.BlockSpec((tm, QKV), lambda i: (i, 0))),
        compiler_params=_params("parallel"),
        name="proj",
    )(x2, g1, w_in, *tabs)


def _dil_body(q_ref, k_ref, v_ref, o_ref, lse_ref, *, rows, lanes):
    nb = rows // BAND
    heads = lanes // HEAD_DIM
    lane = lax.broadcasted_iota(I32, (1, lanes), 1)
    qi = lax.broadcasted_iota(I32, (BAND, 2 * BAND), 0)
    ki = lax.broadcasted_iota(I32, (BAND, 2 * BAND), 1)
    valid = (ki >= qi) & (ki <= qi + BAND)
    qi0 = lax.broadcasted_iota(I32, (BAND, BAND), 0)
    ki0 = lax.broadcasted_iota(I32, (BAND, BAND), 1)
    valid0 = ki0 <= qi0
    nt = (((1,), (1,)), ((), ()))

    def block(q, kc, vc, mask):
        o_t = jnp.zeros((BAND, lanes), F32)
        l_t = jnp.zeros((BAND, lanes), F32)
        for h in range(heads):
            hm = (lane >= h * HEAD_DIM) & (lane < (h + 1) * HEAD_DIM)
            qh = jnp.where(hm, q, jnp.zeros_like(q))
            s = lax.dot_general(qh, kc, nt, preferred_element_type=F32)
            s = jnp.where(mask, s, -jnp.inf)
            m = jnp.max(s, axis=-1, keepdims=True)
            p = jnp.exp(s - m)
            l = jnp.sum(p, axis=-1, keepdims=True)
            pv = jnp.dot(p.astype(BF16), vc, preferred_element_type=F32)
            o_t = jnp.where(hm, pv / l, o_t)
            l_t = jnp.where(hm, m + jnp.log(l), l_t)
        return o_t, l_t

    o_t, l_t = block(q_ref[0, 0:BAND, :], k_ref[0, 0:BAND, :], v_ref[0, 0:BAND, :], valid0)
    o_ref[0, 0:BAND, :] = o_t.astype(BF16)
    lse_ref[0, 0:BAND, :] = l_t

    def step(i, carry):
        r0 = pl.multiple_of(i * BAND, BAND)
        rk = pl.multiple_of(i * BAND - BAND, BAND)
        o_t, l_t = block(q_ref[0, pl.ds(r0, BAND), :], k_ref[0, pl.ds(rk, 2 * BAND), :],
                         v_ref[0, pl.ds(rk, 2 * BAND), :], valid)
        o_ref[0, pl.ds(r0, BAND), :] = o_t.astype(BF16)
        lse_ref[0, pl.ds(r0, BAND), :] = l_t
        return carry

    lax.fori_loop(1, nb, step, 0)


def _dilated_group(qkv_a, bsz, seq, dil):
    rows = seq // dil
    lanes = DIL_LANES[dil]
    hb = WIDTH // lanes
    a3 = qkv_a.reshape(bsz, rows, dil * QKV)
    per = QKV // lanes

    def in_spec(off):
        return pl.BlockSpec((1, rows, lanes), lambda b, r, h: (b, 0, r * per + off * hb + h))

    out_spec = pl.BlockSpec((1, rows, lanes), lambda b, r, h: (b, 0, r * hb + h))
    o, lse = pl.pallas_call(
        functools.partial(_dil_body, rows=rows, lanes=lanes),
        out_shape=(jax.ShapeDtypeStruct((bsz, rows, dil * WIDTH), BF16),
                   jax.ShapeDtypeStruct((bsz, rows, dil * WIDTH), F32)),
        grid=(bsz, dil, hb),
        in_specs=[in_spec(0), in_spec(1), in_spec(2)],
        out_specs=(out_spec, out_spec),
        compiler_params=_params("parallel", "parallel", "parallel"),
        name=f"dilated{dil}",
    )(a3, a3, a3)
    return o.reshape(bsz * seq, WIDTH), lse.reshape(bsz * seq, WIDTH)


def _stick_body(q_ref, k_ref, v_ref, o_ref, *, tq):
    i = pl.program_id(2)
    lane = lax.broadcasted_iota(I32, (1, LANES), 1)
    ti = lax.broadcasted_iota(I32, (tq, tq), 0)
    si = lax.broadcasted_iota(I32, (tq, tq), 1)
    causal = si < ti
    later = (ti > si).astype(BF16)
    nt = (((1,), (1,)), ((), ()))
    q = q_ref[0]
    out = jnp.zeros((tq, LANES), F32)

    def tile(qh, j, carry, acc, diag):
        r0 = pl.multiple_of(j * tq, tq)
        kj = k_ref[0, pl.ds(r0, tq), :]
        vj = v_ref[0, pl.ds(r0, tq), :]
        z = lax.dot_general(qh, kj, nt, preferred_element_type=F32)
        tail = jnp.log(1.0 + jnp.exp(-jnp.abs(z)))
        sp = jnp.maximum(z, 0.0) + tail
        log_beta = jnp.minimum(z, 0.0) - tail
        if diag:
            sp = jnp.where(causal, sp, 0.0)
        after = jnp.dot(sp.astype(BF16), later, preferred_element_type=F32)
        a = jnp.exp(log_beta - after - carry)
        if diag:
            a = jnp.where(causal, a, 0.0)
        acc = acc + jnp.dot(a.astype(BF16), vj, preferred_element_type=F32)
        carry = carry + jnp.sum(sp, axis=-1, keepdims=True)
        return carry, acc

    for h in range(LANES // HEAD_DIM):
        hm = (lane >= h * HEAD_DIM) & (lane < (h + 1) * HEAD_DIM)
        qh = jnp.where(hm, q, jnp.zeros_like(q))
        carry, acc = tile(qh, i, jnp.zeros((tq, 1), F32), jnp.zeros((tq, LANES), F32), True)

        def step(t, c, qh=qh):
            return tile(qh, i - 1 - t, c[0], c[1], False)

        carry, acc = lax.fori_loop(0, i, step, (carry, acc))
        out = jnp.where(hm, acc, out)
    o_ref[0] = out.astype(BF16)


def _stick(qkv_b, bsz, seq):
    tq = STICK_ROWS
    b3 = qkv_b.reshape(bsz, seq, QKV)
    hb = WIDTH // LANES
    return pl.pallas_call(
        functools.partial(_stick_body, tq=tq),
        out_shape=jax.ShapeDtypeStruct((bsz, seq, WIDTH), BF16),
        grid=(bsz, hb, seq // tq),
        in_specs=[pl.BlockSpec((1, tq, LANES), lambda b, h, i: (b, i, h)),
                  pl.BlockSpec((1, seq, LANES), lambda b, h, i: (b, 0, hb + h)),
                  pl.BlockSpec((1, seq, LANES), lambda b, h, i: (b, 0, 2 * hb + h))],
        out_specs=pl.BlockSpec((1, tq, LANES), lambda b, h, i: (b, i, h)),
        compiler_params=_params("parallel", "parallel", "parallel"),
        name="stick",
    )(b3, b3, b3).reshape(bsz * seq, WIDTH)


def _merge_body(x_ref, o1_ref, o2_ref, o3_ref, l1_ref, l2_ref, l3_ref, ob_ref,
                g1_ref, wg_ref, bg_ref, pa_ref, pb_ref, wo_ref, g2_ref, wr_ref, br_ref,
                x1_ref, h2_ref, ids_ref, rank_ref, gate_ref, cnt_ref, carry_ref):
    tm, d = x_ref.shape

    @pl.when(pl.program_id(0) == 0)
    def _():
        carry_ref[...] = jnp.zeros_like(carry_ref)

    x = x_ref[...]
    h = _rms(x, g1_ref[...]).astype(BF16)
    gate = jax.nn.sigmoid(jnp.dot(h, wg_ref[...], preferred_element_type=F32) + bg_ref[...])

    l1, l2, l3 = l1_ref[...], l2_ref[...], l3_ref[...]
    m = jnp.maximum(jnp.maximum(l1, l2), l3)
    w1, w2, w3 = jnp.exp(l1 - m), jnp.exp(l2 - m), jnp.exp(l3 - m)
    num = (o1_ref[...].astype(F32) * w1 + o2_ref[...].astype(F32) * w2
           + o3_ref[...].astype(F32) * w3)
    o_a = (num / (w1 + w2 + w3)).astype(BF16)

    merged = (gate[:, :d] * jnp.dot(o_a, pa_ref[...], preferred_element_type=F32)
              + gate[:, d:] * jnp.dot(ob_ref[...], pb_ref[...], preferred_element_type=F32))
    x1 = x + jnp.dot(merged.astype(BF16), wo_ref[...], preferred_element_type=F32)
    x1_ref[...] = x1
    h2 = _rms(x1, g2_ref[...])
    h2_ref[...] = h2

    logits = jnp.dot(h2, wr_ref[...], preferred_element_type=F32,
                     precision=lax.Precision.HIGHEST) + br_ref[...]
    eidx = lax.broadcasted_iota(I32, (tm, N_EXPERTS), 1)
    work = logits
    tops, ids, hots = [], [], []
    for _ in range(TOP_K):
        mk = jnp.max(work, axis=-1, keepdims=True)
        ik = jnp.min(jnp.where(work == mk, eidx, N_EXPERTS), axis=-1, keepdims=True)
        hot = eidx == ik
        work = jnp.where(hot, -jnp.inf, work)
        tops.append(mk)
        ids.append(ik)
        hots.append(hot)
    exps = [jnp.exp(t - tops[0]) for t in tops]
    den = exps[0] + exps[1] + exps[2] + exps[3]
    gate_ref[...] = jnp.concatenate([e / den for e in exps], axis=1)
    ids_ref[...] = jnp.concatenate(ids, axis=1)

    cnt = sum(hot.astype(F32) for hot in hots)
    ri = lax.broadcasted_iota(I32, (tm, tm), 0)
    ci = lax.broadcasted_iota(I32, (tm, tm), 1)
    before = (ci < ri).astype(BF16)
    prefix = jnp.dot(before, cnt.astype(BF16), preferred_element_type=F32) + carry_ref[...]
    ranks = [jnp.sum(jnp.where(hot, prefix, 0.0), axis=-1, keepdims=True) for hot in hots]
    rank_ref[...] = jnp.concatenate(ranks, axis=1).astype(I32)
    total = carry_ref[...] + jnp.sum(cnt, axis=0, keepdims=True)
    carry_ref[...] = total
    cnt_ref[...] = total.astype(I32)


def _merge(x2, dil, o_b, g1, w_gate, b_gate, w_pa, w_pb, w_out, g2, w_r, b_r):
    n, d = x2.shape
    tm = MERGE_ROWS
    row = lambda w: pl.BlockSpec((tm, w), lambda i: (i, 0))
    (o1, l1), (o2, l2), (o3, l3) = dil
    return pl.pallas_call(
        _merge_body,
        out_shape=(jax.ShapeDtypeStruct((n, d), F32),
                   jax.ShapeDtypeStruct((n, d), F32),
                   jax.ShapeDtypeStruct((n, TOP_K), I32),
                   jax.ShapeDtypeStruct((n, TOP_K), I32),
                   jax.ShapeDtypeStruct((n, TOP_K), F32),
                   jax.ShapeDtypeStruct((1, N_EXPERTS), I32)),
        grid=(n // tm,),
        in_specs=[row(d)] + [row(WIDTH)] * 7 + [
            _const_spec((1, d)), _const_spec((d, 2 * d)), _const_spec((1, 2 * d)),
            _const_spec((WIDTH, d)), _const_spec((WIDTH, d)), _const_spec((d, d)),
            _const_spec((1, d)), _const_spec((d, N_EXPERTS)), _const_spec((1, N_EXPERTS))],
        out_specs=(row(d), row(d), row(TOP_K), row(TOP_K), row(TOP_K),
                   pl.BlockSpec((1, N_EXPERTS), lambda i: (0, 0))),
        scratch_shapes=[pltpu.VMEM((1, N_EXPERTS), F32)],
        compiler_params=_params("arbitrary"),
        name="merge",
    )(x2, o1, o2, o3, l1, l2, l3, o_b, g1, w_gate, b_gate, w_pa, w_pb, w_out, g2, w_r, b_r)


def _dest_body(ids_ref, rank_ref, start_ref, dest_ref):
    tm = ids_ref.shape[0]
    eidx = lax.broadcasted_iota(I32, (tm, N_EXPERTS), 1)
    ids = ids_ref[...]
    cols = []
    for k in range(TOP_K):
        hot = eidx == ids[:, k:k + 1]
        cols.append(jnp.sum(jnp.where(hot, start_ref[...], 0), axis=-1, keepdims=True))
    dest_ref[...] = jnp.concatenate(cols, axis=1) + rank_ref[...]


def _dest(ids, ranks, starts):
    n = ids.shape[0]
    tm = DEST_ROWS
    row = pl.BlockSpec((tm, TOP_K), lambda i: (i, 0))
    return pl.pallas_call(
        _dest_body,
        out_shape=jax.ShapeDtypeStruct((n, TOP_K), I32),
        grid=(n // tm,),
        in_specs=[row, row, _const_spec((1, N_EXPERTS))],
        out_specs=row,
        compiler_params=_params("parallel"),
        name="dest",
    )(ids, ranks, starts)


def _dispatch_body(pad_ref, nb_ref, dest_hbm, h_ref, xs_hbm, idx_ref, zero_ref, sem_idx, sem):
    tm = h_ref.shape[0]
    n_blocks = xs_hbm.shape[0] // EXPERT_ROWS
    i = pl.program_id(0)
    idx_cp = pltpu.make_async_copy(dest_hbm.at[pl.ds(i * tm * TOP_K, tm * TOP_K)], idx_ref, sem_idx)
    idx_cp.start()

    @pl.when(i == 0)
    def _():
        zero_ref[...] = jnp.zeros_like(zero_ref)
        for e in range(N_EXPERTS):
            r0 = pl.multiple_of(pad_ref[e], EXPERT_ROWS)
            pltpu.make_async_copy(zero_ref, xs_hbm.at[pl.ds(r0, EXPERT_ROWS)], sem).start()
        for e in range(N_EXPERTS):
            pltpu.make_async_copy(zero_ref, xs_hbm.at[pl.ds(0, EXPERT_ROWS)], sem).wait()

        def fill(b, carry):
            r0 = pl.multiple_of(b * EXPERT_ROWS, EXPERT_ROWS)
            cp = pltpu.make_async_copy(zero_ref, xs_hbm.at[pl.ds(r0, EXPERT_ROWS)], sem)
            cp.start()
            cp.wait()
            return carry

        lax.fori_loop(nb_ref[0], n_blocks, fill, 0)

    idx_cp.wait()

    def body(t, carry):
        for k in range(TOP_K):
            d = idx_ref[t * TOP_K + k]
            pltpu.make_async_copy(h_ref.at[pl.ds(t, 1)], xs_hbm.at[pl.ds(d, 1)], sem).start()
        return carry

    lax.fori_loop(0, tm, body, 0)
    for k in range(TOP_K):
        pltpu.make_async_copy(h_ref, xs_hbm.at[pl.ds(0, tm)], sem).wait()


def _dispatch(pad_start, n_used, dest_flat, h2, n_rows):
    n, d = h2.shape
    tm = DISPATCH_ROWS
    return pl.pallas_call(
        _dispatch_body,
        out_shape=jax.ShapeDtypeStruct((n_rows, d), h2.dtype),
        grid_spec=pltpu.PrefetchScalarGridSpec(
            num_scalar_prefetch=2,
            grid=(n // tm,),
            in_specs=[pl.BlockSpec(memory_space=pl.ANY),
                      pl.BlockSpec((tm, d), lambda i, pad, nb: (i, 0))],
            out_specs=pl.BlockSpec(memory_space=pl.ANY),
            scratch_shapes=[pltpu.SMEM((tm * TOP_K,), I32),
                            pltpu.VMEM((EXPERT_ROWS, d), h2.dtype),
                            pltpu.SemaphoreType.DMA,
                            pltpu.SemaphoreType.DMA]),
        compiler_params=_params("arbitrary"),
        name="dispatch",
    )(pad_start, n_used, dest_flat, h2)


def _expert_body(be_ref, nb_ref, x_ref, w1_ref, b1_ref, w2_ref, b2_ref, y_ref):
    f = w2_ref.shape[1]

    @pl.when(pl.program_id(0) < nb_ref[0])
    def _():
        xb = x_ref[...].astype(BF16)
        hid = jnp.dot(xb, w1_ref[0], preferred_element_type=F32) + b1_ref[0]
        glu = jnp.minimum(hid[:, :f], SWIGLU_LIMIT)
        lin = jnp.clip(hid[:, f:], -SWIGLU_LIMIT, SWIGLU_LIMIT)
        act = glu * jax.nn.sigmoid(SWIGLU_ALPHA * glu) * (lin + 1.0)
        y_ref[...] = jnp.dot(act.astype(BF16), w2_ref[0], preferred_element_type=F32) + b2_ref[0]

    @pl.when(pl.program_id(0) >= nb_ref[0])
    def _():
        y_ref[...] = jnp.zeros_like(y_ref)


def _experts(block_e, n_used, xs, w1, b1, w2, b2, n_blocks):
    d = xs.shape[1]
    f = w2.shape[1]
    br = EXPERT_ROWS
    blk = lambda b, be, nb: (jnp.minimum(b, nb[0] - 1), 0)
    per_e = lambda b, be, nb: (be[b], 0, 0)
    return pl.pallas_call(
        _expert_body,
        out_shape=jax.ShapeDtypeStruct((n_blocks * br, d), F32),
        grid_spec=pltpu.PrefetchScalarGridSpec(
            num_scalar_prefetch=2,
            grid=(n_blocks,),
            in_specs=[pl.BlockSpec((br, d), blk),
                      pl.BlockSpec((1, d, 2 * f), per_e),
                      pl.BlockSpec((1, 1, 2 * f), per_e),
                      pl.BlockSpec((1, f, d), per_e),
                      pl.BlockSpec((1, 1, d), per_e)],
            out_specs=pl.BlockSpec((br, d), lambda b, be, nb: (b, 0))),
        compiler_params=_params("arbitrary"),
        name="experts",
    )(block_e, n_used, xs, w1, b1, w2, b2)


def _combine_body(dest_hbm, x1_ref, gate_ref, g_ref, y_hbm, o_ref, idx_ref, buf_ref, sem_idx, sem,
                  *, final_norm):
    tm = x1_ref.shape[0]
    i = pl.program_id(0)
    idx_cp = pltpu.make_async_copy(dest_hbm.at[pl.ds(i * tm * TOP_K, tm * TOP_K)], idx_ref, sem_idx)
    idx_cp.start()
    idx_cp.wait()

    def body(t, carry):
        for k in range(TOP_K):
            d = idx_ref[t * TOP_K + k]
            pltpu.make_async_copy(y_hbm.at[pl.ds(d, 1)], buf_ref.at[k, pl.ds(t, 1)], sem).start()
        return carry

    lax.fori_loop(0, tm, body, 0)
    for k in range(TOP_K):
        pltpu.make_async_copy(y_hbm.at[pl.ds(0, tm)], buf_ref.at[k], sem).wait()

    gate = gate_ref[...]
    acc = x1_ref[...]
    for k in range(TOP_K):
        acc = acc + gate[:, k:k + 1] * buf_ref[k]
    o_ref[...] = _rms(acc, g_ref[...]) if final_norm else acc


def _combine(dest_flat, x1, gates, g_f, y, final_norm):
    n, d = x1.shape
    tm = COMBINE_ROWS
    return pl.pallas_call(
        functools.partial(_combine_body, final_norm=final_norm),
        out_shape=jax.ShapeDtypeStruct((n, d), F32),
        grid=(n // tm,),
        in_specs=[pl.BlockSpec(memory_space=pl.ANY),
                  pl.BlockSpec((tm, d), lambda i: (i, 0)),
                  pl.BlockSpec((tm, TOP_K), lambda i: (i, 0)),
                  _const_spec((1, d)),
                  pl.BlockSpec(memory_space=pl.ANY)],
        out_specs=pl.BlockSpec((tm, d), lambda i: (i, 0)),
        scratch_shapes=[pltpu.SMEM((tm * TOP_K,), I32),
                        pltpu.VMEM((TOP_K, tm, d), F32),
                        pltpu.SemaphoreType.DMA,
                        pltpu.SemaphoreType.DMA],
        compiler_params=_params("arbitrary"),
        name="combine",
    )(dest_flat, x1, gates, g_f, y)


def _layer(x2, bsz, seq, p):
    n, d = x2.shape
    qkv_a, qkv_b = _proj(x2, p["g1"], p["w_in"], seq)
    dil = [_dilated_group(qkv_a, bsz, seq, dl) for (_, dl) in DILATED_GROUPS]
    o_b = _stick(qkv_b, bsz, seq)
    x1, h2, ids, ranks, gates, counts = _merge(
        x2, dil, o_b, p["g1"], p["w_gate"], p["b_gate"], p["w_pa"], p["w_pb"], p["w_out"],
        p["g2"], p["w_r"], p["b_r"])

    br = EXPERT_ROWS
    counts = counts[0]
    padded = ((counts + br - 1) // br) * br
    ends = jnp.cumsum(padded)
    starts = ends - padded
    n_blocks = (n * TOP_K) // br + N_EXPERTS
    block_e = jnp.minimum(
        jnp.searchsorted(ends, jnp.arange(n_blocks, dtype=I32) * br, side="right"),
        N_EXPERTS - 1).astype(I32)
    n_used = (ends[-1:] // br).astype(I32)

    dest = _dest(ids, ranks, starts.reshape(1, N_EXPERTS).astype(I32)).reshape(n * TOP_K)
    xs = _dispatch((starts + (counts // br) * br).astype(I32), n_used, dest, h2, n_blocks * br)
    y = _experts(block_e, n_used, xs, p["w1"], p["b1"], p["w2"], p["b2"], n_blocks)
    return dest, x1, gates, y


def kernel(x, norm1_g, w_in, w_proj_a, w_proj_b, w_gate, b_gate, w_out, norm2_g,
           w_router, b_router, w1, b1, w2, b2, norm_f_g):
    bsz, seq, d = x.shape
    depth = w_in.shape[0]
    x2 = x.reshape(bsz * seq, d)
    for l in range(depth):
        p = dict(
            g1=norm1_g[l].reshape(1, d), w_in=w_in[l].astype(BF16),
            w_gate=w_gate[l].astype(BF16), b_gate=b_gate[l].reshape(1, 2 * d),
            w_pa=w_proj_a[l].astype(BF16), w_pb=w_proj_b[l].astype(BF16),
            w_out=w_out[l].astype(BF16), g2=norm2_g[l].reshape(1, d),
            w_r=w_router[l], b_r=b_router[l].reshape(1, N_EXPERTS),
            w1=w1[l].astype(BF16), b1=b1[l].reshape(N_EXPERTS, 1, -1),
            w2=w2[l].astype(BF16), b2=b2[l].reshape(N_EXPERTS, 1, -1))
        dest, x1, gates, y = _layer(x2, bsz, seq, p)
        x2 = _combine(dest, x1, gates, norm_f_g.reshape(1, d), y, final_norm=(l == depth - 1))
    return x2.reshape(bsz, seq, d)
```

```python
import functools

import jax
import jax.numpy as jnp
from jax import lax
from jax.experimental import pallas as pl
from jax.experimental.pallas import tpu as pltpu

F32 = jnp.float32
BF16 = jnp.bfloat16
I32 = jnp.int32

HEAD_DIM = 64
WIDTH = 512
QKV = 3 * WIDTH
DILATED_GROUPS = ((128, 1), (512, 4), (2048, 16))
BAND = 128
ROPE_THETA = 500000.0
ROT_DIM = HEAD_DIM // 4
N_EXPERTS = 32
TOP_K = 4
SWIGLU_ALPHA = 1.702
SWIGLU_LIMIT = 7.0
NORM_EPS = 1e-5

LANES = 128
VMEM_LIMIT = 56 * 1024 * 1024

LOG2E = 1.4426950408889634

PROJ_ROWS = 512
MERGE_ROWS = 256
STICK_ROWS = 512
STICK_KEYS = 256
EXPERT_ROWS = 512
DISPATCH_ROWS = 512
COMBINE_ROWS = 256
DEST_ROWS = 2048
DIL_UNROLL = 8
MIX_ROWS = 512


def _rms(x, g):
    return x * lax.rsqrt(jnp.mean(x * x, axis=-1, keepdims=True) + NORM_EPS) * g


def _params(*sem):
    return pltpu.CompilerParams(dimension_semantics=sem, vmem_limit_bytes=VMEM_LIMIT)


def _const_spec(shape):
    return pl.BlockSpec(shape, lambda *_: (0,) * len(shape), pipeline_mode=pl.Buffered(1))


def _proj_body(x_ref, g_ref, w_ref, cos_ref, sa_ref, sb_ref, a_ref, b_ref):
    h = _rms(x_ref[...], g_ref[...]).astype(BF16)
    rep = WIDTH // LANES
    cos = jnp.concatenate([cos_ref[...]] * rep, axis=1)
    sa = jnp.concatenate([sa_ref[...]] * rep, axis=1)
    sb = jnp.concatenate([sb_ref[...]] * rep, axis=1)
    half = ROT_DIM // 2
    for c in range(6):
        acc = jnp.dot(h, w_ref[:, c * WIDTH:(c + 1) * WIDTH], preferred_element_type=F32)
        if c in (0, 1):
            acc = (acc * cos + pltpu.roll(acc, WIDTH - half, 1) * sa
                   + pltpu.roll(acc, half, 1) * sb)
        if c == 0:
            acc = acc * (HEAD_DIM ** -0.5)
        if c == 3:
            acc = acc * (HEAD_DIM ** -0.5 * LOG2E)
        dst = a_ref if c < 3 else b_ref
        dst[:, (c % 3) * WIDTH:(c % 3 + 1) * WIDTH] = acc.astype(dst.dtype)


def _rotary_tables(seq):
    half = ROT_DIM // 2
    inv_freq = jnp.float32(ROPE_THETA) ** (-jnp.arange(0, ROT_DIM, 2, dtype=F32) / ROT_DIM)
    ang = jnp.arange(seq, dtype=jnp.int32).astype(F32)[:, None] * inv_freq[None, :]
    cos, sin = jnp.cos(ang), jnp.sin(ang)
    ones = jnp.ones((seq, HEAD_DIM - ROT_DIM), F32)
    zeros_h = jnp.zeros((seq, half), F32)
    zeros_r = jnp.zeros((seq, HEAD_DIM - ROT_DIM), F32)
    cos_t = jnp.concatenate([cos, cos, ones], axis=1)
    sa_t = jnp.concatenate([-sin, zeros_h, zeros_r], axis=1)
    sb_t = jnp.concatenate([zeros_h, sin, zeros_r], axis=1)
    rep = LANES // HEAD_DIM
    return tuple(jnp.tile(t, (1, rep)) for t in (cos_t, sa_t, sb_t))


def _proj(x2, g1, w_in, seq):
    n, d = x2.shape
    tm = PROJ_ROWS
    tabs = _rotary_tables(seq)
    per_seq = seq // tm
    tab_spec = pl.BlockSpec((tm, LANES), lambda i: (i % per_seq, 0))
    return pl.pallas_call(
        _proj_body,
        out_shape=(jax.ShapeDtypeStruct((n, QKV), F32), jax.ShapeDtypeStruct((n, QKV), BF16)),
        grid=(n // tm,),
        in_specs=[pl.BlockSpec((tm, d), lambda i: (i, 0)),
                  _const_spec((1, d)),
                  _const_spec((d, 2 * QKV)),
                  tab_spec, tab_spec, tab_spec],
        out_specs=(pl.BlockSpec((tm, QKV), lambda i: (i, 0)),
                   pl.BlockSpec((tm, QKV), lambda i: (i, 0))),
        compiler_params=_params("parallel"),
        name="proj",
    )(x2, g1, w_in, *tabs)


def _dil_body(q_ref, k_ref, v_ref, o_ref, os_ref, ls_ref, *, seq):
    heads = LANES // HEAD_DIM
    lane = lax.broadcasted_iota(I32, (1, LANES), 1)
    masks = [(lane >= h * HEAD_DIM) & (lane < (h + 1) * HEAD_DIM) for h in range(heads)]
    qi = lax.broadcasted_iota(I32, (BAND, 2 * BAND), 0)
    ki = lax.broadcasted_iota(I32, (BAND, 2 * BAND), 1)
    neg = jnp.float32(-1e30)
    band_bias = jnp.where((ki >= qi) & (ki <= qi + BAND), 0.0, neg)
    first_bias = jnp.where(ki >= BAND, 0.0, neg)
    nt = (((1,), (1,)), ((), ()))
    n_blocks = seq // BAND

    def rows(ref, start, dil):
        if dil == 1:
            return ref[0, pl.ds(start, BAND), :]
        return ref[0, pl.ds(start, BAND, stride=dil), :]

    for g, (_, dil) in enumerate(DILATED_GROUPS):
        nb = n_blocks // dil

        def block(n, u, g=g, dil=dil, nb=nb):
            r = n // nb
            i = n % nb
            cur = r + i * (BAND * dil)
            prev = r + jnp.maximum(i - 1, 0) * (BAND * dil)
            q = rows(q_ref, cur, dil).astype(BF16)
            kc = jnp.concatenate([rows(k_ref, prev, dil), rows(k_ref, cur, dil)], axis=0).astype(BF16)
            vc = jnp.concatenate([rows(v_ref, prev, dil), rows(v_ref, cur, dil)], axis=0).astype(BF16)
            if u % nb == 0:
                bias = band_bias + jnp.where(i > 0, 0.0, 1.0) * first_bias
            else:
                bias = band_bias
            o_t = jnp.zeros((BAND, LANES), F32)
            l_t = jnp.zeros((BAND, LANES), F32)
            for h in range(heads):
                qh = jnp.where(masks[h], q, jnp.zeros_like(q))
                s = lax.dot_general(qh, kc, nt, preferred_element_type=F32) + bias
                m = jnp.max(s, axis=-1, keepdims=True)
                p = jnp.exp(s - m)
                l = jnp.sum(p, axis=-1, keepdims=True)
                pv = jnp.dot(p.astype(BF16), vc, preferred_element_type=F32)
                o_t = jnp.where(masks[h], pv / l, o_t)
                l_t = jnp.where(masks[h], m + jnp.log(l), l_t)
            if dil == 1:
                os_ref[g, pl.ds(cur, BAND), :] = o_t
                ls_ref[g, pl.ds(cur, BAND), :] = l_t
            else:
                os_ref[g, pl.ds(cur, BAND, stride=dil), :] = o_t
                ls_ref[g, pl.ds(cur, BAND, stride=dil), :] = l_t

        def step(t, carry, block=block):
            for u in range(DIL_UNROLL):
                block(t * DIL_UNROLL + u, u)
            return carry

        lax.fori_loop(0, n_blocks // DIL_UNROLL, step, 0)

    def mix(c, carry):
        r0 = pl.multiple_of(c * MIX_ROWS, MIX_ROWS)
        ls = [ls_ref[g, pl.ds(r0, MIX_ROWS), :] for g in range(len(DILATED_GROUPS))]
        m = functools.reduce(jnp.maximum, ls)
        ws = [jnp.exp(l - m) for l in ls]
        num = sum(os_ref[g, pl.ds(r0, MIX_ROWS), :] * w for g, w in enumerate(ws))
        o_ref[0, pl.ds(r0, MIX_ROWS), :] = (num / sum(ws)).astype(BF16)
        return carry

    lax.fori_loop(0, seq // MIX_ROWS, mix, 0)


def _dilated(qkv_a, bsz, seq):
    assert all(w == BAND * d for w, d in DILATED_GROUPS)
    n_blocks = seq // BAND
    assert seq % BAND == 0 and n_blocks % DIL_UNROLL == 0 and seq % MIX_ROWS == 0
    for _, d in DILATED_GROUPS:
        nb = n_blocks // d
        assert n_blocks % d == 0 and (nb % DIL_UNROLL == 0 or DIL_UNROLL % nb == 0)
    a3 = qkv_a.reshape(bsz, seq, QKV)
    hb = WIDTH // LANES
    n_groups = len(DILATED_GROUPS)
    spec = lambda off: pl.BlockSpec((1, seq, LANES), lambda b, h: (b, 0, off * hb + h))
    return pl.pallas_call(
        functools.partial(_dil_body, seq=seq),
        out_shape=jax.ShapeDtypeStruct((bsz, seq, WIDTH), BF16),
        grid=(bsz, hb),
        in_specs=[spec(0), spec(1), spec(2)],
        out_specs=pl.BlockSpec((1, seq, LANES), lambda b, h: (b, 0, h)),
        scratch_shapes=[pltpu.VMEM((n_groups, seq, LANES), F32),
                        pltpu.VMEM((n_groups, seq, LANES), F32)],
        compiler_params=_params("parallel", "parallel"),
        name="dilated",
    )(a3, a3, a3).reshape(bsz * seq, WIDTH)


def _stick_body(q_ref, k_ref, v_ref, o_ref, acc_ref, carry_ref, *, tq, tk):
    i = pl.program_id(2)
    heads = LANES // HEAD_DIM
    lane = lax.broadcasted_iota(I32, (1, LANES), 1)
    masks = [(lane >= h * HEAD_DIM) & (lane < (h + 1) * HEAD_DIM) for h in range(heads)]
    lr = lax.broadcasted_iota(I32, (tk, tk), 0)
    lc = lax.broadcasted_iota(I32, (tk, tk), 1)
    later = (lr > lc).astype(BF16)
    nt = (((1,), (1,)), ((), ()))
    q = q_ref[0]
    qh = [jnp.where(mk, q, jnp.zeros_like(q)) for mk in masks]
    acc_ref[...] = jnp.zeros_like(acc_ref)
    carry_ref[...] = jnp.zeros_like(carry_ref)

    def tile(j, diag):
        r0 = pl.multiple_of(j * tk, tk)
        kj = k_ref[0, pl.ds(r0, tk), :]
        vj = v_ref[0, pl.ds(r0, tk), :]
        if diag:
            ti = lax.broadcasted_iota(I32, (tq, tk), 0)
            si = lax.broadcasted_iota(I32, (tq, tk), 1)
            causal = (r0 + si) < (i * tq + ti)
        pv = []
        for h in range(heads):
            z = lax.dot_general(qh[h], kj, nt, preferred_element_type=F32)
            neg_abs = pltpu.bitcast(pltpu.bitcast(z, jnp.uint32) | jnp.uint32(0x80000000), F32)
            sp = jnp.maximum(z, 0.0) + jnp.log2(1.0 + jnp.exp2(neg_abs))
            if diag:
                sp = jnp.where(causal, sp, 0.0)
            c = carry_ref[h]
            after = (jnp.dot(sp.astype(BF16), later, preferred_element_type=F32)
                     + jnp.concatenate([c] * (tk // LANES), axis=1))
            a = jnp.exp2(z - sp - after)
            if diag:
                a = jnp.where(causal, a, 0.0)
            vh = jnp.where(masks[h], vj, jnp.zeros_like(vj))
            pv.append(jnp.dot(a.astype(BF16), vh, preferred_element_type=F32))
            carry_ref[h] = c + jnp.broadcast_to(jnp.sum(sp, axis=-1, keepdims=True), (tq, LANES))
        acc_ref[...] += sum(pv)

    nd = tq // tk
    for dj in reversed(range(nd)):
        tile(i * nd + dj, True)

    def step(t, carry):
        tile(i * nd - 1 - t, False)
        return carry

    lax.fori_loop(0, i * nd, step, 0)
    o_ref[0] = acc_ref[...].astype(BF16)


def _stick(qkv_b, bsz, seq):
    tq, tk = STICK_ROWS, STICK_KEYS
    b3 = qkv_b.reshape(bsz, seq, QKV)
    hb = WIDTH // LANES
    return pl.pallas_call(
        functools.partial(_stick_body, tq=tq, tk=tk),
        out_shape=jax.ShapeDtypeStruct((bsz, seq, WIDTH), BF16),
        grid=(bsz, hb, seq // tq),
        in_specs=[pl.BlockSpec((1, tq, LANES), lambda b, h, i: (b, i, h)),
                  pl.BlockSpec((1, seq, LANES), lambda b, h, i: (b, 0, hb + h)),
                  pl.BlockSpec((1, seq, LANES), lambda b, h, i: (b, 0, 2 * hb + h))],
        out_specs=pl.BlockSpec((1, tq, LANES), lambda b, h, i: (b, i, h)),
        scratch_shapes=[pltpu.VMEM((tq, LANES), F32),
                        pltpu.VMEM((LANES // HEAD_DIM, tq, LANES), F32)],
        compiler_params=_params("parallel", "parallel", "parallel"),
        name="stick",
    )(b3, b3, b3).reshape(bsz * seq, WIDTH)


def _merge_body(x_ref, oa_ref, ob_ref,
                g1_ref, wg_ref, bg_ref, pa_ref, pb_ref, wo_ref, g2_ref, wr_ref, br_ref,
                x1_ref, h2_ref, ids_ref, rank_ref, gate_ref, cnt_ref, carry_ref):
    tm, d = x_ref.shape

    @pl.when(pl.program_id(0) == 0)
    def _():
        carry_ref[...] = jnp.zeros_like(carry_ref)

    x = x_ref[...]
    h = _rms(x, g1_ref[...]).astype(BF16)
    gate = jax.nn.sigmoid(jnp.dot(h, wg_ref[...], preferred_element_type=F32) + bg_ref[...])

    merged = (gate[:, :d] * jnp.dot(oa_ref[...], pa_ref[...], preferred_element_type=F32)
              + gate[:, d:] * jnp.dot(ob_ref[...], pb_ref[...], preferred_element_type=F32))
    x1 = x + jnp.dot(merged.astype(BF16), wo_ref[...], preferred_element_type=F32)
    x1_ref[...] = x1
    h2 = _rms(x1, g2_ref[...])
    h2_ref[...] = h2

    logits = jnp.dot(h2, wr_ref[...], preferred_element_type=F32,
                     precision=lax.Precision.HIGHEST) + br_ref[...]
    eidx = lax.broadcasted_iota(I32, (tm, N_EXPERTS), 1)
    work = logits
    tops, ids, hots = [], [], []
    for _ in range(TOP_K):
        mk = jnp.max(work, axis=-1, keepdims=True)
        ik = jnp.min(jnp.where(work == mk, eidx, N_EXPERTS), axis=-1, keepdims=True)
        hot = eidx == ik
        work = jnp.where(hot, -jnp.inf, work)
        tops.append(mk)
        ids.append(ik)
        hots.append(hot)
    exps = [jnp.exp(t - tops[0]) for t in tops]
    den = exps[0] + exps[1] + exps[2] + exps[3]
    gate_ref[...] = jnp.concatenate([e / den for e in exps], axis=1)
    ids_ref[...] = jnp.concatenate(ids, axis=1)

    cnt = sum(hot.astype(F32) for hot in hots)
    ri = lax.broadcasted_iota(I32, (tm, tm), 0)
    ci = lax.broadcasted_iota(I32, (tm, tm), 1)
    before = (ci < ri).astype(BF16)
    prefix = jnp.dot(before, cnt.astype(BF16), preferred_element_type=F32) + carry_ref[...]
    ranks = [jnp.sum(jnp.where(hot, prefix, 0.0), axis=-1, keepdims=True) for hot in hots]
    rank_ref[...] = jnp.concatenate(ranks, axis=1).astype(I32)
    total = carry_ref[...] + jnp.sum(cnt, axis=0, keepdims=True)
    carry_ref[...] = total
    cnt_ref[...] = total.astype(I32)


def _merge(x2, o_a, o_b, g1, w_gate, b_gate, w_pa, w_pb, w_out, g2, w_r, b_r):
    n, d = x2.shape
    tm = MERGE_ROWS
    row = lambda w: pl.BlockSpec((tm, w), lambda i: (i, 0))
    return pl.pallas_call(
        _merge_body,
        out_shape=(jax.ShapeDtypeStruct((n, d), F32),
                   jax.ShapeDtypeStruct((n, d), F32),
                   jax.ShapeDtypeStruct((n, TOP_K), I32),
                   jax.ShapeDtypeStruct((n, TOP_K), I32),
                   jax.ShapeDtypeStruct((n, TOP_K), F32),
                   jax.ShapeDtypeStruct((1, N_EXPERTS), I32)),
        grid=(n // tm,),
        in_specs=[row(d), row(WIDTH), row(WIDTH),
            _const_spec((1, d)), _const_spec((d, 2 * d)), _const_spec((1, 2 * d)),
            _const_spec((WIDTH, d)), _const_spec((WIDTH, d)), _const_spec((d, d)),
                  _const_spec((1, d)), _const_spec((d, N_EXPERTS)), _const_spec((1, N_EXPERTS))],
        out_specs=(row(d), row(d), row(TOP_K), row(TOP_K), row(TOP_K),
                   pl.BlockSpec((1, N_EXPERTS), lambda i: (0, 0))),
        scratch_shapes=[pltpu.VMEM((1, N_EXPERTS), F32)],
        compiler_params=_params("arbitrary"),
        name="merge",
    )(x2, o_a, o_b, g1, w_gate, b_gate, w_pa, w_pb, w_out, g2, w_r, b_r)


def _dest_body(ids_ref, rank_ref, start_ref, dest_ref):
    tm = ids_ref.shape[0]
    eidx = lax.broadcasted_iota(I32, (tm, N_EXPERTS), 1)
    ids = ids_ref[...]
    cols = []
    for k in range(TOP_K):
        hot = eidx == ids[:, k:k + 1]
        cols.append(jnp.sum(jnp.where(hot, start_ref[...], 0), axis=-1, keepdims=True))
    dest_ref[...] = jnp.concatenate(cols, axis=1) + rank_ref[...]


def _dest(ids, ranks, starts):
    n = ids.shape[0]
    tm = DEST_ROWS
    row = pl.BlockSpec((tm, TOP_K), lambda i: (i, 0))
    return pl.pallas_call(
        _dest_body,
        out_shape=jax.ShapeDtypeStruct((n, TOP_K), I32),
        grid=(n // tm,),
        in_specs=[row, row, _const_spec((1, N_EXPERTS))],
        out_specs=row,
        compiler_params=_params("parallel"),
        name="dest",
    )(ids, ranks, starts)


def _dispatch_body(pad_ref, nb_ref, dest_hbm, h_ref, xs_hbm, idx_ref, zero_ref, sem_idx, sem):
    tm = h_ref.shape[0]
    n_blocks = xs_hbm.shape[0] // EXPERT_ROWS
    i = pl.program_id(0)
    idx_cp = pltpu.make_async_copy(dest_hbm.at[pl.ds(i * tm * TOP_K, tm * TOP_K)], idx_ref, sem_idx)
    idx_cp.start()

    @pl.when(i == 0)
    def _():
        zero_ref[...] = jnp.zeros_like(zero_ref)
        for e in range(N_EXPERTS):
            r0 = pl.multiple_of(pad_ref[e], EXPERT_ROWS)
            pltpu.make_async_copy(zero_ref, xs_hbm.at[pl.ds(r0, EXPERT_ROWS)], sem).start()
        for e in range(N_EXPERTS):
            pltpu.make_async_copy(zero_ref, xs_hbm.at[pl.ds(0, EXPERT_ROWS)], sem).wait()

        def fill(b, carry):
            r0 = pl.multiple_of(b * EXPERT_ROWS, EXPERT_ROWS)
            cp = pltpu.make_async_copy(zero_ref, xs_hbm.at[pl.ds(r0, EXPERT_ROWS)], sem)
            cp.start()
            cp.wait()
            return carry

        lax.fori_loop(nb_ref[0], n_blocks, fill, 0)

    idx_cp.wait()

    def body(t, carry):
        for k in range(TOP_K):
            d = idx_ref[t * TOP_K + k]
            pltpu.make_async_copy(h_ref.at[pl.ds(t, 1)], xs_hbm.at[pl.ds(d, 1)], sem).start()
        return carry

    lax.fori_loop(0, tm, body, 0)
    for k in range(TOP_K):
        pltpu.make_async_copy(h_ref, xs_hbm.at[pl.ds(0, tm)], sem).wait()


def _dispatch(pad_start, n_used, dest_flat, h2, n_rows):
    n, d = h2.shape
    tm = DISPATCH_ROWS
    return pl.pallas_call(
        _dispatch_body,
        out_shape=jax.ShapeDtypeStruct((n_rows, d), h2.dtype),
        grid_spec=pltpu.PrefetchScalarGridSpec(
            num_scalar_prefetch=2,
            grid=(n // tm,),
            in_specs=[pl.BlockSpec(memory_space=pl.ANY),
                      pl.BlockSpec((tm, d), lambda i, pad, nb: (i, 0))],
            out_specs=pl.BlockSpec(memory_space=pl.ANY),
            scratch_shapes=[pltpu.SMEM((tm * TOP_K,), I32),
                            pltpu.VMEM((EXPERT_ROWS, d), h2.dtype),
                            pltpu.SemaphoreType.DMA,
                            pltpu.SemaphoreType.DMA]),
        compiler_params=_params("arbitrary"),
        name="dispatch",
    )(pad_start, n_used, dest_flat, h2)


def _expert_body(be_ref, nb_ref, x_ref, w1_ref, b1_ref, w2_ref, b2_ref, y_ref):
    f = w2_ref.shape[1]

    @pl.when(pl.program_id(0) < nb_ref[0])
    def _():
        xb = x_ref[...].astype(BF16)
        hid = jnp.dot(xb, w1_ref[0], preferred_element_type=F32) + b1_ref[0]
        glu = jnp.minimum(hid[:, :f], SWIGLU_LIMIT)
        lin = jnp.clip(hid[:, f:], -SWIGLU_LIMIT, SWIGLU_LIMIT)
        act = glu * jax.nn.sigmoid(SWIGLU_ALPHA * glu) * (lin + 1.0)
        y_ref[...] = jnp.dot(act.astype(BF16), w2_ref[0], preferred_element_type=F32) + b2_ref[0]

    @pl.when(pl.program_id(0) >= nb_ref[0])
    def _():
        y_ref[...] = jnp.zeros_like(y_ref)


def _experts(block_e, n_used, xs, w1, b1, w2, b2, n_blocks):
    d = xs.shape[1]
    f = w2.shape[1]
    br = EXPERT_ROWS
    blk = lambda b, be, nb: (jnp.minimum(b, nb[0] - 1), 0)
    per_e = lambda b, be, nb: (be[b], 0, 0)
    return pl.pallas_call(
        _expert_body,
        out_shape=jax.ShapeDtypeStruct((n_blocks * br, d), F32),
        grid_spec=pltpu.PrefetchScalarGridSpec(
            num_scalar_prefetch=2,
            grid=(n_blocks,),
            in_specs=[pl.BlockSpec((br, d), blk),
                      pl.BlockSpec((1, d, 2 * f), per_e),
                      pl.BlockSpec((1, 1, 2 * f), per_e),
                      pl.BlockSpec((1, f, d), per_e),
                      pl.BlockSpec((1, 1, d), per_e)],
            out_specs=pl.BlockSpec((br, d), lambda b, be, nb: (b, 0))),
        compiler_params=_params("arbitrary"),
        name="experts",
    )(block_e, n_used, xs, w1, b1, w2, b2)


def _combine_body(dest_hbm, x1_ref, gate_ref, g_ref, y_hbm, o_ref, idx_ref, buf_ref, sem_idx, sem,
                  *, final_norm):
    tm = x1_ref.shape[0]
    i = pl.program_id(0)
    idx_cp = pltpu.make_async_copy(dest_hbm.at[pl.ds(i * tm * TOP_K, tm * TOP_K)], idx_ref, sem_idx)
    idx_cp.start()
    idx_cp.wait()

    def body(t, carry):
        for k in range(TOP_K):
            d = idx_ref[t * TOP_K + k]
            pltpu.make_async_copy(y_hbm.at[pl.ds(d, 1)], buf_ref.at[k, pl.ds(t, 1)], sem).start()
        return carry

    lax.fori_loop(0, tm, body, 0)
    for k in range(TOP_K):
        pltpu.make_async_copy(y_hbm.at[pl.ds(0, tm)], buf_ref.at[k], sem).wait()

    gate = gate_ref[...]
    acc = x1_ref[...]
    for k in range(TOP_K):
        acc = acc + gate[:, k:k + 1] * buf_ref[k]
    o_ref[...] = _rms(acc, g_ref[...]) if final_norm else acc


def _combine(dest_flat, x1, gates, g_f, y, final_norm):
    n, d = x1.shape
    tm = COMBINE_ROWS
    return pl.pallas_call(
        functools.partial(_combine_body, final_norm=final_norm),
        out_shape=jax.ShapeDtypeStruct((n, d), F32),
        grid=(n // tm,),
        in_specs=[pl.BlockSpec(memory_space=pl.ANY),
                  pl.BlockSpec((tm, d), lambda i: (i, 0)),
                  pl.BlockSpec((tm, TOP_K), lambda i: (i, 0)),
                  _const_spec((1, d)),
                  pl.BlockSpec(memory_space=pl.ANY)],
        out_specs=pl.BlockSpec((tm, d), lambda i: (i, 0)),
        scratch_shapes=[pltpu.SMEM((tm * TOP_K,), I32),
                        pltpu.VMEM((TOP_K, tm, d), F32),
                        pltpu.SemaphoreType.DMA,
                        pltpu.SemaphoreType.DMA],
        compiler_params=_params("arbitrary"),
        name="combine",
    )(dest_flat, x1, gates, g_f, y)


def _layer(x2, bsz, seq, p):
    n, d = x2.shape
    qkv_a, qkv_b = _proj(x2, p["g1"], p["w_in"], seq)
    o_a = _dilated(qkv_a, bsz, seq)
    o_b = _stick(qkv_b, bsz, seq)
    x1, h2, ids, ranks, gates, counts = _merge(
        x2, o_a, o_b, p["g1"], p["w_gate"], p["b_gate"], p["w_pa"], p["w_pb"], p["w_out"],
        p["g2"], p["w_r"], p["b_r"])

    br = EXPERT_ROWS
    counts = counts[0]
    padded = ((counts + br - 1) // br) * br
    ends = jnp.cumsum(padded)
    starts = ends - padded
    n_blocks = (n * TOP_K) // br + N_EXPERTS
    block_row = jnp.arange(n_blocks, dtype=I32) * br
    block_e = jnp.minimum(jnp.sum(ends[None, :] <= block_row[:, None], axis=1),
                          N_EXPERTS - 1).astype(I32)
    n_used = (ends[-1:] // br).astype(I32)

    dest = _dest(ids, ranks, starts.reshape(1, N_EXPERTS).astype(I32)).reshape(n * TOP_K)
    xs = _dispatch((starts + (counts // br) * br).astype(I32), n_used, dest, h2, n_blocks * br)
    y = _experts(block_e, n_used, xs, p["w1"], p["b1"], p["w2"], p["b2"], n_blocks)
    return dest, x1, gates, y


def kernel(x, norm1_g, w_in, w_proj_a, w_proj_b, w_gate, b_gate, w_out, norm2_g,
           w_router, b_router, w1, b1, w2, b2, norm_f_g):
    bsz, seq, d = x.shape
    depth = w_in.shape[0]
    x2 = x.reshape(bsz * seq, d)
    for l in range(depth):
        p = dict(
            g1=norm1_g[l].reshape(1, d), w_in=w_in[l].astype(BF16),
            w_gate=w_gate[l].astype(BF16), b_gate=b_gate[l].reshape(1, 2 * d),
            w_pa=w_proj_a[l].astype(BF16), w_pb=w_proj_b[l].astype(BF16),
            w_out=w_out[l].astype(BF16), g2=norm2_g[l].reshape(1, d),
            w_r=w_router[l], b_r=b_router[l].reshape(1, N_EXPERTS),
            w1=w1[l].astype(BF16), b1=b1[l].reshape(N_EXPERTS, 1, -1),
            w2=w2[l].astype(BF16), b2=b2[l].reshape(N_EXPERTS, 1, -1))
        dest, x1, gates, y = _layer(x2, bsz, seq, p)
        x2 = _combine(dest, x1, gates, norm_f_g.reshape(1, d), y, final_norm=(l == depth - 1))
    return x2.reshape(bsz, seq, d)
```

```python
import functools

import jax
import jax.numpy as jnp
from jax import lax
from jax.experimental import pallas as pl
from jax.experimental.pallas import tpu as pltpu

F32 = jnp.float32
BF16 = jnp.bfloat16
I32 = jnp.int32

HEAD_DIM = 64
WIDTH = 512
QKV = 3 * WIDTH
DILATED_GROUPS = ((128, 1), (512, 4), (2048, 16))
BAND = 128
ROPE_THETA = 500000.0
ROT_DIM = HEAD_DIM // 4
N_EXPERTS = 32
TOP_K = 4
SWIGLU_ALPHA = 1.702
SWIGLU_LIMIT = 7.0
NORM_EPS = 1e-5

LANES = 128
VMEM_LIMIT = 56 * 1024 * 1024

LOG2E = 1.4426950408889634

PROJ_ROWS = 512
MERGE_ROWS = 512
STICK_ROWS = 512
STICK_KEYS = 256
EXPERT_ROWS = 512
DISPATCH_ROWS = 512
COMBINE_ROWS = 256
DEST_ROWS = 2048
DIL_UNROLL = 8
MIX_ROWS = 512


def _rms(x, g):
    return x * lax.rsqrt(jnp.mean(x * x, axis=-1, keepdims=True) + NORM_EPS) * g


def _params(*sem):
    return pltpu.CompilerParams(dimension_semantics=sem, vmem_limit_bytes=VMEM_LIMIT)


def _const_spec(shape):
    return pl.BlockSpec(shape, lambda *_: (0,) * len(shape), pipeline_mode=pl.Buffered(1))


def _proj_body(x_ref, g_ref, w_ref, cos_ref, sa_ref, sb_ref, a_ref, b_ref):
    h = _rms(x_ref[...], g_ref[...]).astype(BF16)
    rep = WIDTH // LANES
    cos = jnp.concatenate([cos_ref[...]] * rep, axis=1)
    sa = jnp.concatenate([sa_ref[...]] * rep, axis=1)
    sb = jnp.concatenate([sb_ref[...]] * rep, axis=1)
    half = ROT_DIM // 2
    for c in range(6):
        acc = jnp.dot(h, w_ref[:, c * WIDTH:(c + 1) * WIDTH], preferred_element_type=F32)
        if c in (0, 1):
            acc = (acc * cos + pltpu.roll(acc, WIDTH - half, 1) * sa
                   + pltpu.roll(acc, half, 1) * sb)
        if c == 0:
            acc = acc * (HEAD_DIM ** -0.5)
        if c == 3:
            acc = acc * (HEAD_DIM ** -0.5 * LOG2E)
        dst = a_ref if c < 3 else b_ref
        dst[:, (c % 3) * WIDTH:(c % 3 + 1) * WIDTH] = acc.astype(dst.dtype)


def _rotary_tables(seq):
    half = ROT_DIM // 2
    inv_freq = jnp.float32(ROPE_THETA) ** (-jnp.arange(0, ROT_DIM, 2, dtype=F32) / ROT_DIM)
    ang = jnp.arange(seq, dtype=jnp.int32).astype(F32)[:, None] * inv_freq[None, :]
    cos, sin = jnp.cos(ang), jnp.sin(ang)
    ones = jnp.ones((seq, HEAD_DIM - ROT_DIM), F32)
    zeros_h = jnp.zeros((seq, half), F32)
    zeros_r = jnp.zeros((seq, HEAD_DIM - ROT_DIM), F32)
    cos_t = jnp.concatenate([cos, cos, ones], axis=1)
    sa_t = jnp.concatenate([-sin, zeros_h, zeros_r], axis=1)
    sb_t = jnp.concatenate([zeros_h, sin, zeros_r], axis=1)
    rep = LANES // HEAD_DIM
    return tuple(jnp.tile(t, (1, rep)) for t in (cos_t, sa_t, sb_t))


def _proj(x2, g1, w_in, seq):
    n, d = x2.shape
    tm = PROJ_ROWS
    tabs = _rotary_tables(seq)
    per_seq = seq // tm
    tab_spec = pl.BlockSpec((tm, LANES), lambda i: (i % per_seq, 0))
    return pl.pallas_call(
        _proj_body,
        out_shape=(jax.ShapeDtypeStruct((n, QKV), F32), jax.ShapeDtypeStruct((n, QKV), BF16)),
        grid=(n // tm,),
        in_specs=[pl.BlockSpec((tm, d), lambda i: (i, 0)),
                  _const_spec((1, d)),
                  _const_spec((d, 2 * QKV)),
                  tab_spec, tab_spec, tab_spec],
        out_specs=(pl.BlockSpec((tm, QKV), lambda i: (i, 0)),
                   pl.BlockSpec((tm, QKV), lambda i: (i, 0))),
        compiler_params=_params("parallel"),
        name="proj",
    )(x2, g1, w_in, *tabs)


def _dil_body(q_ref, k_ref, v_ref, o_ref, os_ref, ls_ref, *, seq):
    heads = LANES // HEAD_DIM
    lane = lax.broadcasted_iota(I32, (1, LANES), 1)
    masks = [(lane >= h * HEAD_DIM) & (lane < (h + 1) * HEAD_DIM) for h in range(heads)]
    qi = lax.broadcasted_iota(I32, (BAND, 2 * BAND), 0)
    ki = lax.broadcasted_iota(I32, (BAND, 2 * BAND), 1)
    neg = jnp.float32(-1e30)
    band_bias = jnp.where((ki >= qi) & (ki <= qi + BAND), 0.0, neg)
    first_bias = jnp.where(ki >= BAND, 0.0, neg)
    nt = (((1,), (1,)), ((), ()))
    n_blocks = seq // BAND

    def rows(ref, start, dil):
        if dil == 1:
            return ref[0, pl.ds(start, BAND), :]
        return ref[0, pl.ds(start, BAND, stride=dil), :]

    for g, (_, dil) in enumerate(DILATED_GROUPS):
        nb = n_blocks // dil

        def block(n, u, g=g, dil=dil, nb=nb):
            r = n // nb
            i = n % nb
            cur = r + i * (BAND * dil)
            prev = r + jnp.maximum(i - 1, 0) * (BAND * dil)
            q = rows(q_ref, cur, dil).astype(BF16)
            kc = jnp.concatenate([rows(k_ref, prev, dil), rows(k_ref, cur, dil)], axis=0).astype(BF16)
            vc = jnp.concatenate([rows(v_ref, prev, dil), rows(v_ref, cur, dil)], axis=0).astype(BF16)
            if u % nb == 0:
                bias = band_bias + jnp.where(i > 0, 0.0, 1.0) * first_bias
            else:
                bias = band_bias
            o_t = jnp.zeros((BAND, LANES), F32)
            l_t = jnp.zeros((BAND, LANES), F32)
            for h in range(heads):
                qh = jnp.where(masks[h], q, jnp.zeros_like(q))
                s = lax.dot_general(qh, kc, nt, preferred_element_type=F32) + bias
                m = jnp.max(s, axis=-1, keepdims=True)
                p = jnp.exp(s - m)
                l = jnp.sum(p, axis=-1, keepdims=True)
                pv = jnp.dot(p.astype(BF16), vc, preferred_element_type=F32)
                o_t = jnp.where(masks[h], pv / l, o_t)
                l_t = jnp.where(masks[h], m + jnp.log(l), l_t)
            if dil == 1:
                os_ref[g, pl.ds(cur, BAND), :] = o_t
                ls_ref[g, pl.ds(cur, BAND), :] = l_t
            else:
                os_ref[g, pl.ds(cur, BAND, stride=dil), :] = o_t
                ls_ref[g, pl.ds(cur, BAND, stride=dil), :] = l_t

        def step(t, carry, block=block):
            for u in range(DIL_UNROLL):
                block(t * DIL_UNROLL + u, u)
            return carry

        lax.fori_loop(0, n_blocks // DIL_UNROLL, step, 0)

    def mix(c, carry):
        r0 = pl.multiple_of(c * MIX_ROWS, MIX_ROWS)
        ls = [ls_ref[g, pl.ds(r0, MIX_ROWS), :] for g in range(len(DILATED_GROUPS))]
        m = functools.reduce(jnp.maximum, ls)
        ws = [jnp.exp(l - m) for l in ls]
        num = sum(os_ref[g, pl.ds(r0, MIX_ROWS), :] * w for g, w in enumerate(ws))
        o_ref[0, pl.ds(r0, MIX_ROWS), :] = (num / sum(ws)).astype(BF16)
        return carry

    lax.fori_loop(0, seq // MIX_ROWS, mix, 0)


def _dilated(qkv_a, bsz, seq):
    assert all(w == BAND * d for w, d in DILATED_GROUPS)
    n_blocks = seq // BAND
    assert seq % BAND == 0 and n_blocks % DIL_UNROLL == 0 and seq % MIX_ROWS == 0
    for _, d in DILATED_GROUPS:
        nb = n_blocks // d
        assert n_blocks % d == 0 and (nb % DIL_UNROLL == 0 or DIL_UNROLL % nb == 0)
    a3 = qkv_a.reshape(bsz, seq, QKV)
    hb = WIDTH // LANES
    n_groups = len(DILATED_GROUPS)
    spec = lambda off: pl.BlockSpec((1, seq, LANES), lambda b, h: (b, 0, off * hb + h))
    return pl.pallas_call(
        functools.partial(_dil_body, seq=seq),
        out_shape=jax.ShapeDtypeStruct((bsz, seq, WIDTH), BF16),
        grid=(bsz, hb),
        in_specs=[spec(0), spec(1), spec(2)],
        out_specs=pl.BlockSpec((1, seq, LANES), lambda b, h: (b, 0, h)),
        scratch_shapes=[pltpu.VMEM((n_groups, seq, LANES), F32),
                        pltpu.VMEM((n_groups, seq, LANES), F32)],
        compiler_params=_params("parallel", "parallel"),
        name="dilated",
    )(a3, a3, a3).reshape(bsz * seq, WIDTH)


def _stick_body(q_ref, k_ref, v_ref, o_ref, acc_ref, carry_ref, *, tq, tk):
    i = pl.program_id(2)
    heads = LANES // HEAD_DIM
    lane = lax.broadcasted_iota(I32, (1, LANES), 1)
    masks = [(lane >= h * HEAD_DIM) & (lane < (h + 1) * HEAD_DIM) for h in range(heads)]
    lr = lax.broadcasted_iota(I32, (tk, tk), 0)
    lc = lax.broadcasted_iota(I32, (tk, tk), 1)
    later = (lr > lc).astype(BF16)
    nt = (((1,), (1,)), ((), ()))
    q = q_ref[0]
    qh = [jnp.where(mk, q, jnp.zeros_like(q)) for mk in masks]
    acc_ref[...] = jnp.zeros_like(acc_ref)
    carry_ref[...] = jnp.zeros_like(carry_ref)

    def tile(j, diag):
        r0 = pl.multiple_of(j * tk, tk)
        kj = k_ref[0, pl.ds(r0, tk), :]
        vj = v_ref[0, pl.ds(r0, tk), :]
        if diag:
            ti = lax.broadcasted_iota(I32, (tq, tk), 0)
            si = lax.broadcasted_iota(I32, (tq, tk), 1)
            causal = (r0 + si) < (i * tq + ti)
        pv = []
        for h in range(heads):
            z = lax.dot_general(qh[h], kj, nt, preferred_element_type=F32)
            neg_abs = pltpu.bitcast(pltpu.bitcast(z, jnp.uint32) | jnp.uint32(0x80000000), F32)
            sp = jnp.maximum(z, 0.0) + jnp.log2(1.0 + jnp.exp2(neg_abs))
            if diag:
                sp = jnp.where(causal, sp, 0.0)
            c = carry_ref[h]
            after = (jnp.dot(sp.astype(BF16), later, preferred_element_type=F32)
                     + jnp.concatenate([c] * (tk // LANES), axis=1))
            a = jnp.exp2(z - sp - after)
            if diag:
                a = jnp.where(causal, a, 0.0)
            vh = jnp.where(masks[h], vj, jnp.zeros_like(vj))
            pv.append(jnp.dot(a.astype(BF16), vh, preferred_element_type=F32))
            carry_ref[h] = c + jnp.broadcast_to(jnp.sum(sp, axis=-1, keepdims=True), (tq, LANES))
        acc_ref[...] += sum(pv)

    nd = tq // tk
    for dj in reversed(range(nd)):
        tile(i * nd + dj, True)

    def step(t, carry):
        tile(i * nd - 1 - t, False)
        return carry

    lax.fori_loop(0, i * nd, step, 0)
    o_ref[0] = acc_ref[...].astype(BF16)


def _stick(qkv_b, bsz, seq):
    tq, tk = STICK_ROWS, STICK_KEYS
    b3 = qkv_b.reshape(bsz, seq, QKV)
    hb = WIDTH // LANES
    return pl.pallas_call(
        functools.partial(_stick_body, tq=tq, tk=tk),
        out_shape=jax.ShapeDtypeStruct((bsz, seq, WIDTH), BF16),
        grid=(bsz, hb, seq // tq),
        in_specs=[pl.BlockSpec((1, tq, LANES), lambda b, h, i: (b, i, h)),
                  pl.BlockSpec((1, seq, LANES), lambda b, h, i: (b, 0, hb + h)),
                  pl.BlockSpec((1, seq, LANES), lambda b, h, i: (b, 0, 2 * hb + h))],
        out_specs=pl.BlockSpec((1, tq, LANES), lambda b, h, i: (b, i, h)),
        scratch_shapes=[pltpu.VMEM((tq, LANES), F32),
                        pltpu.VMEM((LANES // HEAD_DIM, tq, LANES), F32)],
        compiler_params=_params("parallel", "parallel", "parallel"),
        name="stick",
    )(b3, b3, b3).reshape(bsz * seq, WIDTH)


def _merge_body(x_ref, oa_ref, ob_ref,
                g1_ref, wg_ref, bg_ref, pa_ref, pb_ref, wo_ref, g2_ref, wrh_ref, wrl_ref, br_ref,
                x1_ref, h2_ref, ids_ref, rank_ref, gate_ref, cnt_ref, carry_ref):
    tm, d = x_ref.shape

    @pl.when(pl.program_id(0) == 0)
    def _():
        carry_ref[...] = jnp.zeros_like(carry_ref)

    def dot(a, w_ref, lo, hi):
        return jnp.dot(a, w_ref[:, lo:hi], preferred_element_type=F32)

    x = x_ref[...]
    h = _rms(x, g1_ref[...]).astype(BF16)
    gate_a = jax.nn.sigmoid(dot(h, wg_ref, 0, d) + bg_ref[:, :d])
    gate_b = jax.nn.sigmoid(dot(h, wg_ref, d, 2 * d) + bg_ref[:, d:])
    merged = (gate_a * dot(oa_ref[...], pa_ref, 0, d)
              + gate_b * dot(ob_ref[...], pb_ref, 0, d)).astype(BF16)
    x1 = x + jnp.concatenate([dot(merged, wo_ref, 0, d // 2), dot(merged, wo_ref, d // 2, d)], axis=1)
    x1_ref[...] = x1
    h2 = _rms(x1, g2_ref[...])
    h2_ref[...] = h2

    h2_hi = h2.astype(BF16)
    h2_lo = (h2 - h2_hi.astype(F32)).astype(BF16)
    logits = (dot(h2_hi, wrh_ref, 0, N_EXPERTS) + dot(h2_lo, wrh_ref, 0, N_EXPERTS)
              + dot(h2_hi, wrl_ref, 0, N_EXPERTS) + br_ref[...])
    eidx = lax.broadcasted_iota(I32, (tm, N_EXPERTS), 1)
    work = logits
    tops, ids, hots = [], [], []
    for _ in range(TOP_K):
        mk = jnp.max(work, axis=-1, keepdims=True)
        ik = jnp.min(jnp.where(work == mk, eidx, N_EXPERTS), axis=-1, keepdims=True)
        hot = eidx == ik
        work = jnp.where(hot, -jnp.inf, work)
        tops.append(mk)
        ids.append(ik)
        hots.append(hot)
    exps = [jnp.exp(t - tops[0]) for t in tops]
    den = exps[0] + exps[1] + exps[2] + exps[3]
    gate_ref[...] = jnp.concatenate([e / den for e in exps], axis=1)
    ids_ref[...] = jnp.concatenate(ids, axis=1)

    cnt = sum(hot.astype(F32) for hot in hots)
    ri = lax.broadcasted_iota(I32, (tm, tm), 0)
    ci = lax.broadcasted_iota(I32, (tm, tm), 1)
    before = (ci < ri).astype(BF16)
    prefix = jnp.dot(before, cnt.astype(BF16), preferred_element_type=F32) + carry_ref[...]
    ranks = [jnp.sum(jnp.where(hot, prefix, 0.0), axis=-1, keepdims=True) for hot in hots]
    rank_ref[...] = jnp.concatenate(ranks, axis=1).astype(I32)
    total = carry_ref[...] + jnp.sum(cnt, axis=0, keepdims=True)
    carry_ref[...] = total
    cnt_ref[...] = total.astype(I32)


def _merge(x2, o_a, o_b, g1, w_gate, b_gate, w_pa, w_pb, w_out, g2, w_r, b_r):
    n, d = x2.shape
    w_r_hi = w_r.astype(BF16)
    w_r_lo = (w_r - w_r_hi.astype(F32)).astype(BF16)
    tm = MERGE_ROWS
    row = lambda w: pl.BlockSpec((tm, w), lambda i: (i, 0))
    return pl.pallas_call(
        _merge_body,
        out_shape=(jax.ShapeDtypeStruct((n, d), F32),
                   jax.ShapeDtypeStruct((n, d), F32),
                   jax.ShapeDtypeStruct((n, TOP_K), I32),
                   jax.ShapeDtypeStruct((n, TOP_K), I32),
                   jax.ShapeDtypeStruct((n, TOP_K), F32),
                   jax.ShapeDtypeStruct((1, N_EXPERTS), I32)),
        grid=(n // tm,),
        in_specs=[row(d), row(WIDTH), row(WIDTH),
                  _const_spec((1, d)), _const_spec((d, 2 * d)), _const_spec((1, 2 * d)),
                  _const_spec((WIDTH, d)), _const_spec((WIDTH, d)), _const_spec((d, d)),
                  _const_spec((1, d)), _const_spec((d, N_EXPERTS)), _const_spec((d, N_EXPERTS)),
                  _const_spec((1, N_EXPERTS))],
        out_specs=(row(d), row(d), row(TOP_K), row(TOP_K), row(TOP_K),
                   pl.BlockSpec((1, N_EXPERTS), lambda i: (0, 0))),
        scratch_shapes=[pltpu.VMEM((1, N_EXPERTS), F32)],
        compiler_params=_params("arbitrary"),
        name="merge",
    )(x2, o_a, o_b, g1, w_gate, b_gate, w_pa, w_pb, w_out, g2, w_r_hi, w_r_lo, b_r)


def _dest_body(ids_ref, rank_ref, start_ref, dest_ref):
    tm = ids_ref.shape[0]
    eidx = lax.broadcasted_iota(I32, (tm, N_EXPERTS), 1)
    ids = ids_ref[...]
    cols = []
    for k in range(TOP_K):
        hot = eidx == ids[:, k:k + 1]
        cols.append(jnp.sum(jnp.where(hot, start_ref[...], 0), axis=-1, keepdims=True))
    dest_ref[...] = jnp.concatenate(cols, axis=1) + rank_ref[...]


def _dest(ids, ranks, starts):
    n = ids.shape[0]
    tm = DEST_ROWS
    row = pl.BlockSpec((tm, TOP_K), lambda i: (i, 0))
    return pl.pallas_call(
        _dest_body,
        out_shape=jax.ShapeDtypeStruct((n, TOP_K), I32),
        grid=(n // tm,),
        in_specs=[row, row, _const_spec((1, N_EXPERTS))],
        out_specs=row,
        compiler_params=_params("parallel"),
        name="dest",
    )(ids, ranks, starts)


def _dispatch_body(pad_ref, nb_ref, dest_hbm, h_ref, xs_hbm, idx_ref, zero_ref, sem_idx, sem):
    tm = h_ref.shape[0]
    n_blocks = xs_hbm.shape[0] // EXPERT_ROWS
    i = pl.program_id(0)
    idx_cp = pltpu.make_async_copy(dest_hbm.at[pl.ds(i * tm * TOP_K, tm * TOP_K)], idx_ref, sem_idx)
    idx_cp.start()

    @pl.when(i == 0)
    def _():
        zero_ref[...] = jnp.zeros_like(zero_ref)
        for e in range(N_EXPERTS):
            r0 = pl.multiple_of(pad_ref[e], EXPERT_ROWS)
            pltpu.make_async_copy(zero_ref, xs_hbm.at[pl.ds(r0, EXPERT_ROWS)], sem).start()
        for e in range(N_EXPERTS):
            pltpu.make_async_copy(zero_ref, xs_hbm.at[pl.ds(0, EXPERT_ROWS)], sem).wait()

        def fill(b, carry):
            r0 = pl.multiple_of(b * EXPERT_ROWS, EXPERT_ROWS)
            cp = pltpu.make_async_copy(zero_ref, xs_hbm.at[pl.ds(r0, EXPERT_ROWS)], sem)
            cp.start()
            cp.wait()
            return carry

        lax.fori_loop(nb_ref[0], n_blocks, fill, 0)

    idx_cp.wait()

    def body(t, carry):
        for k in range(TOP_K):
            d = idx_ref[t * TOP_K + k]
            pltpu.make_async_copy(h_ref.at[pl.ds(t, 1)], xs_hbm.at[pl.ds(d, 1)], sem).start()
        return carry

    lax.fori_loop(0, tm, body, 0)
    for k in range(TOP_K):
        pltpu.make_async_copy(h_ref, xs_hbm.at[pl.ds(0, tm)], sem).wait()


def _dispatch(pad_start, n_used, dest_flat, h2, n_rows):
    n, d = h2.shape
    tm = DISPATCH_ROWS
    return pl.pallas_call(
        _dispatch_body,
        out_shape=jax.ShapeDtypeStruct((n_rows, d), h2.dtype),
        grid_spec=pltpu.PrefetchScalarGridSpec(
            num_scalar_prefetch=2,
            grid=(n // tm,),
            in_specs=[pl.BlockSpec(memory_space=pl.ANY),
                      pl.BlockSpec((tm, d), lambda i, pad, nb: (i, 0))],
            out_specs=pl.BlockSpec(memory_space=pl.ANY),
            scratch_shapes=[pltpu.SMEM((tm * TOP_K,), I32),
                            pltpu.VMEM((EXPERT_ROWS, d), h2.dtype),
                            pltpu.SemaphoreType.DMA,
                            pltpu.SemaphoreType.DMA]),
        compiler_params=_params("arbitrary"),
        name="dispatch",
    )(pad_start, n_used, dest_flat, h2)


def _expert_body(be_ref, nb_ref, x_ref, w1_ref, b1_ref, w2_ref, b2_ref, y_ref):
    f = w2_ref.shape[1]

    @pl.when(pl.program_id(0) < nb_ref[0])
    def _():
        xb = x_ref[...].astype(BF16)
        hid = jnp.dot(xb, w1_ref[0], preferred_element_type=F32) + b1_ref[0]
        glu = jnp.minimum(hid[:, :f], SWIGLU_LIMIT)
        lin = jnp.clip(hid[:, f:], -SWIGLU_LIMIT, SWIGLU_LIMIT)
        act = glu * jax.nn.sigmoid(SWIGLU_ALPHA * glu) * (lin + 1.0)
        y_ref[...] = jnp.dot(act.astype(BF16), w2_ref[0], preferred_element_type=F32) + b2_ref[0]

    @pl.when(pl.program_id(0) >= nb_ref[0])
    def _():
        y_ref[...] = jnp.zeros_like(y_ref)


def _experts(block_e, n_used, xs, w1, b1, w2, b2, n_blocks):
    d = xs.shape[1]
    f = w2.shape[1]
    br = EXPERT_ROWS
    blk = lambda b, be, nb: (jnp.minimum(b, nb[0] - 1), 0)
    per_e = lambda b, be, nb: (be[b], 0, 0)
    return pl.pallas_call(
        _expert_body,
        out_shape=jax.ShapeDtypeStruct((n_blocks * br, d), F32),
        grid_spec=pltpu.PrefetchScalarGridSpec(
            num_scalar_prefetch=2,
            grid=(n_blocks,),
            in_specs=[pl.BlockSpec((br, d), blk),
                      pl.BlockSpec((1, d, 2 * f), per_e),
                      pl.BlockSpec((1, 1, 2 * f), per_e),
                      pl.BlockSpec((1, f, d), per_e),
                      pl.BlockSpec((1, 1, d), per_e)],
            out_specs=pl.BlockSpec((br, d), lambda b, be, nb: (b, 0))),
        compiler_params=_params("arbitrary"),
        name="experts",
    )(block_e, n_used, xs, w1, b1, w2, b2)


def _combine_body(dest_hbm, x1_ref, gate_ref, g_ref, y_hbm, o_ref, idx_ref, buf_ref, sem_idx, sem,
                  *, final_norm):
    tm = x1_ref.shape[0]
    i = pl.program_id(0)
    idx_cp = pltpu.make_async_copy(dest_hbm.at[pl.ds(i * tm * TOP_K, tm * TOP_K)], idx_ref, sem_idx)
    idx_cp.start()
    idx_cp.wait()

    def body(t, carry):
        for k in range(TOP_K):
            d = idx_ref[t * TOP_K + k]
            pltpu.make_async_copy(y_hbm.at[pl.ds(d, 1)], buf_ref.at[k, pl.ds(t, 1)], sem).start()
        return carry

    lax.fori_loop(0, tm, body, 0)
    for k in range(TOP_K):
        pltpu.make_async_copy(y_hbm.at[pl.ds(0, tm)], buf_ref.at[k], sem).wait()

    gate = gate_ref[...]
    acc = x1_ref[...]
    for k in range(TOP_K):
        acc = acc + gate[:, k:k + 1] * buf_ref[k]
    o_ref[...] = _rms(acc, g_ref[...]) if final_norm else acc


def _combine(dest_flat, x1, gates, g_f, y, final_norm):
    n, d = x1.shape
    tm = COMBINE_ROWS
    return pl.pallas_call(
        functools.partial(_combine_body, final_norm=final_norm),
        out_shape=jax.ShapeDtypeStruct((n, d), F32),
        grid=(n // tm,),
        in_specs=[pl.BlockSpec(memory_space=pl.ANY),
                  pl.BlockSpec((tm, d), lambda i: (i, 0)),
                  pl.BlockSpec((tm, TOP_K), lambda i: (i, 0)),
                  _const_spec((1, d)),
                  pl.BlockSpec(memory_space=pl.ANY)],
        out_specs=pl.BlockSpec((tm, d), lambda i: (i, 0)),
        scratch_shapes=[pltpu.SMEM((tm * TOP_K,), I32),
                        pltpu.VMEM((TOP_K, tm, d), F32),
                        pltpu.SemaphoreType.DMA,
                        pltpu.SemaphoreType.DMA],
        compiler_params=_params("arbitrary"),
        name="combine",
    )(dest_flat, x1, gates, g_f, y)


def _layer(x2, bsz, seq, p):
    n, d = x2.shape
    qkv_a, qkv_b = _proj(x2, p["g1"], p["w_in"], seq)
    o_a = _dilated(qkv_a, bsz, seq)
    o_b = _stick(qkv_b, bsz, seq)
    x1, h2, ids, ranks, gates, counts = _merge(
        x2, o_a, o_b, p["g1"], p["w_gate"], p["b_gate"], p["w_pa"], p["w_pb"], p["w_out"],
        p["g2"], p["w_r"], p["b_r"])

    br = EXPERT_ROWS
    counts = counts[0]
    padded = ((counts + br - 1) // br) * br
    ends = jnp.cumsum(padded)
    starts = ends - padded
    n_blocks = (n * TOP_K) // br + N_EXPERTS
    block_row = jnp.arange(n_blocks, dtype=I32) * br
    block_e = jnp.minimum(jnp.sum(ends[None, :] <= block_row[:, None], axis=1),
                          N_EXPERTS - 1).astype(I32)
    n_used = (ends[-1:] // br).astype(I32)

    dest = _dest(ids, ranks, starts.reshape(1, N_EXPERTS).astype(I32)).reshape(n * TOP_K)
    xs = _dispatch((starts + (counts // br) * br).astype(I32), n_used, dest, h2, n_blocks * br)
    y = _experts(block_e, n_used, xs, p["w1"], p["b1"], p["w2"], p["b2"], n_blocks)
    return dest, x1, gates, y


def kernel(x, norm1_g, w_in, w_proj_a, w_proj_b, w_gate, b_gate, w_out, norm2_g,
           w_router, b_router, w1, b1, w2, b2, norm_f_g):
    bsz, seq, d = x.shape
    depth = w_in.shape[0]
    x2 = x.reshape(bsz * seq, d)
    for l in range(depth):
        p = dict(
            g1=norm1_g[l].reshape(1, d), w_in=w_in[l].astype(BF16),
            w_gate=w_gate[l].astype(BF16), b_gate=b_gate[l].reshape(1, 2 * d),
            w_pa=w_proj_a[l].astype(BF16), w_pb=w_proj_b[l].astype(BF16),
            w_out=w_out[l].astype(BF16), g2=norm2_g[l].reshape(1, d),
            w_r=w_router[l], b_r=b_router[l].reshape(1, N_EXPERTS),
            w1=w1[l].astype(BF16), b1=b1[l].reshape(N_EXPERTS, 1, -1),
            w2=w2[l].astype(BF16), b2=b2[l].reshape(N_EXPERTS, 1, -1))
        dest, x1, gates, y = _layer(x2, bsz, seq, p)
        x2 = _combine(dest, x1, gates, norm_f_g.reshape(1, d), y, final_norm=(l == depth - 1))
    return x2.reshape(bsz, seq, d)
```

```python
import functools

import jax
import jax.numpy as jnp
from jax import lax
from jax.experimental import pallas as pl
from jax.experimental.pallas import tpu as pltpu

F32 = jnp.float32
BF16 = jnp.bfloat16
I32 = jnp.int32

HEAD_DIM = 64
WIDTH = 512
QKV = 3 * WIDTH
DILATED_GROUPS = ((128, 1), (512, 4), (2048, 16))
BAND = 128
ROPE_THETA = 500000.0
ROT_DIM = HEAD_DIM // 4
N_EXPERTS = 32
TOP_K = 4
SWIGLU_ALPHA = 1.702
SWIGLU_LIMIT = 7.0
NORM_EPS = 1e-5

LANES = 128
VMEM_LIMIT = 56 * 1024 * 1024

LOG2E = 1.4426950408889634

PROJ_ROWS = 512
MERGE_ROWS = 512
STICK_ROWS = 512
STICK_KEYS = 256
EXPERT_ROWS = 512
DISPATCH_ROWS = 512
COMBINE_ROWS = 256
DEST_ROWS = 2048
DIL_UNROLL = 8
MIX_ROWS = 512


def _rms(x, g):
    return x * lax.rsqrt(jnp.mean(x * x, axis=-1, keepdims=True) + NORM_EPS) * g


def _params(*sem):
    return pltpu.CompilerParams(dimension_semantics=sem, vmem_limit_bytes=VMEM_LIMIT)


def _const_spec(shape):
    return pl.BlockSpec(shape, lambda *_: (0,) * len(shape), pipeline_mode=pl.Buffered(1))


def _proj_body(x_ref, g_ref, w_ref, cos_ref, sa_ref, sb_ref, a_ref, b_ref):
    h = _rms(x_ref[...], g_ref[...]).astype(BF16)
    rep = WIDTH // LANES
    cos = jnp.concatenate([cos_ref[...]] * rep, axis=1)
    sa = jnp.concatenate([sa_ref[...]] * rep, axis=1)
    sb = jnp.concatenate([sb_ref[...]] * rep, axis=1)
    half = ROT_DIM // 2
    for c in range(6):
        acc = jnp.dot(h, w_ref[:, c * WIDTH:(c + 1) * WIDTH], preferred_element_type=F32)
        if c in (0, 1):
            acc = (acc * cos + pltpu.roll(acc, WIDTH - half, 1) * sa
                   + pltpu.roll(acc, half, 1) * sb)
        if c == 0:
            acc = acc * (HEAD_DIM ** -0.5)
        if c == 3:
            acc = acc * (HEAD_DIM ** -0.5 * LOG2E)
        dst = a_ref if c < 3 else b_ref
        dst[:, (c % 3) * WIDTH:(c % 3 + 1) * WIDTH] = acc.astype(dst.dtype)


def _rotary_tables(seq):
    half = ROT_DIM // 2
    inv_freq = jnp.float32(ROPE_THETA) ** (-jnp.arange(0, ROT_DIM, 2, dtype=F32) / ROT_DIM)
    ang = jnp.arange(seq, dtype=jnp.int32).astype(F32)[:, None] * inv_freq[None, :]
    cos, sin = jnp.cos(ang), jnp.sin(ang)
    ones = jnp.ones((seq, HEAD_DIM - ROT_DIM), F32)
    zeros_h = jnp.zeros((seq, half), F32)
    zeros_r = jnp.zeros((seq, HEAD_DIM - ROT_DIM), F32)
    cos_t = jnp.concatenate([cos, cos, ones], axis=1)
    sa_t = jnp.concatenate([-sin, zeros_h, zeros_r], axis=1)
    sb_t = jnp.concatenate([zeros_h, sin, zeros_r], axis=1)
    rep = LANES // HEAD_DIM
    return tuple(jnp.tile(t, (1, rep)) for t in (cos_t, sa_t, sb_t))


def _proj(x2, g1, w_in, seq):
    n, d = x2.shape
    tm = PROJ_ROWS
    tabs = _rotary_tables(seq)
    per_seq = seq // tm
    tab_spec = pl.BlockSpec((tm, LANES), lambda i: (i % per_seq, 0))
    return pl.pallas_call(
        _proj_body,
        out_shape=(jax.ShapeDtypeStruct((n, QKV), F32), jax.ShapeDtypeStruct((n, QKV), BF16)),
        grid=(n // tm,),
        in_specs=[pl.BlockSpec((tm, d), lambda i: (i, 0)),
                  _const_spec((1, d)),
                  _const_spec((d, 2 * QKV)),
                  tab_spec, tab_spec, tab_spec],
        out_specs=(pl.BlockSpec((tm, QKV), lambda i: (i, 0)),
                   pl.BlockSpec((tm, QKV), lambda i: (i, 0))),
        compiler_params=_params("parallel"),
        name="proj",
    )(x2, g1, w_in, *tabs)


def _dil_body(q_ref, k_ref, v_ref, o_ref, os_ref, ls_ref, *, seq):
    heads = LANES // HEAD_DIM
    lane = lax.broadcasted_iota(I32, (1, LANES), 1)
    masks = [(lane >= h * HEAD_DIM) & (lane < (h + 1) * HEAD_DIM) for h in range(heads)]
    qi = lax.broadcasted_iota(I32, (BAND, 2 * BAND), 0)
    ki = lax.broadcasted_iota(I32, (BAND, 2 * BAND), 1)
    neg = jnp.float32(-1e30)
    band_bias = jnp.where((ki >= qi) & (ki <= qi + BAND), 0.0, neg)
    first_bias = jnp.where(ki >= BAND, 0.0, neg)
    nt = (((1,), (1,)), ((), ()))
    n_blocks = seq // BAND

    def rows(ref, start, dil):
        if dil == 1:
            return ref[0, pl.ds(start, BAND), :]
        return ref[0, pl.ds(start, BAND, stride=dil), :]

    for g, (_, dil) in enumerate(DILATED_GROUPS):
        nb = n_blocks // dil

        def block(n, u, g=g, dil=dil, nb=nb):
            r = n // nb
            i = n % nb
            cur = r + i * (BAND * dil)
            prev = r + jnp.maximum(i - 1, 0) * (BAND * dil)
            q = rows(q_ref, cur, dil).astype(BF16)
            kc = jnp.concatenate([rows(k_ref, prev, dil), rows(k_ref, cur, dil)], axis=0).astype(BF16)
            vc = jnp.concatenate([rows(v_ref, prev, dil), rows(v_ref, cur, dil)], axis=0).astype(BF16)
            if u % nb == 0:
                bias = band_bias + jnp.where(i > 0, 0.0, 1.0) * first_bias
            else:
                bias = band_bias
            o_t = jnp.zeros((BAND, LANES), F32)
            l_t = jnp.zeros((BAND, LANES), F32)
            for h in range(heads):
                qh = jnp.where(masks[h], q, jnp.zeros_like(q))
                s = lax.dot_general(qh, kc, nt, preferred_element_type=F32) + bias
                m = jnp.max(s, axis=-1, keepdims=True)
                p = jnp.exp(s - m)
                l = jnp.sum(p, axis=-1, keepdims=True)
                pv = jnp.dot(p.astype(BF16), vc, preferred_element_type=F32)
                o_t = jnp.where(masks[h], pv / l, o_t)
                l_t = jnp.where(masks[h], m + jnp.log(l), l_t)
            if dil == 1:
                os_ref[g, pl.ds(cur, BAND), :] = o_t
                ls_ref[g, pl.ds(cur, BAND), :] = l_t
            else:
                os_ref[g, pl.ds(cur, BAND, stride=dil), :] = o_t
                ls_ref[g, pl.ds(cur, BAND, stride=dil), :] = l_t

        def step(t, carry, block=block):
            for u in range(DIL_UNROLL):
                block(t * DIL_UNROLL + u, u)
            return carry

        lax.fori_loop(0, n_blocks // DIL_UNROLL, step, 0)

    def mix(c, carry):
        r0 = pl.multiple_of(c * MIX_ROWS, MIX_ROWS)
        ls = [ls_ref[g, pl.ds(r0, MIX_ROWS), :] for g in range(len(DILATED_GROUPS))]
        m = functools.reduce(jnp.maximum, ls)
        ws = [jnp.exp(l - m) for l in ls]
        num = sum(os_ref[g, pl.ds(r0, MIX_ROWS), :] * w for g, w in enumerate(ws))
        o_ref[0, pl.ds(r0, MIX_ROWS), :] = (num / sum(ws)).astype(BF16)
        return carry

    lax.fori_loop(0, seq // MIX_ROWS, mix, 0)


def _dilated(qkv_a, bsz, seq):
    assert all(w == BAND * d for w, d in DILATED_GROUPS)
    n_blocks = seq // BAND
    assert seq % BAND == 0 and n_blocks % DIL_UNROLL == 0 and seq % MIX_ROWS == 0
    for _, d in DILATED_GROUPS:
        nb = n_blocks // d
        assert n_blocks % d == 0 and (nb % DIL_UNROLL == 0 or DIL_UNROLL % nb == 0)
    a3 = qkv_a.reshape(bsz, seq, QKV)
    hb = WIDTH // LANES
    n_groups = len(DILATED_GROUPS)
    spec = lambda off: pl.BlockSpec((1, seq, LANES), lambda b, h: (b, 0, off * hb + h))
    return pl.pallas_call(
        functools.partial(_dil_body, seq=seq),
        out_shape=jax.ShapeDtypeStruct((bsz, seq, WIDTH), BF16),
        grid=(bsz, hb),
        in_specs=[spec(0), spec(1), spec(2)],
        out_specs=pl.BlockSpec((1, seq, LANES), lambda b, h: (b, 0, h)),
        scratch_shapes=[pltpu.VMEM((n_groups, seq, LANES), F32),
                        pltpu.VMEM((n_groups, seq, LANES), F32)],
        compiler_params=_params("parallel", "parallel"),
        name="dilated",
    )(a3, a3, a3).reshape(bsz * seq, WIDTH)


def _stick_body(q_ref, k_ref, v_ref, o_ref, acc_ref, carry_ref, z_ref, a_ref, *, tq, tk):
    i = pl.program_id(2)
    heads = LANES // HEAD_DIM
    lane = lax.broadcasted_iota(I32, (1, LANES), 1)
    masks = [(lane >= h * HEAD_DIM) & (lane < (h + 1) * HEAD_DIM) for h in range(heads)]
    lr = lax.broadcasted_iota(I32, (tk, tk), 0)
    lc = lax.broadcasted_iota(I32, (tk, tk), 1)
    later = (lr > lc).astype(BF16)
    nt = (((1,), (1,)), ((), ()))
    q = q_ref[0]
    qh = [jnp.where(mk, q, jnp.zeros_like(q)) for mk in masks]
    acc_ref[...] = jnp.zeros_like(acc_ref)
    carry_ref[...] = jnp.zeros_like(carry_ref)
    nd = tq // tk
    n_before = i * nd

    def first_row(j):
        return pl.multiple_of(jnp.maximum(j, 0) * tk, tk)

    def logits(j, top=0):
        kj = k_ref[0, pl.ds(first_row(j), tk), :]
        return [lax.dot_general(qh[h][top:], kj, nt, preferred_element_type=F32)
                for h in range(heads)]

    def weights(zs, j, diag, top=0):
        rows = tq - top
        if diag:
            ti = lax.broadcasted_iota(I32, (rows, tk), 0)
            si = lax.broadcasted_iota(I32, (rows, tk), 1)
            causal = (first_row(j) + si) < (i * tq + top + ti)
        out = []
        for h, z in enumerate(zs):
            neg_abs = pltpu.bitcast(pltpu.bitcast(z, jnp.uint32) | jnp.uint32(0x80000000), F32)
            sp = jnp.maximum(z, 0.0) + jnp.log2(1.0 + jnp.exp2(neg_abs))
            if diag:
                sp = jnp.where(causal, sp, 0.0)
            c = carry_ref[h, top:, :]
            after = (jnp.dot(sp.astype(BF16), later, preferred_element_type=F32)
                     + jnp.concatenate([c] * (tk // LANES), axis=1))
            a = jnp.exp2(z - sp - after)
            if diag:
                a = jnp.where(causal, a, 0.0)
            carry_ref[h, top:, :] = c + jnp.broadcast_to(jnp.sum(sp, axis=-1, keepdims=True),
                                                         (rows, LANES))
            out.append(a.astype(BF16))
        return out

    def values(a_s, j, top=0):
        vj = v_ref[0, pl.ds(first_row(j), tk), :]
        acc_ref[top:, :] += sum(
            jnp.dot(a_s[h], jnp.where(masks[h], vj, jnp.zeros_like(vj)), preferred_element_type=F32)
            for h in range(heads))

    for dj in reversed(range(nd)):
        j = n_before + dj
        values(weights(logits(j, dj * tk), j, True, dj * tk), j, dj * tk)

    def stage(slot, arrays):
        ref = z_ref if arrays[0].dtype == F32 else a_ref
        for h, arr in enumerate(arrays):
            ref[slot, h] = arr

    def staged(ref, slot):
        return [ref[slot, h] for h in range(heads)]

    a_ref[1] = jnp.zeros_like(a_ref[1])

    @pl.when(i > 0)
    def _():
        stage(0, logits(n_before - 1))

    def step(u, carry):
        for p in range(2):
            j = n_before - 1 - (2 * u + p)
            stage(1 - p, logits(j - 1))
            stage(p, weights(staged(z_ref, p), j, False))
            values(staged(a_ref, 1 - p), j + 1)
        return carry

    lax.fori_loop(0, n_before // 2, step, 0)

    @pl.when(i > 0)
    def _():
        values(staged(a_ref, 1), 0)

    o_ref[0] = acc_ref[...].astype(BF16)


def _stick(qkv_b, bsz, seq):
    tq, tk = STICK_ROWS, STICK_KEYS
    assert tq == 2 * tk and seq % tq == 0
    b3 = qkv_b.reshape(bsz, seq, QKV)
    hb = WIDTH // LANES
    heads = LANES // HEAD_DIM
    return pl.pallas_call(
        functools.partial(_stick_body, tq=tq, tk=tk),
        out_shape=jax.ShapeDtypeStruct((bsz, seq, WIDTH), BF16),
        grid=(bsz, hb, seq // tq),
        in_specs=[pl.BlockSpec((1, tq, LANES), lambda b, h, i: (b, i, h)),
                  pl.BlockSpec((1, seq, LANES), lambda b, h, i: (b, 0, hb + h)),
                  pl.BlockSpec((1, seq, LANES), lambda b, h, i: (b, 0, 2 * hb + h))],
        out_specs=pl.BlockSpec((1, tq, LANES), lambda b, h, i: (b, i, h)),
        scratch_shapes=[pltpu.VMEM((tq, LANES), F32),
                        pltpu.VMEM((heads, tq, LANES), F32),
                        pltpu.VMEM((2, heads, tq, tk), F32),
                        pltpu.VMEM((2, heads, tq, tk), BF16)],
        compiler_params=_params("parallel", "parallel", "parallel"),
        name="stick",
    )(b3, b3, b3).reshape(bsz * seq, WIDTH)


def _merge_body(x_ref, oa_ref, ob_ref,
                g1_ref, wg_ref, bg_ref, pa_ref, pb_ref, wo_ref, g2_ref, wrh_ref, wrl_ref, br_ref,
                x1_ref, h2_ref, ids_ref, rank_ref, gate_ref, cnt_ref, carry_ref):
    tm, d = x_ref.shape

    @pl.when(pl.program_id(0) == 0)
    def _():
        carry_ref[...] = jnp.zeros_like(carry_ref)

    def dot(a, w_ref, lo, hi):
        return jnp.dot(a, w_ref[:, lo:hi], preferred_element_type=F32)

    x = x_ref[...]
    h = _rms(x, g1_ref[...]).astype(BF16)
    gate_a = jax.nn.sigmoid(dot(h, wg_ref, 0, d) + bg_ref[:, :d])
    gate_b = jax.nn.sigmoid(dot(h, wg_ref, d, 2 * d) + bg_ref[:, d:])
    merged = (gate_a * dot(oa_ref[...], pa_ref, 0, d)
              + gate_b * dot(ob_ref[...], pb_ref, 0, d)).astype(BF16)
    x1 = x + jnp.concatenate([dot(merged, wo_ref, 0, d // 2), dot(merged, wo_ref, d // 2, d)], axis=1)
    x1_ref[...] = x1
    h2 = _rms(x1, g2_ref[...])
    h2_ref[...] = h2

    h2_hi = h2.astype(BF16)
    h2_lo = (h2 - h2_hi.astype(F32)).astype(BF16)
    logits = (dot(h2_hi, wrh_ref, 0, N_EXPERTS) + dot(h2_lo, wrh_ref, 0, N_EXPERTS)
              + dot(h2_hi, wrl_ref, 0, N_EXPERTS) + br_ref[...])
    eidx = lax.broadcasted_iota(I32, (tm, N_EXPERTS), 1)
    work = logits
    tops, ids, hots = [], [], []
    for _ in range(TOP_K):
        mk = jnp.max(work, axis=-1, keepdims=True)
        ik = jnp.min(jnp.where(work == mk, eidx, N_EXPERTS), axis=-1, keepdims=True)
        hot = eidx == ik
        work = jnp.where(hot, -jnp.inf, work)
        tops.append(mk)
        ids.append(ik)
        hots.append(hot)
    exps = [jnp.exp(t - tops[0]) for t in tops]
    den = exps[0] + exps[1] + exps[2] + exps[3]
    gate_ref[...] = jnp.concatenate([e / den for e in exps], axis=1)
    ids_ref[...] = jnp.concatenate(ids, axis=1)

    cnt = sum(hot.astype(F32) for hot in hots)
    ri = lax.broadcasted_iota(I32, (tm, tm), 0)
    ci = lax.broadcasted_iota(I32, (tm, tm), 1)
    before = (ci < ri).astype(BF16)
    prefix = jnp.dot(before, cnt.astype(BF16), preferred_element_type=F32) + carry_ref[...]
    ranks = [jnp.sum(jnp.where(hot, prefix, 0.0), axis=-1, keepdims=True) for hot in hots]
    rank_ref[...] = jnp.concatenate(ranks, axis=1).astype(I32)
    total = carry_ref[...] + jnp.sum(cnt, axis=0, keepdims=True)
    carry_ref[...] = total
    cnt_ref[...] = total.astype(I32)


def _merge(x2, o_a, o_b, g1, w_gate, b_gate, w_pa, w_pb, w_out, g2, w_r, b_r):
    n, d = x2.shape
    w_r_hi = w_r.astype(BF16)
    w_r_lo = (w_r - w_r_hi.astype(F32)).astype(BF16)
    tm = MERGE_ROWS
    row = lambda w: pl.BlockSpec((tm, w), lambda i: (i, 0))
    return pl.pallas_call(
        _merge_body,
        out_shape=(jax.ShapeDtypeStruct((n, d), F32),
                   jax.ShapeDtypeStruct((n, d), F32),
                   jax.ShapeDtypeStruct((n, TOP_K), I32),
                   jax.ShapeDtypeStruct((n, TOP_K), I32),
                   jax.ShapeDtypeStruct((n, TOP_K), F32),
                   jax.ShapeDtypeStruct((1, N_EXPERTS), I32)),
        grid=(n // tm,),
        in_specs=[row(d), row(WIDTH), row(WIDTH),
                  _const_spec((1, d)), _const_spec((d, 2 * d)), _const_spec((1, 2 * d)),
                  _const_spec((WIDTH, d)), _const_spec((WIDTH, d)), _const_spec((d, d)),
                  _const_spec((1, d)), _const_spec((d, N_EXPERTS)), _const_spec((d, N_EXPERTS)),
                  _const_spec((1, N_EXPERTS))],
        out_specs=(row(d), row(d), row(TOP_K), row(TOP_K), row(TOP_K),
                   pl.BlockSpec((1, N_EXPERTS), lambda i: (0, 0))),
        scratch_shapes=[pltpu.VMEM((1, N_EXPERTS), F32)],
        compiler_params=_params("arbitrary"),
        name="merge",
    )(x2, o_a, o_b, g1, w_gate, b_gate, w_pa, w_pb, w_out, g2, w_r_hi, w_r_lo, b_r)


def _dest_body(ids_ref, rank_ref, start_ref, dest_ref):
    tm = ids_ref.shape[0]
    eidx = lax.broadcasted_iota(I32, (tm, N_EXPERTS), 1)
    ids = ids_ref[...]
    cols = []
    for k in range(TOP_K):
        hot = eidx == ids[:, k:k + 1]
        cols.append(jnp.sum(jnp.where(hot, start_ref[...], 0), axis=-1, keepdims=True))
    dest_ref[...] = jnp.concatenate(cols, axis=1) + rank_ref[...]


def _dest(ids, ranks, starts):
    n = ids.shape[0]
    tm = DEST_ROWS
    row = pl.BlockSpec((tm, TOP_K), lambda i: (i, 0))
    return pl.pallas_call(
        _dest_body,
        out_shape=jax.ShapeDtypeStruct((n, TOP_K), I32),
        grid=(n // tm,),
        in_specs=[row, row, _const_spec((1, N_EXPERTS))],
        out_specs=row,
        compiler_params=_params("parallel"),
        name="dest",
    )(ids, ranks, starts)


def _dispatch_body(pad_ref, nb_ref, dest_hbm, h_ref, xs_hbm, idx_ref, zero_ref, sem_idx, sem):
    tm = h_ref.shape[0]
    n_blocks = xs_hbm.shape[0] // EXPERT_ROWS
    i = pl.program_id(0)
    idx_cp = pltpu.make_async_copy(dest_hbm.at[pl.ds(i * tm * TOP_K, tm * TOP_K)], idx_ref, sem_idx)
    idx_cp.start()

    @pl.when(i == 0)
    def _():
        zero_ref[...] = jnp.zeros_like(zero_ref)
        for e in range(N_EXPERTS):
            r0 = pl.multiple_of(pad_ref[e], EXPERT_ROWS)
            pltpu.make_async_copy(zero_ref, xs_hbm.at[pl.ds(r0, EXPERT_ROWS)], sem).start()
        for e in range(N_EXPERTS):
            pltpu.make_async_copy(zero_ref, xs_hbm.at[pl.ds(0, EXPERT_ROWS)], sem).wait()

        def fill(b, carry):
            r0 = pl.multiple_of(b * EXPERT_ROWS, EXPERT_ROWS)
            cp = pltpu.make_async_copy(zero_ref, xs_hbm.at[pl.ds(r0, EXPERT_ROWS)], sem)
            cp.start()
            cp.wait()
            return carry

        lax.fori_loop(nb_ref[0], n_blocks, fill, 0)

    idx_cp.wait()

    def body(t, carry):
        for k in range(TOP_K):
            d = idx_ref[t * TOP_K + k]
            pltpu.make_async_copy(h_ref.at[pl.ds(t, 1)], xs_hbm.at[pl.ds(d, 1)], sem).start()
        return carry

    lax.fori_loop(0, tm, body, 0)
    for k in range(TOP_K):
        pltpu.make_async_copy(h_ref, xs_hbm.at[pl.ds(0, tm)], sem).wait()


def _dispatch(pad_start, n_used, dest_flat, h2, n_rows):
    n, d = h2.shape
    tm = DISPATCH_ROWS
    return pl.pallas_call(
        _dispatch_body,
        out_shape=jax.ShapeDtypeStruct((n_rows, d), h2.dtype),
        grid_spec=pltpu.PrefetchScalarGridSpec(
            num_scalar_prefetch=2,
            grid=(n // tm,),
            in_specs=[pl.BlockSpec(memory_space=pl.ANY),
                      pl.BlockSpec((tm, d), lambda i, pad, nb: (i, 0))],
            out_specs=pl.BlockSpec(memory_space=pl.ANY),
            scratch_shapes=[pltpu.SMEM((tm * TOP_K,), I32),
                            pltpu.VMEM((EXPERT_ROWS, d), h2.dtype),
                            pltpu.SemaphoreType.DMA,
                            pltpu.SemaphoreType.DMA]),
        compiler_params=_params("arbitrary"),
        name="dispatch",
    )(pad_start, n_used, dest_flat, h2)


def _expert_body(be_ref, nb_ref, x_ref, w1_ref, b1_ref, w2_ref, b2_ref, y_ref):
    f = w2_ref.shape[1]

    @pl.when(pl.program_id(0) < nb_ref[0])
    def _():
        xb = x_ref[...].astype(BF16)
        hid = jnp.dot(xb, w1_ref[0], preferred_element_type=F32) + b1_ref[0]
        glu = jnp.minimum(hid[:, :f], SWIGLU_LIMIT)
        lin = jnp.clip(hid[:, f:], -SWIGLU_LIMIT, SWIGLU_LIMIT)
        act = glu * jax.nn.sigmoid(SWIGLU_ALPHA * glu) * (lin + 1.0)
        y_ref[...] = jnp.dot(act.astype(BF16), w2_ref[0], preferred_element_type=F32) + b2_ref[0]

    @pl.when(pl.program_id(0) >= nb_ref[0])
    def _():
        y_ref[...] = jnp.zeros_like(y_ref)


def _experts(block_e, n_used, xs, w1, b1, w2, b2, n_blocks):
    d = xs.shape[1]
    f = w2.shape[1]
    br = EXPERT_ROWS
    blk = lambda b, be, nb: (jnp.minimum(b, nb[0] - 1), 0)
    per_e = lambda b, be, nb: (be[b], 0, 0)
    return pl.pallas_call(
        _expert_body,
        out_shape=jax.ShapeDtypeStruct((n_blocks * br, d), F32),
        grid_spec=pltpu.PrefetchScalarGridSpec(
            num_scalar_prefetch=2,
            grid=(n_blocks,),
            in_specs=[pl.BlockSpec((br, d), blk),
                      pl.BlockSpec((1, d, 2 * f), per_e),
                      pl.BlockSpec((1, 1, 2 * f), per_e),
                      pl.BlockSpec((1, f, d), per_e),
                      pl.BlockSpec((1, 1, d), per_e)],
            out_specs=pl.BlockSpec((br, d), lambda b, be, nb: (b, 0))),
        compiler_params=_params("arbitrary"),
        name="experts",
    )(block_e, n_used, xs, w1, b1, w2, b2)


def _combine_body(dest_hbm, x1_ref, gate_ref, g_ref, y_hbm, o_ref, idx_ref, buf_ref, sem_idx, sem,
                  *, final_norm):
    tm = x1_ref.shape[0]
    i = pl.program_id(0)
    idx_cp = pltpu.make_async_copy(dest_hbm.at[pl.ds(i * tm * TOP_K, tm * TOP_K)], idx_ref, sem_idx)
    idx_cp.start()
    idx_cp.wait()

    def body(t, carry):
        for k in range(TOP_K):
            d = idx_ref[t * TOP_K + k]
            pltpu.make_async_copy(y_hbm.at[pl.ds(d, 1)], buf_ref.at[k, pl.ds(t, 1)], sem).start()
        return carry

    lax.fori_loop(0, tm, body, 0)
    for k in range(TOP_K):
        pltpu.make_async_copy(y_hbm.at[pl.ds(0, tm)], buf_ref.at[k], sem).wait()

    gate = gate_ref[...]
    acc = x1_ref[...]
    for k in range(TOP_K):
        acc = acc + gate[:, k:k + 1] * buf_ref[k]
    o_ref[...] = _rms(acc, g_ref[...]) if final_norm else acc


def _combine(dest_flat, x1, gates, g_f, y, final_norm):
    n, d = x1.shape
    tm = COMBINE_ROWS
    return pl.pallas_call(
        functools.partial(_combine_body, final_norm=final_norm),
        out_shape=jax.ShapeDtypeStruct((n, d), F32),
        grid=(n // tm,),
        in_specs=[pl.BlockSpec(memory_space=pl.ANY),
                  pl.BlockSpec((tm, d), lambda i: (i, 0)),
                  pl.BlockSpec((tm, TOP_K), lambda i: (i, 0)),
                  _const_spec((1, d)),
                  pl.BlockSpec(memory_space=pl.ANY)],
        out_specs=pl.BlockSpec((tm, d), lambda i: (i, 0)),
        scratch_shapes=[pltpu.SMEM((tm * TOP_K,), I32),
                        pltpu.VMEM((TOP_K, tm, d), F32),
                        pltpu.SemaphoreType.DMA,
                        pltpu.SemaphoreType.DMA],
        compiler_params=_params("arbitrary"),
        name="combine",
    )(dest_flat, x1, gates, g_f, y)


def _layer(x2, bsz, seq, p):
    n, d = x2.shape
    qkv_a, qkv_b = _proj(x2, p["g1"], p["w_in"], seq)
    o_a = _dilated(qkv_a, bsz, seq)
    o_b = _stick(qkv_b, bsz, seq)
    x1, h2, ids, ranks, gates, counts = _merge(
        x2, o_a, o_b, p["g1"], p["w_gate"], p["b_gate"], p["w_pa"], p["w_pb"], p["w_out"],
        p["g2"], p["w_r"], p["b_r"])

    br = EXPERT_ROWS
    counts = counts[0]
    padded = ((counts + br - 1) // br) * br
    ends = jnp.cumsum(padded)
    starts = ends - padded
    n_blocks = (n * TOP_K) // br + N_EXPERTS
    block_row = jnp.arange(n_blocks, dtype=I32) * br
    block_e = jnp.minimum(jnp.sum(ends[None, :] <= block_row[:, None], axis=1),
                          N_EXPERTS - 1).astype(I32)
    n_used = (ends[-1:] // br).astype(I32)

    dest = _dest(ids, ranks, starts.reshape(1, N_EXPERTS).astype(I32)).reshape(n * TOP_K)
    xs = _dispatch((starts + (counts // br) * br).astype(I32), n_used, dest, h2, n_blocks * br)
    y = _experts(block_e, n_used, xs, p["w1"], p["b1"], p["w2"], p["b2"], n_blocks)
    return dest, x1, gates, y


def kernel(x, norm1_g, w_in, w_proj_a, w_proj_b, w_gate, b_gate, w_out, norm2_g,
           w_router, b_router, w1, b1, w2, b2, norm_f_g):
    bsz, seq, d = x.shape
    depth = w_in.shape[0]
    x2 = x.reshape(bsz * seq, d)
    for l in range(depth):
        p = dict(
            g1=norm1_g[l].reshape(1, d), w_in=w_in[l].astype(BF16),
            w_gate=w_gate[l].astype(BF16), b_gate=b_gate[l].reshape(1, 2 * d),
            w_pa=w_proj_a[l].astype(BF16), w_pb=w_proj_b[l].astype(BF16),
            w_out=w_out[l].astype(BF16), g2=norm2_g[l].reshape(1, d),
            w_r=w_router[l], b_r=b_router[l].reshape(1, N_EXPERTS),
            w1=w1[l].astype(BF16), b1=b1[l].reshape(N_EXPERTS, 1, -1),
            w2=w2[l].astype(BF16), b2=b2[l].reshape(N_EXPERTS, 1, -1))
        dest, x1, gates, y = _layer(x2, bsz, seq, p)
        x2 = _combine(dest, x1, gates, norm_f_g.reshape(1, d), y, final_norm=(l == depth - 1))
    return x2.reshape(bsz, seq, d)
```

```python
import functools

import jax
import jax.numpy as jnp
from jax import lax
from jax.experimental import pallas as pl
from jax.experimental.pallas import tpu as pltpu

F32 = jnp.float32
BF16 = jnp.bfloat16
I32 = jnp.int32

HEAD_DIM = 64
WIDTH = 512
QKV = 3 * WIDTH
DILATED_GROUPS = ((128, 1), (512, 4), (2048, 16))
BAND = 128
ROPE_THETA = 500000.0
ROT_DIM = HEAD_DIM // 4
N_EXPERTS = 32
TOP_K = 4
SWIGLU_ALPHA = 1.702
SWIGLU_LIMIT = 7.0
NORM_EPS = 1e-5

LANES = 128
VMEM_LIMIT = 56 * 1024 * 1024

LOG2E = 1.4426950408889634

PROJ_ROWS = 512
MERGE_ROWS = 512
STICK_ROWS = 512
STICK_KEYS = 256
EXPERT_ROWS = 512
MOE_TILE = MERGE_ROWS
SEG_ALIGN = 8
STAGE_ROWS = MOE_TILE * TOP_K + N_EXPERTS * SEG_ALIGN
DIL_UNROLL = 8
MIX_ROWS = 512


def _rms(x, g):
    return x * lax.rsqrt(jnp.mean(x * x, axis=-1, keepdims=True) + NORM_EPS) * g


def _params(*sem):
    return pltpu.CompilerParams(dimension_semantics=sem, vmem_limit_bytes=VMEM_LIMIT)


def _const_spec(shape):
    return pl.BlockSpec(shape, lambda *_: (0,) * len(shape), pipeline_mode=pl.Buffered(1))


def _proj_body(x_ref, g_ref, w_ref, cos_ref, sa_ref, sb_ref, a_ref, b_ref):
    h = _rms(x_ref[...], g_ref[...]).astype(BF16)
    rep = WIDTH // LANES
    cos = jnp.concatenate([cos_ref[...]] * rep, axis=1)
    sa = jnp.concatenate([sa_ref[...]] * rep, axis=1)
    sb = jnp.concatenate([sb_ref[...]] * rep, axis=1)
    half = ROT_DIM // 2
    for c in range(6):
        acc = jnp.dot(h, w_ref[:, c * WIDTH:(c + 1) * WIDTH], preferred_element_type=F32)
        if c in (0, 1):
            acc = (acc * cos + pltpu.roll(acc, WIDTH - half, 1) * sa
                   + pltpu.roll(acc, half, 1) * sb)
        if c == 0:
            acc = acc * (HEAD_DIM ** -0.5)
        if c == 3:
            acc = acc * (HEAD_DIM ** -0.5 * LOG2E)
        dst = a_ref if c < 3 else b_ref
        dst[:, (c % 3) * WIDTH:(c % 3 + 1) * WIDTH] = acc.astype(dst.dtype)


def _rotary_tables(seq):
    half = ROT_DIM // 2
    inv_freq = jnp.float32(ROPE_THETA) ** (-jnp.arange(0, ROT_DIM, 2, dtype=F32) / ROT_DIM)
    ang = jnp.arange(seq, dtype=jnp.int32).astype(F32)[:, None] * inv_freq[None, :]
    cos, sin = jnp.cos(ang), jnp.sin(ang)
    ones = jnp.ones((seq, HEAD_DIM - ROT_DIM), F32)
    zeros_h = jnp.zeros((seq, half), F32)
    zeros_r = jnp.zeros((seq, HEAD_DIM - ROT_DIM), F32)
    cos_t = jnp.concatenate([cos, cos, ones], axis=1)
    sa_t = jnp.concatenate([-sin, zeros_h, zeros_r], axis=1)
    sb_t = jnp.concatenate([zeros_h, sin, zeros_r], axis=1)
    rep = LANES // HEAD_DIM
    return tuple(jnp.tile(t, (1, rep)) for t in (cos_t, sa_t, sb_t))


def _proj(x2, g1, w_in, seq):
    n, d = x2.shape
    tm = PROJ_ROWS
    tabs = _rotary_tables(seq)
    per_seq = seq // tm
    tab_spec = pl.BlockSpec((tm, LANES), lambda i: (i % per_seq, 0))
    return pl.pallas_call(
        _proj_body,
        out_shape=(jax.ShapeDtypeStruct((n, QKV), F32), jax.ShapeDtypeStruct((n, QKV), BF16)),
        grid=(n // tm,),
        in_specs=[pl.BlockSpec((tm, d), lambda i: (i, 0)),
                  _const_spec((1, d)),
                  _const_spec((d, 2 * QKV)),
                  tab_spec, tab_spec, tab_spec],
        out_specs=(pl.BlockSpec((tm, QKV), lambda i: (i, 0)),
                   pl.BlockSpec((tm, QKV), lambda i: (i, 0))),
        compiler_params=_params("parallel"),
        name="proj",
    )(x2, g1, w_in, *tabs)


def _dil_body(q_ref, k_ref, v_ref, o_ref, os_ref, ls_ref, *, seq):
    heads = LANES // HEAD_DIM
    lane = lax.broadcasted_iota(I32, (1, LANES), 1)
    masks = [(lane >= h * HEAD_DIM) & (lane < (h + 1) * HEAD_DIM) for h in range(heads)]
    qi = lax.broadcasted_iota(I32, (BAND, 2 * BAND), 0)
    ki = lax.broadcasted_iota(I32, (BAND, 2 * BAND), 1)
    neg = jnp.float32(-1e30)
    band_bias = jnp.where((ki >= qi) & (ki <= qi + BAND), 0.0, neg)
    first_bias = jnp.where(ki >= BAND, 0.0, neg)
    nt = (((1,), (1,)), ((), ()))
    n_blocks = seq // BAND

    def rows(ref, start, dil):
        if dil == 1:
            return ref[0, pl.ds(start, BAND), :]
        return ref[0, pl.ds(start, BAND, stride=dil), :]

    for g, (_, dil) in enumerate(DILATED_GROUPS):
        nb = n_blocks // dil

        def block(n, u, g=g, dil=dil, nb=nb):
            r = n // nb
            i = n % nb
            cur = r + i * (BAND * dil)
            prev = r + jnp.maximum(i - 1, 0) * (BAND * dil)
            q = rows(q_ref, cur, dil).astype(BF16)
            kc = jnp.concatenate([rows(k_ref, prev, dil), rows(k_ref, cur, dil)], axis=0).astype(BF16)
            vc = jnp.concatenate([rows(v_ref, prev, dil), rows(v_ref, cur, dil)], axis=0).astype(BF16)
            if u % nb == 0:
                bias = band_bias + jnp.where(i > 0, 0.0, 1.0) * first_bias
            else:
                bias = band_bias
            o_t = jnp.zeros((BAND, LANES), F32)
            l_t = jnp.zeros((BAND, LANES), F32)
            for h in range(heads):
                qh = jnp.where(masks[h], q, jnp.zeros_like(q))
                s = lax.dot_general(qh, kc, nt, preferred_element_type=F32) + bias
                m = jnp.max(s, axis=-1, keepdims=True)
                p = jnp.exp(s - m)
                l = jnp.sum(p, axis=-1, keepdims=True)
                pv = jnp.dot(p.astype(BF16), vc, preferred_element_type=F32)
                o_t = jnp.where(masks[h], pv / l, o_t)
                l_t = jnp.where(masks[h], m + jnp.log(l), l_t)
            if dil == 1:
                os_ref[g, pl.ds(cur, BAND), :] = o_t
                ls_ref[g, pl.ds(cur, BAND), :] = l_t
            else:
                os_ref[g, pl.ds(cur, BAND, stride=dil), :] = o_t
                ls_ref[g, pl.ds(cur, BAND, stride=dil), :] = l_t

        def step(t, carry, block=block):
            for u in range(DIL_UNROLL):
                block(t * DIL_UNROLL + u, u)
            return carry

        lax.fori_loop(0, n_blocks // DIL_UNROLL, step, 0)

    def mix(c, carry):
        r0 = pl.multiple_of(c * MIX_ROWS, MIX_ROWS)
        ls = [ls_ref[g, pl.ds(r0, MIX_ROWS), :] for g in range(len(DILATED_GROUPS))]
        m = functools.reduce(jnp.maximum, ls)
        ws = [jnp.exp(l - m) for l in ls]
        num = sum(os_ref[g, pl.ds(r0, MIX_ROWS), :] * w for g, w in enumerate(ws))
        o_ref[0, pl.ds(r0, MIX_ROWS), :] = (num / sum(ws)).astype(BF16)
        return carry

    lax.fori_loop(0, seq // MIX_ROWS, mix, 0)


def _dilated(qkv_a, bsz, seq):
    assert all(w == BAND * d for w, d in DILATED_GROUPS)
    n_blocks = seq // BAND
    assert seq % BAND == 0 and n_blocks % DIL_UNROLL == 0 and seq % MIX_ROWS == 0
    for _, d in DILATED_GROUPS:
        nb = n_blocks // d
        assert n_blocks % d == 0 and (nb % DIL_UNROLL == 0 or DIL_UNROLL % nb == 0)
    a3 = qkv_a.reshape(bsz, seq, QKV)
    hb = WIDTH // LANES
    n_groups = len(DILATED_GROUPS)
    spec = lambda off: pl.BlockSpec((1, seq, LANES), lambda b, h: (b, 0, off * hb + h))
    return pl.pallas_call(
        functools.partial(_dil_body, seq=seq),
        out_shape=jax.ShapeDtypeStruct((bsz, seq, WIDTH), BF16),
        grid=(bsz, hb),
        in_specs=[spec(0), spec(1), spec(2)],
        out_specs=pl.BlockSpec((1, seq, LANES), lambda b, h: (b, 0, h)),
        scratch_shapes=[pltpu.VMEM((n_groups, seq, LANES), F32),
                        pltpu.VMEM((n_groups, seq, LANES), F32)],
        compiler_params=_params("parallel", "parallel"),
        name="dilated",
    )(a3, a3, a3).reshape(bsz * seq, WIDTH)


def _stick_body(q_ref, k_ref, v_ref, o_ref, acc_ref, carry_ref, z_ref, a_ref, *, tq, tk):
    i = pl.program_id(2)
    heads = LANES // HEAD_DIM
    lane = lax.broadcasted_iota(I32, (1, LANES), 1)
    masks = [(lane >= h * HEAD_DIM) & (lane < (h + 1) * HEAD_DIM) for h in range(heads)]
    lr = lax.broadcasted_iota(I32, (tk, tk), 0)
    lc = lax.broadcasted_iota(I32, (tk, tk), 1)
    later = (lr > lc).astype(BF16)
    nt = (((1,), (1,)), ((), ()))
    q = q_ref[0]
    qh = [jnp.where(mk, q, jnp.zeros_like(q)) for mk in masks]
    acc_ref[...] = jnp.zeros_like(acc_ref)
    carry_ref[...] = jnp.zeros_like(carry_ref)
    nd = tq // tk
    n_before = i * nd

    def first_row(j):
        return pl.multiple_of(jnp.maximum(j, 0) * tk, tk)

    def logits(j, top=0):
        kj = k_ref[0, pl.ds(first_row(j), tk), :]
        return [lax.dot_general(qh[h][top:], kj, nt, preferred_element_type=F32)
                for h in range(heads)]

    def weights(zs, j, diag, top=0):
        rows = tq - top
        if diag:
            ti = lax.broadcasted_iota(I32, (rows, tk), 0)
            si = lax.broadcasted_iota(I32, (rows, tk), 1)
            causal = (first_row(j) + si) < (i * tq + top + ti)
        out = []
        for h, z in enumerate(zs):
            neg_abs = pltpu.bitcast(pltpu.bitcast(z, jnp.uint32) | jnp.uint32(0x80000000), F32)
            sp = jnp.maximum(z, 0.0) + jnp.log2(1.0 + jnp.exp2(neg_abs))
            if diag:
                sp = jnp.where(causal, sp, 0.0)
            c = carry_ref[h, top:, :]
            after = (jnp.dot(sp.astype(BF16), later, preferred_element_type=F32)
                     + jnp.concatenate([c] * (tk // LANES), axis=1))
            a = jnp.exp2(z - sp - after)
            if diag:
                a = jnp.where(causal, a, 0.0)
            carry_ref[h, top:, :] = c + jnp.broadcast_to(jnp.sum(sp, axis=-1, keepdims=True),
                                                         (rows, LANES))
            out.append(a.astype(BF16))
        return out

    def values(a_s, j, top=0):
        vj = v_ref[0, pl.ds(first_row(j), tk), :]
        acc_ref[top:, :] += sum(
            jnp.dot(a_s[h], jnp.where(masks[h], vj, jnp.zeros_like(vj)), preferred_element_type=F32)
            for h in range(heads))

    for dj in reversed(range(nd)):
        j = n_before + dj
        values(weights(logits(j, dj * tk), j, True, dj * tk), j, dj * tk)

    def stage(slot, arrays):
        ref = z_ref if arrays[0].dtype == F32 else a_ref
        for h, arr in enumerate(arrays):
            ref[slot, h] = arr

    def staged(ref, slot):
        return [ref[slot, h] for h in range(heads)]

    a_ref[1] = jnp.zeros_like(a_ref[1])

    @pl.when(i > 0)
    def _():
        stage(0, logits(n_before - 1))

    def step(u, carry):
        for p in range(2):
            j = n_before - 1 - (2 * u + p)
            stage(1 - p, logits(j - 1))
            stage(p, weights(staged(z_ref, p), j, False))
            values(staged(a_ref, 1 - p), j + 1)
        return carry

    lax.fori_loop(0, n_before // 2, step, 0)

    @pl.when(i > 0)
    def _():
        values(staged(a_ref, 1), 0)

    o_ref[0] = acc_ref[...].astype(BF16)


def _stick(qkv_b, bsz, seq):
    tq, tk = STICK_ROWS, STICK_KEYS
    assert tq == 2 * tk and seq % tq == 0
    b3 = qkv_b.reshape(bsz, seq, QKV)
    hb = WIDTH // LANES
    heads = LANES // HEAD_DIM
    return pl.pallas_call(
        functools.partial(_stick_body, tq=tq, tk=tk),
        out_shape=jax.ShapeDtypeStruct((bsz, seq, WIDTH), BF16),
        grid=(bsz, hb, seq // tq),
        in_specs=[pl.BlockSpec((1, tq, LANES), lambda b, h, i: (b, i, h)),
                  pl.BlockSpec((1, seq, LANES), lambda b, h, i: (b, 0, hb + h)),
                  pl.BlockSpec((1, seq, LANES), lambda b, h, i: (b, 0, 2 * hb + h))],
        out_specs=pl.BlockSpec((1, tq, LANES), lambda b, h, i: (b, i, h)),
        scratch_shapes=[pltpu.VMEM((tq, LANES), F32),
                        pltpu.VMEM((heads, tq, LANES), F32),
                        pltpu.VMEM((2, heads, tq, tk), F32),
                        pltpu.VMEM((2, heads, tq, tk), BF16)],
        compiler_params=_params("parallel", "parallel", "parallel"),
        name="stick",
    )(b3, b3, b3).reshape(bsz * seq, WIDTH)


def _merge_body(x_ref, oa_ref, ob_ref,
                g1_ref, wg_ref, bg_ref, pa_ref, pb_ref, wo_ref, g2_ref, wrh_ref, wrl_ref, br_ref,
                x1_ref, h2_ref, ids_ref, rank_ref, gate_ref, cnt_ref, before_ref, carry_ref):
    tm, d = x_ref.shape

    @pl.when(pl.program_id(0) == 0)
    def _():
        carry_ref[...] = jnp.zeros_like(carry_ref)

    def dot(a, w_ref, lo, hi):
        return jnp.dot(a, w_ref[:, lo:hi], preferred_element_type=F32)

    x = x_ref[...]
    h = _rms(x, g1_ref[...]).astype(BF16)
    gate_a = jax.nn.sigmoid(dot(h, wg_ref, 0, d) + bg_ref[:, :d])
    gate_b = jax.nn.sigmoid(dot(h, wg_ref, d, 2 * d) + bg_ref[:, d:])
    merged = (gate_a * dot(oa_ref[...], pa_ref, 0, d)
              + gate_b * dot(ob_ref[...], pb_ref, 0, d)).astype(BF16)
    x1 = x + jnp.concatenate([dot(merged, wo_ref, 0, d // 2), dot(merged, wo_ref, d // 2, d)], axis=1)
    x1_ref[...] = x1
    h2 = _rms(x1, g2_ref[...])
    h2_ref[...] = h2

    h2_hi = h2.astype(BF16)
    h2_lo = (h2 - h2_hi.astype(F32)).astype(BF16)
    logits = (dot(h2_hi, wrh_ref, 0, N_EXPERTS) + dot(h2_lo, wrh_ref, 0, N_EXPERTS)
              + dot(h2_hi, wrl_ref, 0, N_EXPERTS) + br_ref[...])
    eidx = lax.broadcasted_iota(I32, (tm, N_EXPERTS), 1)
    work = logits
    tops, ids, hots = [], [], []
    for _ in range(TOP_K):
        mk = jnp.max(work, axis=-1, keepdims=True)
        ik = jnp.min(jnp.where(work == mk, eidx, N_EXPERTS), axis=-1, keepdims=True)
        hot = eidx == ik
        work = jnp.where(hot, -jnp.inf, work)
        tops.append(mk)
        ids.append(ik)
        hots.append(hot)
    exps = [jnp.exp(t - tops[0]) for t in tops]
    den = exps[0] + exps[1] + exps[2] + exps[3]
    gate_ref[...] = jnp.concatenate([e / den for e in exps], axis=1)
    ids_ref[...] = jnp.concatenate(ids, axis=1)

    cnt = sum(hot.astype(F32) for hot in hots)
    ri = lax.broadcasted_iota(I32, (tm, tm), 0)
    ci = lax.broadcasted_iota(I32, (tm, tm), 1)
    before = (ci < ri).astype(BF16)
    prefix = jnp.dot(before, cnt.astype(BF16), preferred_element_type=F32) + carry_ref[...]
    ranks = [jnp.sum(jnp.where(hot, prefix, 0.0), axis=-1, keepdims=True) for hot in hots]
    rank_ref[...] = jnp.concatenate(ranks, axis=1).astype(I32)
    before_ref[0] = carry_ref[...].astype(I32)
    total = carry_ref[...] + jnp.sum(cnt, axis=0, keepdims=True)
    carry_ref[...] = total
    cnt_ref[...] = total.astype(I32)


def _merge(x2, o_a, o_b, g1, w_gate, b_gate, w_pa, w_pb, w_out, g2, w_r, b_r):
    n, d = x2.shape
    w_r_hi = w_r.astype(BF16)
    w_r_lo = (w_r - w_r_hi.astype(F32)).astype(BF16)
    tm = MERGE_ROWS
    row = lambda w: pl.BlockSpec((tm, w), lambda i: (i, 0))
    return pl.pallas_call(
        _merge_body,
        out_shape=(jax.ShapeDtypeStruct((n, d), F32),
                   jax.ShapeDtypeStruct((n, d), F32),
                   jax.ShapeDtypeStruct((n, TOP_K), I32),
                   jax.ShapeDtypeStruct((n, TOP_K), I32),
                   jax.ShapeDtypeStruct((n, TOP_K), F32),
                   jax.ShapeDtypeStruct((1, N_EXPERTS), I32),
                   jax.ShapeDtypeStruct((n // tm, 1, N_EXPERTS), I32)),
        grid=(n // tm,),
        in_specs=[row(d), row(WIDTH), row(WIDTH),
                  _const_spec((1, d)), _const_spec((d, 2 * d)), _const_spec((1, 2 * d)),
                  _const_spec((WIDTH, d)), _const_spec((WIDTH, d)), _const_spec((d, d)),
                  _const_spec((1, d)), _const_spec((d, N_EXPERTS)), _const_spec((d, N_EXPERTS)),
                  _const_spec((1, N_EXPERTS))],
        out_specs=(row(d), row(d), row(TOP_K), row(TOP_K), row(TOP_K),
                   pl.BlockSpec((1, N_EXPERTS), lambda i: (0, 0)),
                   pl.BlockSpec((1, 1, N_EXPERTS), lambda i: (i, 0, 0))),
        scratch_shapes=[pltpu.VMEM((1, N_EXPERTS), F32)],
        compiler_params=_params("arbitrary"),
        name="merge",
    )(x2, o_a, o_b, g1, w_gate, b_gate, w_pa, w_pb, w_out, g2, w_r_hi, w_r_lo, b_r)


def _slot_body(ids_ref, rank_ref, adj_ref, slot_ref):
    tm = ids_ref.shape[0]
    eidx = lax.broadcasted_iota(I32, (tm, N_EXPERTS), 1)
    ids = ids_ref[...]
    cols = []
    for k in range(TOP_K):
        hot = eidx == ids[:, k:k + 1]
        cols.append(jnp.sum(jnp.where(hot, adj_ref[0], 0), axis=-1, keepdims=True))
    slot_ref[...] = jnp.concatenate(cols, axis=1) + rank_ref[...]


def _slots(ids, ranks, adj):
    n = ids.shape[0]
    tm = MOE_TILE
    row = pl.BlockSpec((tm, TOP_K), lambda i: (i, 0))
    return pl.pallas_call(
        _slot_body,
        out_shape=jax.ShapeDtypeStruct((n, TOP_K), I32),
        grid=(n // tm,),
        in_specs=[row, row, pl.BlockSpec((1, 1, N_EXPERTS), lambda i: (i, 0, 0))],
        out_specs=row,
        compiler_params=_params("parallel"),
        name="slots",
    )(ids, ranks, adj)


def _segment_copies(len_ref, src_ref, dst_ref, tile, make_copy, wait):
    for e in range(N_EXPERTS):
        base = tile * N_EXPERTS + e
        length, src, dst = len_ref[base], src_ref[base], dst_ref[base]
        size = MOE_TILE
        while size >= SEG_ALIGN:
            done = (length // (2 * size)) * (2 * size)

            @pl.when((length & size) != 0)
            def _(size=size, done=done):
                cp = make_copy(pl.multiple_of(src + done, SEG_ALIGN),
                               pl.multiple_of(dst + done, SEG_ALIGN), size)
                if wait:
                    cp.wait()
                else:
                    cp.start()

            size //= 2


def _dispatch_body(len_ref, src_ref, dst_ref, pad_ref, nb_ref, slot_ref, h_ref, xs_hbm,
                   stage_ref, zero_ref, sem, zero_sem):
    tm = h_ref.shape[0]
    n_blocks = xs_hbm.shape[0] // EXPERT_ROWS
    tile = pl.program_id(0)
    buf = tile % 2

    @pl.when(tile == 0)
    def _():
        zero_ref[...] = jnp.zeros_like(zero_ref)
        for e in range(N_EXPERTS):
            r0 = pl.multiple_of(pad_ref[e], EXPERT_ROWS)
            pltpu.make_async_copy(zero_ref, xs_hbm.at[pl.ds(r0, EXPERT_ROWS)], zero_sem).start()
        for e in range(N_EXPERTS):
            pltpu.make_async_copy(zero_ref, xs_hbm.at[pl.ds(0, EXPERT_ROWS)], zero_sem).wait()

        def fill(b, carry):
            r0 = pl.multiple_of(b * EXPERT_ROWS, EXPERT_ROWS)
            cp = pltpu.make_async_copy(zero_ref, xs_hbm.at[pl.ds(r0, EXPERT_ROWS)], zero_sem)
            cp.start()
            cp.wait()
            return carry

        lax.fori_loop(nb_ref[0], n_blocks, fill, 0)

    slots = slot_ref[...]
    col = lax.broadcasted_iota(I32, (tm, STAGE_ROWS), 1)
    hit = col == slots[:, 0:1]
    for k in range(1, TOP_K):
        hit = hit | (col == slots[:, k:k + 1])
    pick = jnp.where(hit, 1.0, 0.0).astype(BF16)
    stage_ref[buf] = lax.dot_general(pick, h_ref[...].astype(BF16), (((0,), (0,)), ((), ())),
                                     preferred_element_type=F32)

    def copy_from(b):
        return lambda src, dst, size: pltpu.make_async_copy(
            stage_ref.at[b, pl.ds(src, size)], xs_hbm.at[pl.ds(dst, size)], sem)

    @pl.when(tile > 0)
    def _():
        _segment_copies(len_ref, src_ref, dst_ref, tile - 1, copy_from(1 - buf), wait=True)

    _segment_copies(len_ref, src_ref, dst_ref, tile, copy_from(buf), wait=False)

    @pl.when(tile == pl.num_programs(0) - 1)
    def _():
        _segment_copies(len_ref, src_ref, dst_ref, tile, copy_from(buf), wait=True)


def _dispatch(seg_len, seg_src, seg_dst, pad_start, n_used, slots, h2, n_rows):
    n, d = h2.shape
    tm = MOE_TILE
    tiled = lambda w: pl.BlockSpec((tm, w), lambda i, *_: (i, 0))
    return pl.pallas_call(
        _dispatch_body,
        out_shape=jax.ShapeDtypeStruct((n_rows, d), F32),
        grid_spec=pltpu.PrefetchScalarGridSpec(
            num_scalar_prefetch=5,
            grid=(n // tm,),
            in_specs=[tiled(TOP_K), tiled(d)],
            out_specs=pl.BlockSpec(memory_space=pl.ANY),
            scratch_shapes=[pltpu.VMEM((2, STAGE_ROWS, d), F32),
                            pltpu.VMEM((EXPERT_ROWS, d), F32),
                            pltpu.SemaphoreType.DMA,
                            pltpu.SemaphoreType.DMA]),
        compiler_params=_params("arbitrary"),
        name="dispatch",
    )(seg_len, seg_src, seg_dst, pad_start, n_used, slots, h2)


def _expert_body(be_ref, nb_ref, x_ref, w1_ref, b1_ref, w2_ref, b2_ref, y_ref):
    f = w2_ref.shape[1]

    @pl.when(pl.program_id(0) < nb_ref[0])
    def _():
        xb = x_ref[...].astype(BF16)
        hid = jnp.dot(xb, w1_ref[0], preferred_element_type=F32) + b1_ref[0]
        glu = jnp.minimum(hid[:, :f], SWIGLU_LIMIT)
        lin = jnp.clip(hid[:, f:], -SWIGLU_LIMIT, SWIGLU_LIMIT)
        act = glu * jax.nn.sigmoid(SWIGLU_ALPHA * glu) * (lin + 1.0)
        y_ref[...] = jnp.dot(act.astype(BF16), w2_ref[0], preferred_element_type=F32) + b2_ref[0]

    @pl.when(pl.program_id(0) >= nb_ref[0])
    def _():
        y_ref[...] = jnp.zeros_like(y_ref)


def _experts(block_e, n_used, xs, w1, b1, w2, b2, n_blocks):
    d = xs.shape[1]
    f = w2.shape[1]
    br = EXPERT_ROWS
    blk = lambda b, be, nb: (jnp.minimum(b, nb[0] - 1), 0)
    per_e = lambda b, be, nb: (be[b], 0, 0)
    return pl.pallas_call(
        _expert_body,
        out_shape=jax.ShapeDtypeStruct((n_blocks * br, d), F32),
        grid_spec=pltpu.PrefetchScalarGridSpec(
            num_scalar_prefetch=2,
            grid=(n_blocks,),
            in_specs=[pl.BlockSpec((br, d), blk),
                      pl.BlockSpec((1, d, 2 * f), per_e),
                      pl.BlockSpec((1, 1, 2 * f), per_e),
                      pl.BlockSpec((1, f, d), per_e),
                      pl.BlockSpec((1, 1, d), per_e)],
            out_specs=pl.BlockSpec((br, d), lambda b, be, nb: (b, 0))),
        compiler_params=_params("arbitrary"),
        name="experts",
    )(block_e, n_used, xs, w1, b1, w2, b2)


def _combine_body(len_ref, src_ref, dst_ref, slot_ref, x1_ref, gate_ref, g_ref, y_hbm, o_ref,
                  stage_ref, sem, *, final_norm):
    tm = x1_ref.shape[0]
    tile = pl.program_id(0)
    buf = tile % 2

    def copy_into(b):
        return lambda src, dst, size: pltpu.make_async_copy(
            y_hbm.at[pl.ds(dst, size)], stage_ref.at[b, pl.ds(src, size)], sem)

    @pl.when(tile == 0)
    def _():
        stage_ref[...] = jnp.zeros_like(stage_ref)
        _segment_copies(len_ref, src_ref, dst_ref, tile, copy_into(0), wait=False)

    _segment_copies(len_ref, src_ref, dst_ref, tile, copy_into(buf), wait=True)

    @pl.when(tile + 1 < pl.num_programs(0))
    def _():
        _segment_copies(len_ref, src_ref, dst_ref, tile + 1, copy_into(1 - buf), wait=False)

    slots = slot_ref[...]
    gate = gate_ref[...]
    col = lax.broadcasted_iota(I32, (tm, STAGE_ROWS), 1)
    weight = jnp.zeros((tm, STAGE_ROWS), F32)
    for k in range(TOP_K):
        weight = jnp.where(col == slots[:, k:k + 1], gate[:, k:k + 1], weight)
    moe = jnp.dot(weight.astype(BF16), stage_ref[buf].astype(BF16), preferred_element_type=F32)
    acc = x1_ref[...] + moe
    o_ref[...] = _rms(acc, g_ref[...]) if final_norm else acc


def _combine(seg_len, seg_src, seg_dst, slots, x1, gates, g_f, y, final_norm):
    n, d = x1.shape
    tm = MOE_TILE
    tiled = lambda w: pl.BlockSpec((tm, w), lambda i, *_: (i, 0))
    return pl.pallas_call(
        functools.partial(_combine_body, final_norm=final_norm),
        out_shape=jax.ShapeDtypeStruct((n, d), F32),
        grid_spec=pltpu.PrefetchScalarGridSpec(
            num_scalar_prefetch=3,
            grid=(n // tm,),
            in_specs=[tiled(TOP_K), tiled(d), tiled(TOP_K),
                      pl.BlockSpec((1, d), lambda i, *_: (0, 0)),
                      pl.BlockSpec(memory_space=pl.ANY)],
            out_specs=tiled(d),
            scratch_shapes=[pltpu.VMEM((2, STAGE_ROWS, d), F32),
                            pltpu.SemaphoreType.DMA]),
        compiler_params=_params("arbitrary"),
        name="combine",
    )(seg_len, seg_src, seg_dst, slots, x1, gates, g_f, y)


def _layer(x2, bsz, seq, p):
    n, d = x2.shape
    qkv_a, qkv_b = _proj(x2, p["g1"], p["w_in"], seq)
    o_a = _dilated(qkv_a, bsz, seq)
    o_b = _stick(qkv_b, bsz, seq)
    x1, h2, ids, ranks, gates, counts, before = _merge(
        x2, o_a, o_b, p["g1"], p["w_gate"], p["b_gate"], p["w_pa"], p["w_pb"], p["w_out"],
        p["g2"], p["w_r"], p["b_r"])

    br = EXPERT_ROWS
    n_tiles = n // MOE_TILE
    before = before.reshape(n_tiles, N_EXPERTS)
    seg_cnt = jnp.concatenate([before[1:], counts], axis=0) - before
    seg_len = (seg_cnt + SEG_ALIGN - 1) // SEG_ALIGN * SEG_ALIGN
    seg_src = jnp.cumsum(seg_len, axis=1) - seg_len
    rows_e = jnp.sum(seg_len, axis=0)
    padded = (rows_e + br - 1) // br * br
    ends = jnp.cumsum(padded)
    starts = ends - padded
    seg_dst = starts[None, :] + jnp.cumsum(seg_len, axis=0) - seg_len
    n_blocks = (n * TOP_K + n_tiles * N_EXPERTS * (SEG_ALIGN - 1)) // br + N_EXPERTS
    block_row = jnp.arange(n_blocks, dtype=I32) * br
    block_e = jnp.minimum(jnp.sum(ends[None, :] <= block_row[:, None], axis=1),
                          N_EXPERTS - 1).astype(I32)
    n_used = (ends[-1:] // br).astype(I32)
    pad_start = (starts + rows_e // br * br).astype(I32)
    tables = [t.astype(I32).reshape(n_tiles * N_EXPERTS) for t in (seg_len, seg_src, seg_dst)]

    slots = _slots(ids, ranks, (seg_src - before).astype(I32).reshape(n_tiles, 1, N_EXPERTS))
    xs = _dispatch(*tables, pad_start, n_used, slots, h2, n_blocks * br)
    y = _experts(block_e, n_used, xs, p["w1"], p["b1"], p["w2"], p["b2"], n_blocks)
    return tables, slots, x1, gates, y


def kernel(x, norm1_g, w_in, w_proj_a, w_proj_b, w_gate, b_gate, w_out, norm2_g,
           w_router, b_router, w1, b1, w2, b2, norm_f_g):
    bsz, seq, d = x.shape
    depth = w_in.shape[0]
    x2 = x.reshape(bsz * seq, d)
    for l in range(depth):
        p = dict(
            g1=norm1_g[l].reshape(1, d), w_in=w_in[l].astype(BF16),
            w_gate=w_gate[l].astype(BF16), b_gate=b_gate[l].reshape(1, 2 * d),
            w_pa=w_proj_a[l].astype(BF16), w_pb=w_proj_b[l].astype(BF16),
            w_out=w_out[l].astype(BF16), g2=norm2_g[l].reshape(1, d),
            w_r=w_router[l], b_r=b_router[l].reshape(1, N_EXPERTS),
            w1=w1[l].astype(BF16), b1=b1[l].reshape(N_EXPERTS, 1, -1),
            w2=w2[l].astype(BF16), b2=b2[l].reshape(N_EXPERTS, 1, -1))
        tables, slots, x1, gates, y = _layer(x2, bsz, seq, p)
        x2 = _combine(*tables, slots, x1, gates, norm_f_g.reshape(1, d), y,
                      final_norm=(l == depth - 1))
    return x2.reshape(bsz, seq, d)
```

```python
import functools

import jax
import jax.numpy as jnp
from jax import lax
from jax.experimental import pallas as pl
from jax.experimental.pallas import tpu as pltpu

F32 = jnp.float32
BF16 = jnp.bfloat16
I32 = jnp.int32

HEAD_DIM = 64
WIDTH = 512
QKV = 3 * WIDTH
DILATED_GROUPS = ((128, 1), (512, 4), (2048, 16))
BAND = 128
ROPE_THETA = 500000.0
ROT_DIM = HEAD_DIM // 4
N_EXPERTS = 32
TOP_K = 4
SWIGLU_ALPHA = 1.702
SWIGLU_LIMIT = 7.0
NORM_EPS = 1e-5

LANES = 128
VMEM_LIMIT = 56 * 1024 * 1024

LOG2E = 1.4426950408889634

PROJ_ROWS = 512
MERGE_ROWS = 512
STICK_ROWS = 512
STICK_KEYS = 256
EXPERT_ROWS = 512
MOE_TILE = MERGE_ROWS
SEG_ALIGN = 8
STAGE_ROWS = MOE_TILE * TOP_K + N_EXPERTS * SEG_ALIGN
DIL_UNROLL = 8
MIX_ROWS = 512


def _rms(x, g):
    return x * lax.rsqrt(jnp.mean(x * x, axis=-1, keepdims=True) + NORM_EPS) * g


def _params(*sem):
    return pltpu.CompilerParams(dimension_semantics=sem, vmem_limit_bytes=VMEM_LIMIT)


def _const_spec(shape):
    return pl.BlockSpec(shape, lambda *_: (0,) * len(shape), pipeline_mode=pl.Buffered(1))


def _proj_body(x_ref, g_ref, w_ref, cos_ref, sa_ref, sb_ref, a_ref, b_ref):
    h = _rms(x_ref[...], g_ref[...]).astype(BF16)
    rep = WIDTH // LANES
    cos = jnp.concatenate([cos_ref[...]] * rep, axis=1)
    sa = jnp.concatenate([sa_ref[...]] * rep, axis=1)
    sb = jnp.concatenate([sb_ref[...]] * rep, axis=1)
    half = ROT_DIM // 2
    for c in range(6):
        acc = jnp.dot(h, w_ref[:, c * WIDTH:(c + 1) * WIDTH], preferred_element_type=F32)
        if c in (0, 1):
            acc = (acc * cos + pltpu.roll(acc, WIDTH - half, 1) * sa
                   + pltpu.roll(acc, half, 1) * sb)
        if c == 0:
            acc = acc * (HEAD_DIM ** -0.5)
        if c == 3:
            acc = acc * (HEAD_DIM ** -0.5 * LOG2E)
        dst = a_ref if c < 3 else b_ref
        dst[:, (c % 3) * WIDTH:(c % 3 + 1) * WIDTH] = acc.astype(dst.dtype)


def _rotary_tables(seq):
    half = ROT_DIM // 2
    inv_freq = jnp.float32(ROPE_THETA) ** (-jnp.arange(0, ROT_DIM, 2, dtype=F32) / ROT_DIM)
    ang = jnp.arange(seq, dtype=jnp.int32).astype(F32)[:, None] * inv_freq[None, :]
    cos, sin = jnp.cos(ang), jnp.sin(ang)
    ones = jnp.ones((seq, HEAD_DIM - ROT_DIM), F32)
    zeros_h = jnp.zeros((seq, half), F32)
    zeros_r = jnp.zeros((seq, HEAD_DIM - ROT_DIM), F32)
    cos_t = jnp.concatenate([cos, cos, ones], axis=1)
    sa_t = jnp.concatenate([-sin, zeros_h, zeros_r], axis=1)
    sb_t = jnp.concatenate([zeros_h, sin, zeros_r], axis=1)
    rep = LANES // HEAD_DIM
    return tuple(jnp.tile(t, (1, rep)) for t in (cos_t, sa_t, sb_t))


def _proj(x2, g1, w_in, seq):
    n, d = x2.shape
    tm = PROJ_ROWS
    tabs = _rotary_tables(seq)
    per_seq = seq // tm
    tab_spec = pl.BlockSpec((tm, LANES), lambda i: (i % per_seq, 0))
    return pl.pallas_call(
        _proj_body,
        out_shape=(jax.ShapeDtypeStruct((n, QKV), F32), jax.ShapeDtypeStruct((n, QKV), BF16)),
        grid=(n // tm,),
        in_specs=[pl.BlockSpec((tm, d), lambda i: (i, 0)),
                  _const_spec((1, d)),
                  _const_spec((d, 2 * QKV)),
                  tab_spec, tab_spec, tab_spec],
        out_specs=(pl.BlockSpec((tm, QKV), lambda i: (i, 0)),
                   pl.BlockSpec((tm, QKV), lambda i: (i, 0))),
        compiler_params=_params("parallel"),
        name="proj",
    )(x2, g1, w_in, *tabs)


def _dil_body(q_ref, k_ref, v_ref, o_ref, os_ref, ls_ref, *, seq):
    heads = LANES // HEAD_DIM
    lane = lax.broadcasted_iota(I32, (1, LANES), 1)
    masks = [(lane >= h * HEAD_DIM) & (lane < (h + 1) * HEAD_DIM) for h in range(heads)]
    qi = lax.broadcasted_iota(I32, (BAND, 2 * BAND), 0)
    ki = lax.broadcasted_iota(I32, (BAND, 2 * BAND), 1)
    neg = jnp.float32(-1e30)
    band_bias = jnp.where((ki >= qi) & (ki <= qi + BAND), 0.0, neg)
    first_bias = jnp.where(ki >= BAND, 0.0, neg)
    nt = (((1,), (1,)), ((), ()))
    n_blocks = seq // BAND

    def rows(ref, start, dil):
        if dil == 1:
            return ref[0, pl.ds(start, BAND), :]
        return ref[0, pl.ds(start, BAND, stride=dil), :]

    for g, (_, dil) in enumerate(DILATED_GROUPS):
        nb = n_blocks // dil

        def block(n, u, last, g=g, dil=dil, nb=nb):
            r = n // nb
            i = n % nb
            cur = r + i * (BAND * dil)
            q = rows(q_ref, cur, dil).astype(BF16)
            k_cur = rows(k_ref, cur, dil).astype(BF16)
            v_cur = rows(v_ref, cur, dil).astype(BF16)
            if u % nb != 0:
                k_prev, v_prev = last
                bias = band_bias
            elif nb <= DIL_UNROLL:
                k_prev, v_prev = k_cur, v_cur
                bias = band_bias + first_bias
            else:
                prev = r + jnp.maximum(i - 1, 0) * (BAND * dil)
                k_prev = rows(k_ref, prev, dil).astype(BF16)
                v_prev = rows(v_ref, prev, dil).astype(BF16)
                bias = band_bias + jnp.where(i > 0, 0.0, 1.0) * first_bias
            kc = jnp.concatenate([k_prev, k_cur], axis=0)
            vc = jnp.concatenate([v_prev, v_cur], axis=0)
            o_t = jnp.zeros((BAND, LANES), F32)
            l_t = jnp.zeros((BAND, LANES), F32)
            for h in range(heads):
                qh = jnp.where(masks[h], q, jnp.zeros_like(q))
                s = lax.dot_general(qh, kc, nt, preferred_element_type=F32) + bias
                m = jnp.max(s, axis=-1, keepdims=True)
                p = jnp.exp(s - m)
                l = jnp.sum(p, axis=-1, keepdims=True)
                pv = jnp.dot(p.astype(BF16), vc, preferred_element_type=F32)
                o_t = jnp.where(masks[h], pv / l, o_t)
                l_t = jnp.where(masks[h], m + jnp.log(l), l_t)
            if dil == 1:
                os_ref[g, pl.ds(cur, BAND), :] = o_t
                ls_ref[g, pl.ds(cur, BAND), :] = l_t
            else:
                os_ref[g, pl.ds(cur, BAND, stride=dil), :] = o_t
                ls_ref[g, pl.ds(cur, BAND, stride=dil), :] = l_t
            return k_cur, v_cur

        def step(t, carry, block=block):
            last = None
            for u in range(DIL_UNROLL):
                last = block(t * DIL_UNROLL + u, u, last)
            return carry

        lax.fori_loop(0, n_blocks // DIL_UNROLL, step, 0)

    def mix(c, carry):
        r0 = pl.multiple_of(c * MIX_ROWS, MIX_ROWS)
        ls = [ls_ref[g, pl.ds(r0, MIX_ROWS), :] for g in range(len(DILATED_GROUPS))]
        m = functools.reduce(jnp.maximum, ls)
        ws = [jnp.exp(l - m) for l in ls]
        num = sum(os_ref[g, pl.ds(r0, MIX_ROWS), :] * w for g, w in enumerate(ws))
        o_ref[0, pl.ds(r0, MIX_ROWS), :] = (num / sum(ws)).astype(BF16)
        return carry

    lax.fori_loop(0, seq // MIX_ROWS, mix, 0)


def _dilated(qkv_a, bsz, seq):
    assert all(w == BAND * d for w, d in DILATED_GROUPS)
    n_blocks = seq // BAND
    assert seq % BAND == 0 and n_blocks % DIL_UNROLL == 0 and seq % MIX_ROWS == 0
    for _, d in DILATED_GROUPS:
        nb = n_blocks // d
        assert n_blocks % d == 0 and (nb % DIL_UNROLL == 0 or DIL_UNROLL % nb == 0)
    a3 = qkv_a.reshape(bsz, seq, QKV)
    hb = WIDTH // LANES
    n_groups = len(DILATED_GROUPS)
    spec = lambda off: pl.BlockSpec((1, seq, LANES), lambda b, h: (b, 0, off * hb + h))
    return pl.pallas_call(
        functools.partial(_dil_body, seq=seq),
        out_shape=jax.ShapeDtypeStruct((bsz, seq, WIDTH), BF16),
        grid=(bsz, hb),
        in_specs=[spec(0), spec(1), spec(2)],
        out_specs=pl.BlockSpec((1, seq, LANES), lambda b, h: (b, 0, h)),
        scratch_shapes=[pltpu.VMEM((n_groups, seq, LANES), F32),
                        pltpu.VMEM((n_groups, seq, LANES), F32)],
        compiler_params=_params("parallel", "parallel"),
        name="dilated",
    )(a3, a3, a3).reshape(bsz * seq, WIDTH)


def _stick_body(q_ref, k_ref, v_ref, o_ref, acc_ref, carry_ref, z_ref, a_ref, *, tq, tk):
    i = pl.program_id(2)
    heads = LANES // HEAD_DIM
    lane = lax.broadcasted_iota(I32, (1, LANES), 1)
    masks = [(lane >= h * HEAD_DIM) & (lane < (h + 1) * HEAD_DIM) for h in range(heads)]
    lr = lax.broadcasted_iota(I32, (tk, tk), 0)
    lc = lax.broadcasted_iota(I32, (tk, tk), 1)
    later = (lr > lc).astype(BF16)
    nt = (((1,), (1,)), ((), ()))
    q = q_ref[0]
    qh = [jnp.where(mk, q, jnp.zeros_like(q)) for mk in masks]
    acc_ref[...] = jnp.zeros_like(acc_ref)
    carry_ref[...] = jnp.zeros_like(carry_ref)
    nd = tq // tk
    n_before = i * nd

    def first_row(j):
        return pl.multiple_of(jnp.maximum(j, 0) * tk, tk)

    def logits(j, top=0):
        kj = k_ref[0, pl.ds(first_row(j), tk), :]
        return [lax.dot_general(qh[h][top:], kj, nt, preferred_element_type=F32)
                for h in range(heads)]

    def weights(zs, j, diag, top=0):
        rows = tq - top
        if diag:
            ti = lax.broadcasted_iota(I32, (rows, tk), 0)
            si = lax.broadcasted_iota(I32, (rows, tk), 1)
            causal = (first_row(j) + si) < (i * tq + top + ti)
        out = []
        for h, z in enumerate(zs):
            neg_abs = pltpu.bitcast(pltpu.bitcast(z, jnp.uint32) | jnp.uint32(0x80000000), F32)
            sp = jnp.maximum(z, 0.0) + jnp.log2(1.0 + jnp.exp2(neg_abs))
            if diag:
                sp = jnp.where(causal, sp, 0.0)
            c = carry_ref[h, top:, :]
            after = (jnp.dot(sp.astype(BF16), later, preferred_element_type=F32)
                     + jnp.concatenate([c] * (tk // LANES), axis=1))
            a = jnp.exp2(z - sp - after)
            if diag:
                a = jnp.where(causal, a, 0.0)
            carry_ref[h, top:, :] = c + jnp.broadcast_to(jnp.sum(sp, axis=-1, keepdims=True),
                                                         (rows, LANES))
            out.append(a.astype(BF16))
        return out

    def values(a_s, j, top=0):
        vj = v_ref[0, pl.ds(first_row(j), tk), :]
        acc_ref[top:, :] += sum(
            jnp.dot(a_s[h], jnp.where(masks[h], vj, jnp.zeros_like(vj)), preferred_element_type=F32)
            for h in range(heads))

    for dj in reversed(range(nd)):
        j = n_before + dj
        values(weights(logits(j, dj * tk), j, True, dj * tk), j, dj * tk)

    def stage(slot, arrays):
        ref = z_ref if arrays[0].dtype == F32 else a_ref
        for h, arr in enumerate(arrays):
            ref[slot, h] = arr

    def staged(ref, slot):
        return [ref[slot, h] for h in range(heads)]

    a_ref[1] = jnp.zeros_like(a_ref[1])

    @pl.when(i > 0)
    def _():
        stage(0, logits(n_before - 1))

    def step(u, carry):
        for p in range(2):
            j = n_before - 1 - (2 * u + p)
            stage(1 - p, logits(j - 1))
            stage(p, weights(staged(z_ref, p), j, False))
            values(staged(a_ref, 1 - p), j + 1)
        return carry

    lax.fori_loop(0, n_before // 2, step, 0)

    @pl.when(i > 0)
    def _():
        values(staged(a_ref, 1), 0)

    o_ref[0] = acc_ref[...].astype(BF16)


def _stick(qkv_b, bsz, seq):
    tq, tk = STICK_ROWS, STICK_KEYS
    assert tq == 2 * tk and seq % tq == 0
    b3 = qkv_b.reshape(bsz, seq, QKV)
    hb = WIDTH // LANES
    heads = LANES // HEAD_DIM
    return pl.pallas_call(
        functools.partial(_stick_body, tq=tq, tk=tk),
        out_shape=jax.ShapeDtypeStruct((bsz, seq, WIDTH), BF16),
        grid=(bsz, hb, seq // tq),
        in_specs=[pl.BlockSpec((1, tq, LANES), lambda b, h, i: (b, i, h)),
                  pl.BlockSpec((1, seq, LANES), lambda b, h, i: (b, 0, hb + h)),
                  pl.BlockSpec((1, seq, LANES), lambda b, h, i: (b, 0, 2 * hb + h))],
        out_specs=pl.BlockSpec((1, tq, LANES), lambda b, h, i: (b, i, h)),
        scratch_shapes=[pltpu.VMEM((tq, LANES), F32),
                        pltpu.VMEM((heads, tq, LANES), F32),
                        pltpu.VMEM((2, heads, tq, tk), F32),
                        pltpu.VMEM((2, heads, tq, tk), BF16)],
        compiler_params=_params("parallel", "parallel", "parallel"),
        name="stick",
    )(b3, b3, b3).reshape(bsz * seq, WIDTH)


def _merge_body(x_ref, oa_ref, ob_ref,
                g1_ref, wg_ref, bg_ref, pa_ref, pb_ref, wo_ref, g2_ref, wrh_ref, wrl_ref, br_ref,
                x1_ref, h2_ref, ids_ref, rank_ref, gate_ref, cnt_ref, before_ref, carry_ref):
    tm, d = x_ref.shape

    @pl.when(pl.program_id(0) == 0)
    def _():
        carry_ref[...] = jnp.zeros_like(carry_ref)

    def dot(a, w_ref, lo, hi):
        return jnp.dot(a, w_ref[:, lo:hi], preferred_element_type=F32)

    x = x_ref[...]
    h = _rms(x, g1_ref[...]).astype(BF16)
    gate_a = jax.nn.sigmoid(dot(h, wg_ref, 0, d) + bg_ref[:, :d])
    gate_b = jax.nn.sigmoid(dot(h, wg_ref, d, 2 * d) + bg_ref[:, d:])
    merged = (gate_a * dot(oa_ref[...], pa_ref, 0, d)
              + gate_b * dot(ob_ref[...], pb_ref, 0, d)).astype(BF16)
    x1 = x + jnp.concatenate([dot(merged, wo_ref, 0, d // 2), dot(merged, wo_ref, d // 2, d)], axis=1)
    x1_ref[...] = x1
    h2 = _rms(x1, g2_ref[...])
    h2_ref[...] = h2

    h2_hi = h2.astype(BF16)
    h2_lo = (h2 - h2_hi.astype(F32)).astype(BF16)
    logits = (dot(h2_hi, wrh_ref, 0, N_EXPERTS) + dot(h2_lo, wrh_ref, 0, N_EXPERTS)
              + dot(h2_hi, wrl_ref, 0, N_EXPERTS) + br_ref[...])
    eidx = lax.broadcasted_iota(I32, (tm, N_EXPERTS), 1)
    work = logits
    tops, ids, hots = [], [], []
    for _ in range(TOP_K):
        mk = jnp.max(work, axis=-1, keepdims=True)
        ik = jnp.min(jnp.where(work == mk, eidx, N_EXPERTS), axis=-1, keepdims=True)
        hot = eidx == ik
        work = jnp.where(hot, -jnp.inf, work)
        tops.append(mk)
        ids.append(ik)
        hots.append(hot)
    exps = [jnp.exp(t - tops[0]) for t in tops]
    den = exps[0] + exps[1] + exps[2] + exps[3]
    gate_ref[...] = jnp.concatenate([e / den for e in exps], axis=1)
    ids_ref[...] = jnp.concatenate(ids, axis=1)

    cnt = sum(hot.astype(F32) for hot in hots)
    ri = lax.broadcasted_iota(I32, (tm, tm), 0)
    ci = lax.broadcasted_iota(I32, (tm, tm), 1)
    before = (ci < ri).astype(BF16)
    prefix = jnp.dot(before, cnt.astype(BF16), preferred_element_type=F32) + carry_ref[...]
    ranks = [jnp.sum(jnp.where(hot, prefix, 0.0), axis=-1, keepdims=True) for hot in hots]
    rank_ref[...] = jnp.concatenate(ranks, axis=1).astype(I32)
    before_ref[0] = carry_ref[...].astype(I32)
    total = carry_ref[...] + jnp.sum(cnt, axis=0, keepdims=True)
    carry_ref[...] = total
    cnt_ref[...] = total.astype(I32)


def _merge(x2, o_a, o_b, g1, w_gate, b_gate, w_pa, w_pb, w_out, g2, w_r, b_r):
    n, d = x2.shape
    w_r_hi = w_r.astype(BF16)
    w_r_lo = (w_r - w_r_hi.astype(F32)).astype(BF16)
    tm = MERGE_ROWS
    row = lambda w: pl.BlockSpec((tm, w), lambda i: (i, 0))
    return pl.pallas_call(
        _merge_body,
        out_shape=(jax.ShapeDtypeStruct((n, d), F32),
                   jax.ShapeDtypeStruct((n, d), F32),
                   jax.ShapeDtypeStruct((n, TOP_K), I32),
                   jax.ShapeDtypeStruct((n, TOP_K), I32),
                   jax.ShapeDtypeStruct((n, TOP_K), F32),
                   jax.ShapeDtypeStruct((1, N_EXPERTS), I32),
                   jax.ShapeDtypeStruct((n // tm, 1, N_EXPERTS), I32)),
        grid=(n // tm,),
        in_specs=[row(d), row(WIDTH), row(WIDTH),
                  _const_spec((1, d)), _const_spec((d, 2 * d)), _const_spec((1, 2 * d)),
                  _const_spec((WIDTH, d)), _const_spec((WIDTH, d)), _const_spec((d, d)),
                  _const_spec((1, d)), _const_spec((d, N_EXPERTS)), _const_spec((d, N_EXPERTS)),
                  _const_spec((1, N_EXPERTS))],
        out_specs=(row(d), row(d), row(TOP_K), row(TOP_K), row(TOP_K),
                   pl.BlockSpec((1, N_EXPERTS), lambda i: (0, 0)),
                   pl.BlockSpec((1, 1, N_EXPERTS), lambda i: (i, 0, 0))),
        scratch_shapes=[pltpu.VMEM((1, N_EXPERTS), F32)],
        compiler_params=_params("arbitrary"),
        name="merge",
    )(x2, o_a, o_b, g1, w_gate, b_gate, w_pa, w_pb, w_out, g2, w_r_hi, w_r_lo, b_r)


def _tile_slots(ids_ref, rank_ref, adj_ref):
    tm = ids_ref.shape[0]
    eidx = lax.broadcasted_iota(I32, (tm, N_EXPERTS), 1)
    ids = ids_ref[...]
    cols = []
    for k in range(TOP_K):
        hot = eidx == ids[:, k:k + 1]
        cols.append(jnp.sum(jnp.where(hot, adj_ref[0], 0), axis=-1, keepdims=True))
    return jnp.concatenate(cols, axis=1) + rank_ref[...]


def _segment_copies(len_ref, src_ref, dst_ref, tile, make_copy, wait):
    for e in range(N_EXPERTS):
        base = tile * N_EXPERTS + e
        length, src, dst = len_ref[base], src_ref[base], dst_ref[base]
        size = MOE_TILE
        while size >= SEG_ALIGN:
            done = (length // (2 * size)) * (2 * size)

            @pl.when((length & size) != 0)
            def _(size=size, done=done):
                cp = make_copy(pl.multiple_of(src + done, SEG_ALIGN),
                               pl.multiple_of(dst + done, SEG_ALIGN), size)
                if wait:
                    cp.wait()
                else:
                    cp.start()

            size //= 2


def _dispatch_body(len_ref, src_ref, dst_ref, pad_ref, nb_ref, ids_ref, rank_ref, adj_ref, h_ref,
                   xs_hbm, stage_ref, zero_ref, sem, zero_sem):
    tm = h_ref.shape[0]
    n_blocks = xs_hbm.shape[0] // EXPERT_ROWS
    tile = pl.program_id(0)
    buf = tile % 2

    @pl.when(tile == 0)
    def _():
        zero_ref[...] = jnp.zeros_like(zero_ref)
        for e in range(N_EXPERTS):
            r0 = pl.multiple_of(pad_ref[e], EXPERT_ROWS)
            pltpu.make_async_copy(zero_ref, xs_hbm.at[pl.ds(r0, EXPERT_ROWS)], zero_sem).start()
        for e in range(N_EXPERTS):
            pltpu.make_async_copy(zero_ref, xs_hbm.at[pl.ds(0, EXPERT_ROWS)], zero_sem).wait()

        def fill(b, carry):
            r0 = pl.multiple_of(b * EXPERT_ROWS, EXPERT_ROWS)
            cp = pltpu.make_async_copy(zero_ref, xs_hbm.at[pl.ds(r0, EXPERT_ROWS)], zero_sem)
            cp.start()
            cp.wait()
            return carry

        lax.fori_loop(nb_ref[0], n_blocks, fill, 0)

    slots = _tile_slots(ids_ref, rank_ref, adj_ref)
    col = lax.broadcasted_iota(I32, (tm, STAGE_ROWS), 1)
    hit = col == slots[:, 0:1]
    for k in range(1, TOP_K):
        hit = hit | (col == slots[:, k:k + 1])
    pick = jnp.where(hit, 1.0, 0.0).astype(BF16)
    stage_ref[buf] = lax.dot_general(pick, h_ref[...].astype(BF16), (((0,), (0,)), ((), ())),
                                     preferred_element_type=F32)

    def copy_from(b):
        return lambda src, dst, size: pltpu.make_async_copy(
            stage_ref.at[b, pl.ds(src, size)], xs_hbm.at[pl.ds(dst, size)], sem)

    @pl.when(tile > 0)
    def _():
        _segment_copies(len_ref, src_ref, dst_ref, tile - 1, copy_from(1 - buf), wait=True)

    _segment_copies(len_ref, src_ref, dst_ref, tile, copy_from(buf), wait=False)

    @pl.when(tile == pl.num_programs(0) - 1)
    def _():
        _segment_copies(len_ref, src_ref, dst_ref, tile, copy_from(buf), wait=True)


def _dispatch(seg_len, seg_src, seg_dst, pad_start, n_used, ids, ranks, adj, h2, n_rows):
    n, d = h2.shape
    tm = MOE_TILE
    tiled = lambda w: pl.BlockSpec((tm, w), lambda i, *_: (i, 0))
    adj_spec = pl.BlockSpec((1, 1, N_EXPERTS), lambda i, *_: (i, 0, 0))
    return pl.pallas_call(
        _dispatch_body,
        out_shape=jax.ShapeDtypeStruct((n_rows, d), F32),
        grid_spec=pltpu.PrefetchScalarGridSpec(
            num_scalar_prefetch=5,
            grid=(n // tm,),
            in_specs=[tiled(TOP_K), tiled(TOP_K), adj_spec, tiled(d)],
            out_specs=pl.BlockSpec(memory_space=pl.ANY),
            scratch_shapes=[pltpu.VMEM((2, STAGE_ROWS, d), F32),
                            pltpu.VMEM((EXPERT_ROWS, d), F32),
                            pltpu.SemaphoreType.DMA,
                            pltpu.SemaphoreType.DMA]),
        compiler_params=_params("arbitrary"),
        name="dispatch",
    )(seg_len, seg_src, seg_dst, pad_start, n_used, ids, ranks, adj, h2)


def _expert_body(be_ref, nb_ref, x_ref, w1_ref, b1_ref, w2_ref, b2_ref, y_ref, w1b_ref, w2b_ref):
    f = w2_ref.shape[1]
    b = pl.program_id(0)

    @pl.when((b == 0) | (be_ref[b] != be_ref[jnp.maximum(b - 1, 0)]))
    def _():
        w1b_ref[...] = w1_ref[0].astype(BF16)
        w2b_ref[...] = w2_ref[0].astype(BF16)

    @pl.when(b < nb_ref[0])
    def _():
        xb = x_ref[...].astype(BF16)
        hid = jnp.dot(xb, w1b_ref[...], preferred_element_type=F32) + b1_ref[0]
        glu = jnp.minimum(hid[:, :f], SWIGLU_LIMIT)
        lin = jnp.clip(hid[:, f:], -SWIGLU_LIMIT, SWIGLU_LIMIT)
        act = glu * jax.nn.sigmoid(SWIGLU_ALPHA * glu) * (lin + 1.0)
        y_ref[...] = jnp.dot(act.astype(BF16), w2b_ref[...], preferred_element_type=F32) + b2_ref[0]

    @pl.when(b >= nb_ref[0])
    def _():
        y_ref[...] = jnp.zeros_like(y_ref)


def _experts(block_e, n_used, xs, w1, b1, w2, b2, n_blocks):
    d = xs.shape[1]
    f = w2.shape[1]
    br = EXPERT_ROWS
    blk = lambda b, be, nb: (jnp.minimum(b, nb[0] - 1), 0)
    per_e = lambda b, be, nb: (be[b], 0, 0)
    return pl.pallas_call(
        _expert_body,
        out_shape=jax.ShapeDtypeStruct((n_blocks * br, d), F32),
        grid_spec=pltpu.PrefetchScalarGridSpec(
            num_scalar_prefetch=2,
            grid=(n_blocks,),
            in_specs=[pl.BlockSpec((br, d), blk),
                      pl.BlockSpec((1, d, 2 * f), per_e),
                      pl.BlockSpec((1, 1, 2 * f), per_e),
                      pl.BlockSpec((1, f, d), per_e),
                      pl.BlockSpec((1, 1, d), per_e)],
            out_specs=pl.BlockSpec((br, d), lambda b, be, nb: (b, 0)),
            scratch_shapes=[pltpu.VMEM((d, 2 * f), BF16), pltpu.VMEM((f, d), BF16)]),
        compiler_params=_params("arbitrary"),
        name="experts",
    )(block_e, n_used, xs, w1, b1, w2, b2)


def _combine_body(len_ref, src_ref, dst_ref, ids_ref, rank_ref, adj_ref, x1_ref, gate_ref, g_ref,
                  y_hbm, o_ref, stage_ref, sem, *, final_norm):
    tm = x1_ref.shape[0]
    tile = pl.program_id(0)
    buf = tile % 2

    def copy_into(b):
        return lambda src, dst, size: pltpu.make_async_copy(
            y_hbm.at[pl.ds(dst, size)], stage_ref.at[b, pl.ds(src, size)], sem)

    @pl.when(tile == 0)
    def _():
        stage_ref[...] = jnp.zeros_like(stage_ref)
        _segment_copies(len_ref, src_ref, dst_ref, tile, copy_into(0), wait=False)

    _segment_copies(len_ref, src_ref, dst_ref, tile, copy_into(buf), wait=True)

    @pl.when(tile + 1 < pl.num_programs(0))
    def _():
        _segment_copies(len_ref, src_ref, dst_ref, tile + 1, copy_into(1 - buf), wait=False)

    slots = _tile_slots(ids_ref, rank_ref, adj_ref)
    gate = gate_ref[...]
    col = lax.broadcasted_iota(I32, (tm, STAGE_ROWS), 1)
    weight = jnp.zeros((tm, STAGE_ROWS), F32)
    for k in range(TOP_K):
        weight = jnp.where(col == slots[:, k:k + 1], gate[:, k:k + 1], weight)
    moe = jnp.dot(weight.astype(BF16), stage_ref[buf].astype(BF16), preferred_element_type=F32)
    acc = x1_ref[...] + moe
    o_ref[...] = _rms(acc, g_ref[...]) if final_norm else acc


def _combine(seg_len, seg_src, seg_dst, ids, ranks, adj, x1, gates, g_f, y, final_norm):
    n, d = x1.shape
    tm = MOE_TILE
    tiled = lambda w: pl.BlockSpec((tm, w), lambda i, *_: (i, 0))
    adj_spec = pl.BlockSpec((1, 1, N_EXPERTS), lambda i, *_: (i, 0, 0))
    return pl.pallas_call(
        functools.partial(_combine_body, final_norm=final_norm),
        out_shape=jax.ShapeDtypeStruct((n, d), F32),
        grid_spec=pltpu.PrefetchScalarGridSpec(
            num_scalar_prefetch=3,
            grid=(n // tm,),
            in_specs=[tiled(TOP_K), tiled(TOP_K), adj_spec, tiled(d), tiled(TOP_K),
                      pl.BlockSpec((1, d), lambda i, *_: (0, 0)),
                      pl.BlockSpec(memory_space=pl.ANY)],
            out_specs=tiled(d),
            scratch_shapes=[pltpu.VMEM((2, STAGE_ROWS, d), F32),
                            pltpu.SemaphoreType.DMA]),
        compiler_params=_params("arbitrary"),
        name="combine",
    )(seg_len, seg_src, seg_dst, ids, ranks, adj, x1, gates, g_f, y)


def _layer(x2, bsz, seq, p):
    n, d = x2.shape
    qkv_a, qkv_b = _proj(x2, p["g1"], p["w_in"], seq)
    o_a = _dilated(qkv_a, bsz, seq)
    o_b = _stick(qkv_b, bsz, seq)
    x1, h2, ids, ranks, gates, counts, before = _merge(
        x2, o_a, o_b, p["g1"], p["w_gate"], p["b_gate"], p["w_pa"], p["w_pb"], p["w_out"],
        p["g2"], p["w_r"], p["b_r"])

    br = EXPERT_ROWS
    n_tiles = n // MOE_TILE
    before = before.reshape(n_tiles, N_EXPERTS)
    seg_cnt = jnp.concatenate([before[1:], counts], axis=0) - before
    seg_len = (seg_cnt + SEG_ALIGN - 1) // SEG_ALIGN * SEG_ALIGN
    seg_src = jnp.cumsum(seg_len, axis=1) - seg_len
    rows_e = jnp.sum(seg_len, axis=0)
    padded = (rows_e + br - 1) // br * br
    ends = jnp.cumsum(padded)
    starts = ends - padded
    seg_dst = starts[None, :] + jnp.cumsum(seg_len, axis=0) - seg_len
    n_blocks = (n * TOP_K + n_tiles * N_EXPERTS * (SEG_ALIGN - 1)) // br + N_EXPERTS
    block_row = jnp.arange(n_blocks, dtype=I32) * br
    block_e = jnp.minimum(jnp.sum(ends[None, :] <= block_row[:, None], axis=1),
                          N_EXPERTS - 1).astype(I32)
    n_used = (ends[-1:] // br).astype(I32)
    pad_start = (starts + rows_e // br * br).astype(I32)
    tables = [t.astype(I32).reshape(n_tiles * N_EXPERTS) for t in (seg_len, seg_src, seg_dst)]

    adj = (seg_src - before).astype(I32).reshape(n_tiles, 1, N_EXPERTS)
    route = (*tables, ids, ranks, adj)
    xs = _dispatch(*tables, pad_start, n_used, ids, ranks, adj, h2, n_blocks * br)
    y = _experts(block_e, n_used, xs, p["w1"], p["b1"], p["w2"], p["b2"], n_blocks)
    return route, x1, gates, y


def kernel(x, norm1_g, w_in, w_proj_a, w_proj_b, w_gate, b_gate, w_out, norm2_g,
           w_router, b_router, w1, b1, w2, b2, norm_f_g):
    bsz, seq, d = x.shape
    depth = w_in.shape[0]
    x2 = x.reshape(bsz * seq, d)
    for l in range(depth):
        p = dict(
            g1=norm1_g[l].reshape(1, d), w_in=w_in[l].astype(BF16),
            w_gate=w_gate[l].astype(BF16), b_gate=b_gate[l].reshape(1, 2 * d),
            w_pa=w_proj_a[l].astype(BF16), w_pb=w_proj_b[l].astype(BF16),
            w_out=w_out[l].astype(BF16), g2=norm2_g[l].reshape(1, d),
            w_r=w_router[l], b_r=b_router[l].reshape(1, N_EXPERTS),
            w1=w1[l], b1=b1[l].reshape(N_EXPERTS, 1, -1),
            w2=w2[l], b2=b2[l].reshape(N_EXPERTS, 1, -1))
        route, x1, gates, y = _layer(x2, bsz, seq, p)
        x2 = _combine(*route, x1, gates, norm_f_g.reshape(1, d), y, final_norm=(l == depth - 1))
    return x2.reshape(bsz, seq, d)
```

```python
import functools

import jax
import jax.numpy as jnp
from jax import lax
from jax.experimental import pallas as pl
from jax.experimental.pallas import tpu as pltpu

F32 = jnp.float32
BF16 = jnp.bfloat16
I32 = jnp.int32

HEAD_DIM = 64
WIDTH = 512
QKV = 3 * WIDTH
DILATED_GROUPS = ((128, 1), (512, 4), (2048, 16))
BAND = 128
ROPE_THETA = 500000.0
ROT_DIM = HEAD_DIM // 4
N_EXPERTS = 32
TOP_K = 4
SWIGLU_ALPHA = 1.702
SWIGLU_LIMIT = 7.0
NORM_EPS = 1e-5

LANES = 128
VMEM_LIMIT = 56 * 1024 * 1024

LOG2E = 1.4426950408889634

PROJ_ROWS = 512
MERGE_ROWS = 512
STICK_ROWS = 512
STICK_KEYS = 256
EXPERT_ROWS = 512
MOE_TILE = MERGE_ROWS
SEG_ALIGN = 8
STAGE_ROWS = MOE_TILE * TOP_K + N_EXPERTS * SEG_ALIGN
DIL_UNROLL = 8
MIX_ROWS = 512


def _rms(x, g):
    return x * lax.rsqrt(jnp.mean(x * x, axis=-1, keepdims=True) + NORM_EPS) * g


def _params(*sem):
    return pltpu.CompilerParams(dimension_semantics=sem, vmem_limit_bytes=VMEM_LIMIT)


def _const_spec(shape):
    return pl.BlockSpec(shape, lambda *_: (0,) * len(shape), pipeline_mode=pl.Buffered(1))


def _proj_body(x_ref, g_ref, w_ref, cos_ref, sa_ref, sb_ref, a_ref, b_ref):
    h = _rms(x_ref[...], g_ref[...]).astype(BF16)
    rep = WIDTH // LANES
    cos = jnp.concatenate([cos_ref[...]] * rep, axis=1)
    sa = jnp.concatenate([sa_ref[...]] * rep, axis=1)
    sb = jnp.concatenate([sb_ref[...]] * rep, axis=1)
    half = ROT_DIM // 2
    for c in range(6):
        acc = jnp.dot(h, w_ref[:, c * WIDTH:(c + 1) * WIDTH], preferred_element_type=F32)
        if c in (0, 1):
            acc = (acc * cos + pltpu.roll(acc, WIDTH - half, 1) * sa
                   + pltpu.roll(acc, half, 1) * sb)
        if c == 0:
            acc = acc * (HEAD_DIM ** -0.5)
        if c == 3:
            acc = acc * (HEAD_DIM ** -0.5 * LOG2E)
        dst = a_ref if c < 3 else b_ref
        dst[:, (c % 3) * WIDTH:(c % 3 + 1) * WIDTH] = acc.astype(dst.dtype)


def _rotary_tables(seq):
    half = ROT_DIM // 2
    inv_freq = jnp.float32(ROPE_THETA) ** (-jnp.arange(0, ROT_DIM, 2, dtype=F32) / ROT_DIM)
    ang = jnp.arange(seq, dtype=jnp.int32).astype(F32)[:, None] * inv_freq[None, :]
    cos, sin = jnp.cos(ang), jnp.sin(ang)
    ones = jnp.ones((seq, HEAD_DIM - ROT_DIM), F32)
    zeros_h = jnp.zeros((seq, half), F32)
    zeros_r = jnp.zeros((seq, HEAD_DIM - ROT_DIM), F32)
    cos_t = jnp.concatenate([cos, cos, ones], axis=1)
    sa_t = jnp.concatenate([-sin, zeros_h, zeros_r], axis=1)
    sb_t = jnp.concatenate([zeros_h, sin, zeros_r], axis=1)
    rep = LANES // HEAD_DIM
    return tuple(jnp.tile(t, (1, rep)) for t in (cos_t, sa_t, sb_t))


def _proj(x2, g1, w_in, seq):
    n, d = x2.shape
    tm = PROJ_ROWS
    tabs = _rotary_tables(seq)
    per_seq = seq // tm
    tab_spec = pl.BlockSpec((tm, LANES), lambda i: (i % per_seq, 0))
    return pl.pallas_call(
        _proj_body,
        out_shape=(jax.ShapeDtypeStruct((n, QKV), F32), jax.ShapeDtypeStruct((n, QKV), BF16)),
        grid=(n // tm,),
        in_specs=[pl.BlockSpec((tm, d), lambda i: (i, 0)),
                  _const_spec((1, d)),
                  _const_spec((d, 2 * QKV)),
                  tab_spec, tab_spec, tab_spec],
        out_specs=(pl.BlockSpec((tm, QKV), lambda i: (i, 0)),
                   pl.BlockSpec((tm, QKV), lambda i: (i, 0))),
        compiler_params=_params("parallel"),
        name="proj",
    )(x2, g1, w_in, *tabs)


def _dil_body(q_ref, k_ref, v_ref, o_ref, os_ref, ls_ref, *, seq):
    heads = LANES // HEAD_DIM
    lane = lax.broadcasted_iota(I32, (1, LANES), 1)
    masks = [(lane >= h * HEAD_DIM) & (lane < (h + 1) * HEAD_DIM) for h in range(heads)]
    qi = lax.broadcasted_iota(I32, (BAND, 2 * BAND), 0)
    ki = lax.broadcasted_iota(I32, (BAND, 2 * BAND), 1)
    neg = jnp.float32(-1e30)
    band_bias = jnp.where((ki >= qi) & (ki <= qi + BAND), 0.0, neg)
    first_bias = jnp.where(ki >= BAND, 0.0, neg)
    nt = (((1,), (1,)), ((), ()))
    n_blocks = seq // BAND

    def rows(ref, start, dil):
        if dil == 1:
            return ref[0, pl.ds(start, BAND), :]
        return ref[0, pl.ds(start, BAND, stride=dil), :]

    for g, (_, dil) in enumerate(DILATED_GROUPS):
        nb = n_blocks // dil

        def block(n, u, last, g=g, dil=dil, nb=nb):
            r = n // nb
            i = n % nb
            cur = r + i * (BAND * dil)
            q = rows(q_ref, cur, dil).astype(BF16)
            k_cur = rows(k_ref, cur, dil).astype(BF16)
            v_cur = rows(v_ref, cur, dil).astype(BF16)
            if u % nb != 0:
                k_prev, v_prev = last
                bias = band_bias
            elif nb <= DIL_UNROLL:
                k_prev, v_prev = k_cur, v_cur
                bias = band_bias + first_bias
            else:
                prev = r + jnp.maximum(i - 1, 0) * (BAND * dil)
                k_prev = rows(k_ref, prev, dil).astype(BF16)
                v_prev = rows(v_ref, prev, dil).astype(BF16)
                bias = band_bias + jnp.where(i > 0, 0.0, 1.0) * first_bias
            kc = jnp.concatenate([k_prev, k_cur], axis=0)
            vc = jnp.concatenate([v_prev, v_cur], axis=0)
            o_t = jnp.zeros((BAND, LANES), F32)
            l_t = jnp.zeros((BAND, LANES), F32)
            for h in range(heads):
                qh = jnp.where(masks[h], q, jnp.zeros_like(q))
                s = lax.dot_general(qh, kc, nt, preferred_element_type=F32) + bias
                m = jnp.max(s, axis=-1, keepdims=True)
                p = jnp.exp(s - m)
                l = jnp.sum(p, axis=-1, keepdims=True)
                pv = jnp.dot(p.astype(BF16), vc, preferred_element_type=F32)
                o_t = jnp.where(masks[h], pv / l, o_t)
                l_t = jnp.where(masks[h], m + jnp.log(l), l_t)
            if dil == 1:
                os_ref[g, pl.ds(cur, BAND), :] = o_t
                ls_ref[g, pl.ds(cur, BAND), :] = l_t
            else:
                os_ref[g, pl.ds(cur, BAND, stride=dil), :] = o_t
                ls_ref[g, pl.ds(cur, BAND, stride=dil), :] = l_t
            return k_cur, v_cur

        def step(t, carry, block=block):
            last = None
            for u in range(DIL_UNROLL):
                last = block(t * DIL_UNROLL + u, u, last)
            return carry

        lax.fori_loop(0, n_blocks // DIL_UNROLL, step, 0)

    def mix(c, carry):
        r0 = pl.multiple_of(c * MIX_ROWS, MIX_ROWS)
        ls = [ls_ref[g, pl.ds(r0, MIX_ROWS), :] for g in range(len(DILATED_GROUPS))]
        m = functools.reduce(jnp.maximum, ls)
        ws = [jnp.exp(l - m) for l in ls]
        num = sum(os_ref[g, pl.ds(r0, MIX_ROWS), :] * w for g, w in enumerate(ws))
        o_ref[0, pl.ds(r0, MIX_ROWS), :] = (num / sum(ws)).astype(BF16)
        return carry

    lax.fori_loop(0, seq // MIX_ROWS, mix, 0)


def _dilated(qkv_a, bsz, seq):
    assert all(w == BAND * d for w, d in DILATED_GROUPS)
    n_blocks = seq // BAND
    assert seq % BAND == 0 and n_blocks % DIL_UNROLL == 0 and seq % MIX_ROWS == 0
    for _, d in DILATED_GROUPS:
        nb = n_blocks // d
        assert n_blocks % d == 0 and (nb % DIL_UNROLL == 0 or DIL_UNROLL % nb == 0)
    a3 = qkv_a.reshape(bsz, seq, QKV)
    hb = WIDTH // LANES
    n_groups = len(DILATED_GROUPS)
    spec = lambda off: pl.BlockSpec((1, seq, LANES), lambda b, h: (b, 0, off * hb + h))
    return pl.pallas_call(
        functools.partial(_dil_body, seq=seq),
        out_shape=jax.ShapeDtypeStruct((bsz, seq, WIDTH), BF16),
        grid=(bsz, hb),
        in_specs=[spec(0), spec(1), spec(2)],
        out_specs=pl.BlockSpec((1, seq, LANES), lambda b, h: (b, 0, h)),
        scratch_shapes=[pltpu.VMEM((n_groups, seq, LANES), F32),
                        pltpu.VMEM((n_groups, seq, LANES), F32)],
        compiler_params=_params("parallel", "parallel"),
        name="dilated",
    )(a3, a3, a3).reshape(bsz * seq, WIDTH)


def _stick_body(q_ref, k_ref, v_ref, o_ref, acc_ref, carry_ref, z_ref, a_ref, *, tq, tk):
    i = pl.program_id(2)
    heads = LANES // HEAD_DIM
    lane = lax.broadcasted_iota(I32, (1, LANES), 1)
    masks = [(lane >= h * HEAD_DIM) & (lane < (h + 1) * HEAD_DIM) for h in range(heads)]
    lr = lax.broadcasted_iota(I32, (tk, tk), 0)
    lc = lax.broadcasted_iota(I32, (tk, tk), 1)
    later = (lr > lc).astype(BF16)
    nt = (((1,), (1,)), ((), ()))
    q = q_ref[0]
    qh = [jnp.where(mk, q, jnp.zeros_like(q)) for mk in masks]
    acc_ref[...] = jnp.zeros_like(acc_ref)
    carry_ref[...] = jnp.zeros_like(carry_ref)
    nd = tq // tk
    n_before = i * nd

    def first_row(j):
        return pl.multiple_of(jnp.maximum(j, 0) * tk, tk)

    def logits(j, top=0):
        kj = k_ref[0, pl.ds(first_row(j), tk), :]
        return [lax.dot_general(qh[h][top:], kj, nt, preferred_element_type=F32)
                for h in range(heads)]

    def weights(zs, j, diag, top=0):
        rows = tq - top
        if diag:
            ti = lax.broadcasted_iota(I32, (rows, tk), 0)
            si = lax.broadcasted_iota(I32, (rows, tk), 1)
            causal = (first_row(j) + si) < (i * tq + top + ti)
        out = []
        for h, z in enumerate(zs):
            neg_abs = pltpu.bitcast(pltpu.bitcast(z, jnp.uint32) | jnp.uint32(0x80000000), F32)
            sp = jnp.maximum(z, 0.0) + jnp.log2(1.0 + jnp.exp2(neg_abs))
            if diag:
                sp = jnp.where(causal, sp, 0.0)
            c = carry_ref[h, top:, :]
            after = (jnp.dot(sp.astype(BF16), later, preferred_element_type=F32)
                     + jnp.concatenate([c] * (tk // LANES), axis=1))
            a = jnp.exp2(z - sp - after)
            if diag:
                a = jnp.where(causal, a, 0.0)
            carry_ref[h, top:, :] = c + jnp.broadcast_to(jnp.sum(sp, axis=-1, keepdims=True),
                                                         (rows, LANES))
            out.append(a.astype(BF16))
        return out

    def values(a_s, j, top=0):
        vj = v_ref[0, pl.ds(first_row(j), tk), :]
        acc_ref[top:, :] += sum(
            jnp.dot(a_s[h], jnp.where(masks[h], vj, jnp.zeros_like(vj)), preferred_element_type=F32)
            for h in range(heads))

    for dj in reversed(range(nd)):
        j = n_before + dj
        values(weights(logits(j, dj * tk), j, True, dj * tk), j, dj * tk)

    def stage(slot, arrays):
        ref = z_ref if arrays[0].dtype == F32 else a_ref
        for h, arr in enumerate(arrays):
            ref[slot, h] = arr

    def staged(ref, slot):
        return [ref[slot, h] for h in range(heads)]

    a_ref[1] = jnp.zeros_like(a_ref[1])

    @pl.when(i > 0)
    def _():
        stage(0, logits(n_before - 1))

    def step(u, carry):
        for p in range(2):
            j = n_before - 1 - (2 * u + p)
            stage(1 - p, logits(j - 1))
            stage(p, weights(staged(z_ref, p), j, False))
            values(staged(a_ref, 1 - p), j + 1)
        return carry

    lax.fori_loop(0, n_before // 2, step, 0)

    @pl.when(i > 0)
    def _():
        values(staged(a_ref, 1), 0)

    o_ref[0] = acc_ref[...].astype(BF16)


def _stick(qkv_b, bsz, seq):
    tq, tk = STICK_ROWS, STICK_KEYS
    assert tq == 2 * tk and seq % tq == 0
    b3 = qkv_b.reshape(bsz, seq, QKV)
    hb = WIDTH // LANES
    heads = LANES // HEAD_DIM
    return pl.pallas_call(
        functools.partial(_stick_body, tq=tq, tk=tk),
        out_shape=jax.ShapeDtypeStruct((bsz, seq, WIDTH), BF16),
        grid=(bsz, hb, seq // tq),
        in_specs=[pl.BlockSpec((1, tq, LANES), lambda b, h, i: (b, i, h)),
                  pl.BlockSpec((1, seq, LANES), lambda b, h, i: (b, 0, hb + h)),
                  pl.BlockSpec((1, seq, LANES), lambda b, h, i: (b, 0, 2 * hb + h))],
        out_specs=pl.BlockSpec((1, tq, LANES), lambda b, h, i: (b, i, h)),
        scratch_shapes=[pltpu.VMEM((tq, LANES), F32),
                        pltpu.VMEM((heads, tq, LANES), F32),
                        pltpu.VMEM((2, heads, tq, tk), F32),
                        pltpu.VMEM((2, heads, tq, tk), BF16)],
        compiler_params=_params("parallel", "parallel", "parallel"),
        name="stick",
    )(b3, b3, b3).reshape(bsz * seq, WIDTH)


def _merge_body(x_ref, oa_ref, ob_ref,
                g1_ref, wg_ref, bg_ref, pa_ref, pb_ref, wo_ref, g2_ref, wrh_ref, wrl_ref, br_ref,
                x1_ref, h2_ref, slot_ref, gate_ref, cnt_ref, before_ref, carry_ref):
    tm, d = x_ref.shape

    @pl.when(pl.program_id(0) == 0)
    def _():
        carry_ref[...] = jnp.zeros_like(carry_ref)

    def dot(a, w_ref, lo, hi):
        return jnp.dot(a, w_ref[:, lo:hi], preferred_element_type=F32)

    x = x_ref[...]
    h = _rms(x, g1_ref[...]).astype(BF16)
    gate_a = jax.nn.sigmoid(dot(h, wg_ref, 0, d) + bg_ref[:, :d])
    gate_b = jax.nn.sigmoid(dot(h, wg_ref, d, 2 * d) + bg_ref[:, d:])
    merged = (gate_a * dot(oa_ref[...], pa_ref, 0, d)
              + gate_b * dot(ob_ref[...], pb_ref, 0, d)).astype(BF16)
    x1 = x + jnp.concatenate([dot(merged, wo_ref, 0, d // 2), dot(merged, wo_ref, d // 2, d)], axis=1)
    x1_ref[...] = x1
    h2 = _rms(x1, g2_ref[...])
    h2_ref[...] = h2

    h2_hi = h2.astype(BF16)
    h2_lo = (h2 - h2_hi.astype(F32)).astype(BF16)
    logits = (dot(h2_hi, wrh_ref, 0, N_EXPERTS) + dot(h2_lo, wrh_ref, 0, N_EXPERTS)
              + dot(h2_hi, wrl_ref, 0, N_EXPERTS) + br_ref[...])
    eidx = lax.broadcasted_iota(I32, (tm, N_EXPERTS), 1)
    work = logits
    tops, hots = [], []
    for _ in range(TOP_K):
        mk = jnp.max(work, axis=-1, keepdims=True)
        ik = jnp.min(jnp.where(work == mk, eidx, N_EXPERTS), axis=-1, keepdims=True)
        hot = eidx == ik
        work = jnp.where(hot, -jnp.inf, work)
        tops.append(mk)
        hots.append(hot)
    exps = [jnp.exp(t - tops[0]) for t in tops]
    den = exps[0] + exps[1] + exps[2] + exps[3]
    gate_ref[...] = jnp.concatenate([e / den for e in exps], axis=1)

    cnt = sum(hot.astype(F32) for hot in hots)
    ri = lax.broadcasted_iota(I32, (tm, tm), 0)
    ci = lax.broadcasted_iota(I32, (tm, tm), 1)
    earlier = (ci < ri).astype(BF16)
    rank = jnp.dot(earlier, cnt.astype(BF16), preferred_element_type=F32)
    tile_cnt = jnp.sum(cnt, axis=0, keepdims=True)
    seg_len = jnp.ceil(tile_cnt * (1.0 / SEG_ALIGN)) * SEG_ALIGN
    ei = lax.broadcasted_iota(I32, (N_EXPERTS, N_EXPERTS), 0)
    ej = lax.broadcasted_iota(I32, (N_EXPERTS, N_EXPERTS), 1)
    seg_start = jnp.dot(jnp.broadcast_to(seg_len, (8, N_EXPERTS)).astype(BF16),
                        (ei < ej).astype(BF16), preferred_element_type=F32)[0:1]
    place = rank + seg_start
    slots = [jnp.sum(jnp.where(hot, place, 0.0), axis=-1, keepdims=True) for hot in hots]
    slot_ref[...] = jnp.concatenate(slots, axis=1).astype(I32)
    before_ref[0] = carry_ref[...].astype(I32)
    total = carry_ref[...] + tile_cnt
    carry_ref[...] = total
    cnt_ref[...] = total.astype(I32)


def _merge(x2, o_a, o_b, g1, w_gate, b_gate, w_pa, w_pb, w_out, g2, w_r, b_r):
    n, d = x2.shape
    w_r_hi = w_r.astype(BF16)
    w_r_lo = (w_r - w_r_hi.astype(F32)).astype(BF16)
    tm = MERGE_ROWS
    row = lambda w: pl.BlockSpec((tm, w), lambda i: (i, 0))
    return pl.pallas_call(
        _merge_body,
        out_shape=(jax.ShapeDtypeStruct((n, d), F32),
                   jax.ShapeDtypeStruct((n, d), F32),
                   jax.ShapeDtypeStruct((n, TOP_K), I32),
                   jax.ShapeDtypeStruct((n, TOP_K), F32),
                   jax.ShapeDtypeStruct((1, N_EXPERTS), I32),
                   jax.ShapeDtypeStruct((n // tm, 1, N_EXPERTS), I32)),
        grid=(n // tm,),
        in_specs=[row(d), row(WIDTH), row(WIDTH),
                  _const_spec((1, d)), _const_spec((d, 2 * d)), _const_spec((1, 2 * d)),
                  _const_spec((WIDTH, d)), _const_spec((WIDTH, d)), _const_spec((d, d)),
                  _const_spec((1, d)), _const_spec((d, N_EXPERTS)), _const_spec((d, N_EXPERTS)),
                  _const_spec((1, N_EXPERTS))],
        out_specs=(row(d), row(d), row(TOP_K), row(TOP_K),
                   pl.BlockSpec((1, N_EXPERTS), lambda i: (0, 0)),
                   pl.BlockSpec((1, 1, N_EXPERTS), lambda i: (i, 0, 0))),
        scratch_shapes=[pltpu.VMEM((1, N_EXPERTS), F32)],
        compiler_params=_params("arbitrary"),
        name="merge",
    )(x2, o_a, o_b, g1, w_gate, b_gate, w_pa, w_pb, w_out, g2, w_r_hi, w_r_lo, b_r)


def _segment_copies(len_ref, src_ref, dst_ref, tile, make_copy, wait):
    for e in range(N_EXPERTS):
        base = tile * N_EXPERTS + e
        length, src, dst = len_ref[base], src_ref[base], dst_ref[base]
        size = MOE_TILE
        while size >= SEG_ALIGN:
            done = (length // (2 * size)) * (2 * size)

            @pl.when((length & size) != 0)
            def _(size=size, done=done):
                cp = make_copy(pl.multiple_of(src + done, SEG_ALIGN),
                               pl.multiple_of(dst + done, SEG_ALIGN), size)
                if wait:
                    cp.wait()
                else:
                    cp.start()

            size //= 2


def _dispatch_body(len_ref, src_ref, dst_ref, pad_ref, nb_ref, slot_ref, h_ref,
                   xs_hbm, stage_ref, zero_ref, sem, zero_sem):
    tm = h_ref.shape[0]
    n_blocks = xs_hbm.shape[0] // EXPERT_ROWS
    tile = pl.program_id(0)
    buf = tile % 2

    @pl.when(tile == 0)
    def _():
        zero_ref[...] = jnp.zeros_like(zero_ref)
        for e in range(N_EXPERTS):
            r0 = pl.multiple_of(pad_ref[e], EXPERT_ROWS)
            pltpu.make_async_copy(zero_ref, xs_hbm.at[pl.ds(r0, EXPERT_ROWS)], zero_sem).start()
        for e in range(N_EXPERTS):
            pltpu.make_async_copy(zero_ref, xs_hbm.at[pl.ds(0, EXPERT_ROWS)], zero_sem).wait()

        def fill(b, carry):
            r0 = pl.multiple_of(b * EXPERT_ROWS, EXPERT_ROWS)
            cp = pltpu.make_async_copy(zero_ref, xs_hbm.at[pl.ds(r0, EXPERT_ROWS)], zero_sem)
            cp.start()
            cp.wait()
            return carry

        lax.fori_loop(nb_ref[0], n_blocks, fill, 0)

    slots = slot_ref[...]
    col = lax.broadcasted_iota(I32, (tm, STAGE_ROWS), 1)
    hit = col == slots[:, 0:1]
    for k in range(1, TOP_K):
        hit = hit | (col == slots[:, k:k + 1])
    pick = jnp.where(hit, 1.0, 0.0).astype(BF16)
    stage_ref[buf] = lax.dot_general(pick, h_ref[...].astype(BF16), (((0,), (0,)), ((), ())),
                                     preferred_element_type=F32)

    def copy_from(b):
        return lambda src, dst, size: pltpu.make_async_copy(
            stage_ref.at[b, pl.ds(src, size)], xs_hbm.at[pl.ds(dst, size)], sem)

    @pl.when(tile > 0)
    def _():
        _segment_copies(len_ref, src_ref, dst_ref, tile - 1, copy_from(1 - buf), wait=True)

    _segment_copies(len_ref, src_ref, dst_ref, tile, copy_from(buf), wait=False)

    @pl.when(tile == pl.num_programs(0) - 1)
    def _():
        _segment_copies(len_ref, src_ref, dst_ref, tile, copy_from(buf), wait=True)


def _dispatch(seg_len, seg_src, seg_dst, pad_start, n_used, slots, h2, n_rows):
    n, d = h2.shape
    tm = MOE_TILE
    tiled = lambda w: pl.BlockSpec((tm, w), lambda i, *_: (i, 0))
    return pl.pallas_call(
        _dispatch_body,
        out_shape=jax.ShapeDtypeStruct((n_rows, d), F32),
        grid_spec=pltpu.PrefetchScalarGridSpec(
            num_scalar_prefetch=5,
            grid=(n // tm,),
            in_specs=[tiled(TOP_K), tiled(d)],
            out_specs=pl.BlockSpec(memory_space=pl.ANY),
            scratch_shapes=[pltpu.VMEM((2, STAGE_ROWS, d), F32),
                            pltpu.VMEM((EXPERT_ROWS, d), F32),
                            pltpu.SemaphoreType.DMA,
                            pltpu.SemaphoreType.DMA]),
        compiler_params=_params("arbitrary"),
        name="dispatch",
    )(seg_len, seg_src, seg_dst, pad_start, n_used, slots, h2)


def _expert_body(be_ref, nb_ref, x_ref, w1_ref, b1_ref, w2_ref, b2_ref, y_ref, w1b_ref, w2b_ref):
    f = w2_ref.shape[1]
    b = pl.program_id(0)

    @pl.when((b == 0) | (be_ref[b] != be_ref[jnp.maximum(b - 1, 0)]))
    def _():
        w1b_ref[...] = w1_ref[0].astype(BF16)
        w2b_ref[...] = w2_ref[0].astype(BF16)

    @pl.when(b < nb_ref[0])
    def _():
        xb = x_ref[...].astype(BF16)
        hid = jnp.dot(xb, w1b_ref[...], preferred_element_type=F32) + b1_ref[0]
        glu = jnp.minimum(hid[:, :f], SWIGLU_LIMIT)
        lin = jnp.clip(hid[:, f:], -SWIGLU_LIMIT, SWIGLU_LIMIT)
        act = glu * jax.nn.sigmoid(SWIGLU_ALPHA * glu) * (lin + 1.0)
        y_ref[...] = jnp.dot(act.astype(BF16), w2b_ref[...], preferred_element_type=F32) + b2_ref[0]

    @pl.when(b >= nb_ref[0])
    def _():
        y_ref[...] = jnp.zeros_like(y_ref)


def _experts(block_e, n_used, xs, w1, b1, w2, b2, n_blocks):
    d = xs.shape[1]
    f = w2.shape[1]
    br = EXPERT_ROWS
    blk = lambda b, be, nb: (jnp.minimum(b, nb[0] - 1), 0)
    per_e = lambda b, be, nb: (be[b], 0, 0)
    return pl.pallas_call(
        _expert_body,
        out_shape=jax.ShapeDtypeStruct((n_blocks * br, d), F32),
        grid_spec=pltpu.PrefetchScalarGridSpec(
            num_scalar_prefetch=2,
            grid=(n_blocks,),
            in_specs=[pl.BlockSpec((br, d), blk),
                      pl.BlockSpec((1, d, 2 * f), per_e),
                      pl.BlockSpec((1, 1, 2 * f), per_e),
                      pl.BlockSpec((1, f, d), per_e),
                      pl.BlockSpec((1, 1, d), per_e)],
            out_specs=pl.BlockSpec((br, d), lambda b, be, nb: (b, 0)),
            scratch_shapes=[pltpu.VMEM((d, 2 * f), BF16), pltpu.VMEM((f, d), BF16)]),
        compiler_params=_params("arbitrary"),
        name="experts",
    )(block_e, n_used, xs, w1, b1, w2, b2)


def _combine_body(len_ref, src_ref, dst_ref, slot_ref, x1_ref, gate_ref, g_ref,
                  y_hbm, o_ref, stage_ref, sem, *, final_norm):
    tm = x1_ref.shape[0]
    tile = pl.program_id(0)
    buf = tile % 2

    def copy_into(b):
        return lambda src, dst, size: pltpu.make_async_copy(
            y_hbm.at[pl.ds(dst, size)], stage_ref.at[b, pl.ds(src, size)], sem)

    @pl.when(tile == 0)
    def _():
        stage_ref[...] = jnp.zeros_like(stage_ref)
        _segment_copies(len_ref, src_ref, dst_ref, tile, copy_into(0), wait=False)

    _segment_copies(len_ref, src_ref, dst_ref, tile, copy_into(buf), wait=True)

    @pl.when(tile + 1 < pl.num_programs(0))
    def _():
        _segment_copies(len_ref, src_ref, dst_ref, tile + 1, copy_into(1 - buf), wait=False)

    slots = slot_ref[...]
    gate = gate_ref[...]
    col = lax.broadcasted_iota(I32, (tm, STAGE_ROWS), 1)
    weight = jnp.zeros((tm, STAGE_ROWS), F32)
    for k in range(TOP_K):
        weight = jnp.where(col == slots[:, k:k + 1], gate[:, k:k + 1], weight)
    moe = jnp.dot(weight.astype(BF16), stage_ref[buf].astype(BF16), preferred_element_type=F32)
    acc = x1_ref[...] + moe
    o_ref[...] = _rms(acc, g_ref[...]) if final_norm else acc


def _combine(seg_len, seg_src, seg_dst, slots, x1, gates, g_f, y, final_norm):
    n, d = x1.shape
    tm = MOE_TILE
    tiled = lambda w: pl.BlockSpec((tm, w), lambda i, *_: (i, 0))
    return pl.pallas_call(
        functools.partial(_combine_body, final_norm=final_norm),
        out_shape=jax.ShapeDtypeStruct((n, d), F32),
        grid_spec=pltpu.PrefetchScalarGridSpec(
            num_scalar_prefetch=3,
            grid=(n // tm,),
            in_specs=[tiled(TOP_K), tiled(d), tiled(TOP_K),
                      pl.BlockSpec((1, d), lambda i, *_: (0, 0)),
                      pl.BlockSpec(memory_space=pl.ANY)],
            out_specs=tiled(d),
            scratch_shapes=[pltpu.VMEM((2, STAGE_ROWS, d), F32),
                            pltpu.SemaphoreType.DMA]),
        compiler_params=_params("arbitrary"),
        name="combine",
    )(seg_len, seg_src, seg_dst, slots, x1, gates, g_f, y)


def _layer(x2, bsz, seq, p):
    n, d = x2.shape
    qkv_a, qkv_b = _proj(x2, p["g1"], p["w_in"], seq)
    o_a = _dilated(qkv_a, bsz, seq)
    o_b = _stick(qkv_b, bsz, seq)
    x1, h2, slots, gates, counts, before = _merge(
        x2, o_a, o_b, p["g1"], p["w_gate"], p["b_gate"], p["w_pa"], p["w_pb"], p["w_out"],
        p["g2"], p["w_r"], p["b_r"])

    br = EXPERT_ROWS
    n_tiles = n // MOE_TILE
    before = before.reshape(n_tiles, N_EXPERTS)
    seg_cnt = jnp.concatenate([before[1:], counts], axis=0) - before
    seg_len = (seg_cnt + SEG_ALIGN - 1) // SEG_ALIGN * SEG_ALIGN
    seg_src = jnp.cumsum(seg_len, axis=1) - seg_len
    rows_e = jnp.sum(seg_len, axis=0)
    padded = (rows_e + br - 1) // br * br
    ends = jnp.cumsum(padded)
    starts = ends - padded
    seg_dst = starts[None, :] + jnp.cumsum(seg_len, axis=0) - seg_len
    n_blocks = (n * TOP_K + n_tiles * N_EXPERTS * (SEG_ALIGN - 1)) // br + N_EXPERTS
    block_row = jnp.arange(n_blocks, dtype=I32) * br
    block_e = jnp.minimum(jnp.sum(ends[None, :] <= block_row[:, None], axis=1),
                          N_EXPERTS - 1).astype(I32)
    n_used = (ends[-1:] // br).astype(I32)
    pad_start = (starts + rows_e // br * br).astype(I32)
    tables = [t.astype(I32).reshape(n_tiles * N_EXPERTS) for t in (seg_len, seg_src, seg_dst)]

    xs = _dispatch(*tables, pad_start, n_used, slots, h2, n_blocks * br)
    y = _experts(block_e, n_used, xs, p["w1"], p["b1"], p["w2"], p["b2"], n_blocks)
    return (*tables, slots), x1, gates, y


def kernel(x, norm1_g, w_in, w_proj_a, w_proj_b, w_gate, b_gate, w_out, norm2_g,
           w_router, b_router, w1, b1, w2, b2, norm_f_g):
    bsz, seq, d = x.shape
    depth = w_in.shape[0]
    x2 = x.reshape(bsz * seq, d)
    for l in range(depth):
        p = dict(
            g1=norm1_g[l].reshape(1, d), w_in=w_in[l].astype(BF16),
            w_gate=w_gate[l].astype(BF16), b_gate=b_gate[l].reshape(1, 2 * d),
            w_pa=w_proj_a[l].astype(BF16), w_pb=w_proj_b[l].astype(BF16),
            w_out=w_out[l].astype(BF16), g2=norm2_g[l].reshape(1, d),
            w_r=w_router[l], b_r=b_router[l].reshape(1, N_EXPERTS),
            w1=w1[l], b1=b1[l].reshape(N_EXPERTS, 1, -1),
            w2=w2[l], b2=b2[l].reshape(N_EXPERTS, 1, -1))
        route, x1, gates, y = _layer(x2, bsz, seq, p)
        x2 = _combine(*route, x1, gates, norm_f_g.reshape(1, d), y, final_norm=(l == depth - 1))
    return x2.reshape(bsz, seq, d)
```

```python
import functools

import jax
import jax.numpy as jnp
from jax import lax
from jax.experimental import pallas as pl
from jax.experimental.pallas import tpu as pltpu

F32 = jnp.float32
BF16 = jnp.bfloat16
I32 = jnp.int32

HEAD_DIM = 64
WIDTH = 512
QKV = 3 * WIDTH
DILATED_GROUPS = ((128, 1), (512, 4), (2048, 16))
BAND = 128
ROPE_THETA = 500000.0
ROT_DIM = HEAD_DIM // 4
N_EXPERTS = 32
TOP_K = 4
SWIGLU_ALPHA = 1.702
SWIGLU_LIMIT = 7.0
NORM_EPS = 1e-5

LANES = 128
VMEM_LIMIT = 56 * 1024 * 1024

LOG2E = 1.4426950408889634

PROJ_ROWS = 512
MERGE_ROWS = 512
STICK_ROWS = 512
STICK_KEYS = 256
STICK_TRIP = 4
EXPERT_ROWS = 512
MOE_TILE = MERGE_ROWS
SEG_ALIGN = 8
STAGE_ROWS = MOE_TILE * TOP_K + N_EXPERTS * SEG_ALIGN
DIL_UNROLL = 16
MIX_ROWS = 512


def _rms(x, g):
    return x * lax.rsqrt(jnp.mean(x * x, axis=-1, keepdims=True) + NORM_EPS) * g


def _params(*sem):
    return pltpu.CompilerParams(dimension_semantics=sem, vmem_limit_bytes=VMEM_LIMIT)


def _const_spec(shape):
    return pl.BlockSpec(shape, lambda *_: (0,) * len(shape), pipeline_mode=pl.Buffered(1))


def _proj_body(x_ref, g_ref, w_ref, cos_ref, sa_ref, sb_ref, a_ref, b_ref):
    h = _rms(x_ref[...], g_ref[...]).astype(BF16)
    rep = WIDTH // LANES
    cos = jnp.concatenate([cos_ref[...]] * rep, axis=1)
    sa = jnp.concatenate([sa_ref[...]] * rep, axis=1)
    sb = jnp.concatenate([sb_ref[...]] * rep, axis=1)
    half = ROT_DIM // 2
    for c in range(6):
        acc = jnp.dot(h, w_ref[:, c * WIDTH:(c + 1) * WIDTH], preferred_element_type=F32)
        if c in (0, 1):
            acc = (acc * cos + pltpu.roll(acc, WIDTH - half, 1) * sa
                   + pltpu.roll(acc, half, 1) * sb)
        if c == 0:
            acc = acc * (HEAD_DIM ** -0.5)
        if c == 3:
            acc = acc * (HEAD_DIM ** -0.5 * LOG2E)
        dst = a_ref if c < 3 else b_ref
        dst[:, (c % 3) * WIDTH:(c % 3 + 1) * WIDTH] = acc.astype(dst.dtype)


def _rotary_tables(seq):
    half = ROT_DIM // 2
    inv_freq = jnp.float32(ROPE_THETA) ** (-jnp.arange(0, ROT_DIM, 2, dtype=F32) / ROT_DIM)
    ang = jnp.arange(seq, dtype=jnp.int32).astype(F32)[:, None] * inv_freq[None, :]
    cos, sin = jnp.cos(ang), jnp.sin(ang)
    ones = jnp.ones((seq, HEAD_DIM - ROT_DIM), F32)
    zeros_h = jnp.zeros((seq, half), F32)
    zeros_r = jnp.zeros((seq, HEAD_DIM - ROT_DIM), F32)
    cos_t = jnp.concatenate([cos, cos, ones], axis=1)
    sa_t = jnp.concatenate([-sin, zeros_h, zeros_r], axis=1)
    sb_t = jnp.concatenate([zeros_h, sin, zeros_r], axis=1)
    rep = LANES // HEAD_DIM
    return tuple(jnp.tile(t, (1, rep)) for t in (cos_t, sa_t, sb_t))


def _proj(x2, g1, w_in, seq):
    n, d = x2.shape
    tm = PROJ_ROWS
    tabs = _rotary_tables(seq)
    per_seq = seq // tm
    tab_spec = pl.BlockSpec((tm, LANES), lambda i: (i % per_seq, 0))
    return pl.pallas_call(
        _proj_body,
        out_shape=(jax.ShapeDtypeStruct((n, QKV), F32), jax.ShapeDtypeStruct((n, QKV), BF16)),
        grid=(n // tm,),
        in_specs=[pl.BlockSpec((tm, d), lambda i: (i, 0)),
                  _const_spec((1, d)),
                  _const_spec((d, 2 * QKV)),
                  tab_spec, tab_spec, tab_spec],
        out_specs=(pl.BlockSpec((tm, QKV), lambda i: (i, 0)),
                   pl.BlockSpec((tm, QKV), lambda i: (i, 0))),
        compiler_params=_params("parallel"),
        name="proj",
    )(x2, g1, w_in, *tabs)


def _dil_body(q_ref, k_ref, v_ref, o_ref, os_ref, ls_ref, *, seq):
    heads = LANES // HEAD_DIM
    lane = lax.broadcasted_iota(I32, (1, LANES), 1)
    masks = [(lane >= h * HEAD_DIM) & (lane < (h + 1) * HEAD_DIM) for h in range(heads)]
    qi = lax.broadcasted_iota(I32, (BAND, 2 * BAND), 0)
    ki = lax.broadcasted_iota(I32, (BAND, 2 * BAND), 1)
    neg = jnp.float32(-1e30)
    band_bias = jnp.where((ki >= qi) & (ki <= qi + BAND), 0.0, neg)
    first_bias = jnp.where(ki >= BAND, 0.0, neg)
    nt = (((1,), (1,)), ((), ()))
    n_blocks = seq // BAND

    def rows(ref, start, dil):
        if dil == 1:
            return ref[0, pl.ds(start, BAND), :]
        return ref[0, pl.ds(start, BAND, stride=dil), :]

    for g, (_, dil) in enumerate(DILATED_GROUPS):
        nb = n_blocks // dil

        def block(n, u, last, g=g, dil=dil, nb=nb):
            r = n // nb
            i = n % nb
            cur = r + i * (BAND * dil)
            q = rows(q_ref, cur, dil).astype(BF16)
            k_cur = rows(k_ref, cur, dil).astype(BF16)
            v_cur = rows(v_ref, cur, dil).astype(BF16)
            if u % nb != 0:
                k_prev, v_prev = last
                bias = band_bias
            elif nb <= DIL_UNROLL:
                k_prev, v_prev = k_cur, v_cur
                bias = band_bias + first_bias
            else:
                prev = r + jnp.maximum(i - 1, 0) * (BAND * dil)
                k_prev = rows(k_ref, prev, dil).astype(BF16)
                v_prev = rows(v_ref, prev, dil).astype(BF16)
                bias = band_bias + jnp.where(i > 0, 0.0, 1.0) * first_bias
            kc = jnp.concatenate([k_prev, k_cur], axis=0)
            vc = jnp.concatenate([v_prev, v_cur], axis=0)
            o_t = jnp.zeros((BAND, LANES), F32)
            l_t = jnp.zeros((BAND, LANES), F32)
            for h in range(heads):
                qh = jnp.where(masks[h], q, jnp.zeros_like(q))
                s = lax.dot_general(qh, kc, nt, preferred_element_type=F32) + bias
                m = jnp.max(s, axis=-1, keepdims=True)
                p = jnp.exp(s - m)
                l = jnp.sum(p, axis=-1, keepdims=True)
                pv = jnp.dot(p.astype(BF16), vc, preferred_element_type=F32)
                o_t = jnp.where(masks[h], pv / l, o_t)
                l_t = jnp.where(masks[h], m + jnp.log(l), l_t)
            if dil == 1:
                os_ref[g, pl.ds(cur, BAND), :] = o_t
                ls_ref[g, pl.ds(cur, BAND), :] = l_t
            else:
                os_ref[g, pl.ds(cur, BAND, stride=dil), :] = o_t
                ls_ref[g, pl.ds(cur, BAND, stride=dil), :] = l_t
            return k_cur, v_cur

        def step(t, carry, block=block):
            last = None
            for u in range(DIL_UNROLL):
                last = block(t * DIL_UNROLL + u, u, last)
            return carry

        lax.fori_loop(0, n_blocks // DIL_UNROLL, step, 0)

    def mix(c, carry):
        r0 = pl.multiple_of(c * MIX_ROWS, MIX_ROWS)
        ls = [ls_ref[g, pl.ds(r0, MIX_ROWS), :] for g in range(len(DILATED_GROUPS))]
        m = functools.reduce(jnp.maximum, ls)
        ws = [jnp.exp(l - m) for l in ls]
        num = sum(os_ref[g, pl.ds(r0, MIX_ROWS), :] * w for g, w in enumerate(ws))
        o_ref[0, pl.ds(r0, MIX_ROWS), :] = (num / sum(ws)).astype(BF16)
        return carry

    lax.fori_loop(0, seq // MIX_ROWS, mix, 0)


def _dilated(qkv_a, bsz, seq):
    assert all(w == BAND * d for w, d in DILATED_GROUPS)
    n_blocks = seq // BAND
    assert seq % BAND == 0 and n_blocks % DIL_UNROLL == 0 and seq % MIX_ROWS == 0
    for _, d in DILATED_GROUPS:
        nb = n_blocks // d
        assert n_blocks % d == 0 and (nb % DIL_UNROLL == 0 or DIL_UNROLL % nb == 0)
    a3 = qkv_a.reshape(bsz, seq, QKV)
    hb = WIDTH // LANES
    n_groups = len(DILATED_GROUPS)
    spec = lambda off: pl.BlockSpec((1, seq, LANES), lambda b, h: (b, 0, off * hb + h))
    return pl.pallas_call(
        functools.partial(_dil_body, seq=seq),
        out_shape=jax.ShapeDtypeStruct((bsz, seq, WIDTH), BF16),
        grid=(bsz, hb),
        in_specs=[spec(0), spec(1), spec(2)],
        out_specs=pl.BlockSpec((1, seq, LANES), lambda b, h: (b, 0, h)),
        scratch_shapes=[pltpu.VMEM((n_groups, seq, LANES), F32),
                        pltpu.VMEM((n_groups, seq, LANES), F32)],
        compiler_params=_params("parallel", "parallel"),
        name="dilated",
    )(a3, a3, a3).reshape(bsz * seq, WIDTH)


def _stick_body(q_ref, k_ref, v_ref, o_ref, acc_ref, carry_ref, z_ref, a_ref, *, tq, tk):
    i = pl.program_id(2)
    heads = LANES // HEAD_DIM
    lane = lax.broadcasted_iota(I32, (1, LANES), 1)
    masks = [(lane >= h * HEAD_DIM) & (lane < (h + 1) * HEAD_DIM) for h in range(heads)]
    lr = lax.broadcasted_iota(I32, (tk, tk), 0)
    lc = lax.broadcasted_iota(I32, (tk, tk), 1)
    later = (lr > lc).astype(BF16)
    nt = (((1,), (1,)), ((), ()))
    q = q_ref[0]
    qh = [jnp.where(mk, q, jnp.zeros_like(q)) for mk in masks]
    acc_ref[...] = jnp.zeros_like(acc_ref)
    carry_ref[...] = jnp.zeros_like(carry_ref)
    nd = tq // tk
    n_before = i * nd

    def first_row(j):
        return pl.multiple_of(jnp.maximum(j, 0) * tk, tk)

    def logits(j, top=0):
        kj = k_ref[0, pl.ds(first_row(j), tk), :]
        return [lax.dot_general(qh[h][top:], kj, nt, preferred_element_type=F32)
                for h in range(heads)]

    def weights(zs, j, diag, top=0):
        rows = tq - top
        if diag:
            ti = lax.broadcasted_iota(I32, (rows, tk), 0)
            si = lax.broadcasted_iota(I32, (rows, tk), 1)
            causal = (first_row(j) + si) < (i * tq + top + ti)
        out = []
        for h, z in enumerate(zs):
            neg_abs = pltpu.bitcast(pltpu.bitcast(z, jnp.uint32) | jnp.uint32(0x80000000), F32)
            sp = jnp.maximum(z, 0.0) + jnp.log2(1.0 + jnp.exp2(neg_abs))
            if diag:
                sp = jnp.where(causal, sp, 0.0)
            c = carry_ref[h, top:, :]
            after = (jnp.dot(sp.astype(BF16), later, preferred_element_type=F32)
                     + jnp.concatenate([c] * (tk // LANES), axis=1))
            a = jnp.exp2(z - sp - after)
            if diag:
                a = jnp.where(causal, a, 0.0)
            carry_ref[h, top:, :] = c + jnp.broadcast_to(jnp.sum(sp, axis=-1, keepdims=True),
                                                         (rows, LANES))
            out.append(a.astype(BF16))
        return out

    def values(a_s, j, top=0):
        vj = v_ref[0, pl.ds(first_row(j), tk), :]
        acc_ref[top:, :] += sum(
            jnp.dot(a_s[h], jnp.where(masks[h], vj, jnp.zeros_like(vj)), preferred_element_type=F32)
            for h in range(heads))

    for dj in reversed(range(nd)):
        j = n_before + dj
        values(weights(logits(j, dj * tk), j, True, dj * tk), j, dj * tk)

    def stage(slot, arrays):
        ref = z_ref if arrays[0].dtype == F32 else a_ref
        for h, arr in enumerate(arrays):
            ref[slot, h] = arr

    def staged(ref, slot):
        return [ref[slot, h] for h in range(heads)]

    a_ref[1] = jnp.zeros_like(a_ref[1])

    @pl.when(i > 0)
    def _():
        stage(0, logits(n_before - 1))

    def pipelined(t, p):
        j = n_before - 1 - t
        stage(1 - p, logits(j - 1))
        stage(p, weights(staged(z_ref, p), j, False))
        values(staged(a_ref, 1 - p), j + 1)

    def step(u, carry):
        for k in range(STICK_TRIP):
            pipelined(STICK_TRIP * u + k, k % 2)
        return carry

    lax.fori_loop(0, n_before // STICK_TRIP, step, 0)

    @pl.when(n_before % STICK_TRIP != 0)
    def _():
        for k in range(2):
            pipelined(n_before // STICK_TRIP * STICK_TRIP + k, k)

    @pl.when(i > 0)
    def _():
        values(staged(a_ref, 1), 0)

    o_ref[0] = acc_ref[...].astype(BF16)


def _stick(qkv_b, bsz, seq):
    tq, tk = STICK_ROWS, STICK_KEYS
    assert tq == 2 * tk and seq % tq == 0
    b3 = qkv_b.reshape(bsz, seq, QKV)
    hb = WIDTH // LANES
    heads = LANES // HEAD_DIM
    return pl.pallas_call(
        functools.partial(_stick_body, tq=tq, tk=tk),
        out_shape=jax.ShapeDtypeStruct((bsz, seq, WIDTH), BF16),
        grid=(bsz, hb, seq // tq),
        in_specs=[pl.BlockSpec((1, tq, LANES), lambda b, h, i: (b, i, h)),
                  pl.BlockSpec((1, seq, LANES), lambda b, h, i: (b, 0, hb + h)),
                  pl.BlockSpec((1, seq, LANES), lambda b, h, i: (b, 0, 2 * hb + h))],
        out_specs=pl.BlockSpec((1, tq, LANES), lambda b, h, i: (b, i, h)),
        scratch_shapes=[pltpu.VMEM((tq, LANES), F32),
                        pltpu.VMEM((heads, tq, LANES), F32),
                        pltpu.VMEM((2, heads, tq, tk), F32),
                        pltpu.VMEM((2, heads, tq, tk), BF16)],
        compiler_params=_params("parallel", "parallel", "parallel"),
        name="stick",
    )(b3, b3, b3).reshape(bsz * seq, WIDTH)


def _merge_body(x_ref, oa_ref, ob_ref,
                g1_ref, wg_ref, bg_ref, pa_ref, pb_ref, wo_ref, g2_ref, wrh_ref, wrl_ref, br_ref,
                x1_ref, h2_ref, slot_ref, gate_ref, cnt_ref, before_ref, carry_ref):
    tm, d = x_ref.shape

    @pl.when(pl.program_id(0) == 0)
    def _():
        carry_ref[...] = jnp.zeros_like(carry_ref)

    def dot(a, w_ref, lo, hi):
        return jnp.dot(a, w_ref[:, lo:hi], preferred_element_type=F32)

    def mix_and_route(rows):
        x = x_ref[rows, :]
        h = _rms(x, g1_ref[...]).astype(BF16)
        gate_a = jax.nn.sigmoid(dot(h, wg_ref, 0, d) + bg_ref[:, :d])
        gate_b = jax.nn.sigmoid(dot(h, wg_ref, d, 2 * d) + bg_ref[:, d:])
        merged = (gate_a * dot(oa_ref[rows, :], pa_ref, 0, d)
                  + gate_b * dot(ob_ref[rows, :], pb_ref, 0, d)).astype(BF16)
        x1 = x + jnp.concatenate([dot(merged, wo_ref, 0, d // 2), dot(merged, wo_ref, d // 2, d)],
                                 axis=1)
        x1_ref[rows, :] = x1
        h2 = _rms(x1, g2_ref[...])
        h2_ref[rows, :] = h2

        h2_hi = h2.astype(BF16)
        h2_lo = (h2 - h2_hi.astype(F32)).astype(BF16)
        work = (dot(h2_hi, wrh_ref, 0, N_EXPERTS) + dot(h2_lo, wrh_ref, 0, N_EXPERTS)
                + dot(h2_hi, wrl_ref, 0, N_EXPERTS) + br_ref[...])
        eidx = lax.broadcasted_iota(I32, work.shape, 1)
        tops, hots = [], []
        for _ in range(TOP_K):
            mk = jnp.max(work, axis=-1, keepdims=True)
            ik = jnp.min(jnp.where(work == mk, eidx, N_EXPERTS), axis=-1, keepdims=True)
            hot = eidx == ik
            work = jnp.where(hot, -jnp.inf, work)
            tops.append(mk)
            hots.append(hot)
        exps = [jnp.exp(t - tops[0]) for t in tops]
        den = exps[0] + exps[1] + exps[2] + exps[3]
        gate_ref[rows, :] = jnp.concatenate([e / den for e in exps], axis=1)
        return [hot.astype(F32) for hot in hots]

    hots = mix_and_route(slice(0, tm))

    cnt = sum(hots)
    ri = lax.broadcasted_iota(I32, (tm, tm), 0)
    ci = lax.broadcasted_iota(I32, (tm, tm), 1)
    earlier = (ci < ri).astype(BF16)
    rank = jnp.dot(earlier, cnt.astype(BF16), preferred_element_type=F32)
    tile_cnt = jnp.sum(cnt, axis=0, keepdims=True)
    seg_len = jnp.ceil(tile_cnt * (1.0 / SEG_ALIGN)) * SEG_ALIGN
    ei = lax.broadcasted_iota(I32, (N_EXPERTS, N_EXPERTS), 0)
    ej = lax.broadcasted_iota(I32, (N_EXPERTS, N_EXPERTS), 1)
    seg_start = jnp.dot(jnp.broadcast_to(seg_len, (8, N_EXPERTS)).astype(BF16),
                        (ei < ej).astype(BF16), preferred_element_type=F32)[0:1]
    place = rank + seg_start
    slots = [jnp.sum(hot * place, axis=-1, keepdims=True) for hot in hots]
    slot_ref[...] = jnp.concatenate(slots, axis=1).astype(I32)
    before_ref[0] = carry_ref[...].astype(I32)
    total = carry_ref[...] + tile_cnt
    carry_ref[...] = total
    cnt_ref[...] = total.astype(I32)


def _merge(x2, o_a, o_b, g1, w_gate, b_gate, w_pa, w_pb, w_out, g2, w_r, b_r):
    n, d = x2.shape
    w_r_hi = w_r.astype(BF16)
    w_r_lo = (w_r - w_r_hi.astype(F32)).astype(BF16)
    tm = MERGE_ROWS
    row = lambda w: pl.BlockSpec((tm, w), lambda i: (i, 0))
    return pl.pallas_call(
        _merge_body,
        out_shape=(jax.ShapeDtypeStruct((n, d), F32),
                   jax.ShapeDtypeStruct((n, d), F32),
                   jax.ShapeDtypeStruct((n, TOP_K), I32),
                   jax.ShapeDtypeStruct((n, TOP_K), F32),
                   jax.ShapeDtypeStruct((1, N_EXPERTS), I32),
                   jax.ShapeDtypeStruct((n // tm, 1, N_EXPERTS), I32)),
        grid=(n // tm,),
        in_specs=[row(d), row(WIDTH), row(WIDTH),
                  _const_spec((1, d)), _const_spec((d, 2 * d)), _const_spec((1, 2 * d)),
                  _const_spec((WIDTH, d)), _const_spec((WIDTH, d)), _const_spec((d, d)),
                  _const_spec((1, d)), _const_spec((d, N_EXPERTS)), _const_spec((d, N_EXPERTS)),
                  _const_spec((1, N_EXPERTS))],
        out_specs=(row(d), row(d), row(TOP_K), row(TOP_K),
                   pl.BlockSpec((1, N_EXPERTS), lambda i: (0, 0)),
                   pl.BlockSpec((1, 1, N_EXPERTS), lambda i: (i, 0, 0))),
        scratch_shapes=[pltpu.VMEM((1, N_EXPERTS), F32)],
        compiler_params=_params("arbitrary"),
        name="merge",
    )(x2, o_a, o_b, g1, w_gate, b_gate, w_pa, w_pb, w_out, g2, w_r_hi, w_r_lo, b_r)


def _segment_copies(len_ref, src_ref, dst_ref, tile, make_copy, wait):
    for e in range(N_EXPERTS):
        base = tile * N_EXPERTS + e
        length, src, dst = len_ref[base], src_ref[base], dst_ref[base]
        size = MOE_TILE
        while size >= SEG_ALIGN:
            done = (length // (2 * size)) * (2 * size)

            @pl.when((length & size) != 0)
            def _(size=size, done=done):
                cp = make_copy(pl.multiple_of(src + done, SEG_ALIGN),
                               pl.multiple_of(dst + done, SEG_ALIGN), size)
                if wait:
                    cp.wait()
                else:
                    cp.start()

            size //= 2


def _dispatch_body(len_ref, src_ref, dst_ref, pad_ref, nb_ref, slot_ref, h_ref,
                   xs_hbm, stage_ref, zero_ref, sem, zero_sem):
    tm = h_ref.shape[0]
    n_blocks = xs_hbm.shape[0] // EXPERT_ROWS
    tile = pl.program_id(0)
    buf = tile % 2

    @pl.when(tile == 0)
    def _():
        zero_ref[...] = jnp.zeros_like(zero_ref)
        for e in range(N_EXPERTS):
            r0 = pl.multiple_of(pad_ref[e], EXPERT_ROWS)
            pltpu.make_async_copy(zero_ref, xs_hbm.at[pl.ds(r0, EXPERT_ROWS)], zero_sem).start()
        for e in range(N_EXPERTS):
            pltpu.make_async_copy(zero_ref, xs_hbm.at[pl.ds(0, EXPERT_ROWS)], zero_sem).wait()

        def fill(b, carry):
            r0 = pl.multiple_of(b * EXPERT_ROWS, EXPERT_ROWS)
            cp = pltpu.make_async_copy(zero_ref, xs_hbm.at[pl.ds(r0, EXPERT_ROWS)], zero_sem)
            cp.start()
            cp.wait()
            return carry

        lax.fori_loop(nb_ref[0], n_blocks, fill, 0)

    slots = slot_ref[...]
    col = lax.broadcasted_iota(I32, (tm, STAGE_ROWS), 1)
    hit = col == slots[:, 0:1]
    for k in range(1, TOP_K):
        hit = hit | (col == slots[:, k:k + 1])
    pick = jnp.where(hit, 1.0, 0.0).astype(BF16)
    stage_ref[buf] = lax.dot_general(pick, h_ref[...].astype(BF16), (((0,), (0,)), ((), ())),
                                     preferred_element_type=F32)

    def copy_from(b):
        return lambda src, dst, size: pltpu.make_async_copy(
            stage_ref.at[b, pl.ds(src, size)], xs_hbm.at[pl.ds(dst, size)], sem)

    @pl.when(tile > 0)
    def _():
        _segment_copies(len_ref, src_ref, dst_ref, tile - 1, copy_from(1 - buf), wait=True)

    _segment_copies(len_ref, src_ref, dst_ref, tile, copy_from(buf), wait=False)

    @pl.when(tile == pl.num_programs(0) - 1)
    def _():
        _segment_copies(len_ref, src_ref, dst_ref, tile, copy_from(buf), wait=True)


def _dispatch(seg_len, seg_src, seg_dst, pad_start, n_used, slots, h2, n_rows):
    n, d = h2.shape
    tm = MOE_TILE
    tiled = lambda w: pl.BlockSpec((tm, w), lambda i, *_: (i, 0))
    return pl.pallas_call(
        _dispatch_body,
        out_shape=jax.ShapeDtypeStruct((n_rows, d), F32),
        grid_spec=pltpu.PrefetchScalarGridSpec(
            num_scalar_prefetch=5,
            grid=(n // tm,),
            in_specs=[tiled(TOP_K), tiled(d)],
            out_specs=pl.BlockSpec(memory_space=pl.ANY),
            scratch_shapes=[pltpu.VMEM((2, STAGE_ROWS, d), F32),
                            pltpu.VMEM((EXPERT_ROWS, d), F32),
                            pltpu.SemaphoreType.DMA,
                            pltpu.SemaphoreType.DMA]),
        compiler_params=_params("arbitrary"),
        name="dispatch",
    )(seg_len, seg_src, seg_dst, pad_start, n_used, slots, h2)


def _expert_body(be_ref, nb_ref, x_ref, w1_ref, b1_ref, w2_ref, b2_ref, y_ref, w1b_ref, w2b_ref):
    f = w2_ref.shape[1]
    b = pl.program_id(0)

    @pl.when((b == 0) | (be_ref[b] != be_ref[jnp.maximum(b - 1, 0)]))
    def _():
        w1b_ref[...] = w1_ref[0].astype(BF16)
        w2b_ref[...] = w2_ref[0].astype(BF16)

    @pl.when(b < nb_ref[0])
    def _():
        xb = x_ref[...].astype(BF16)
        hid = jnp.dot(xb, w1b_ref[...], preferred_element_type=F32) + b1_ref[0]
        glu = jnp.minimum(hid[:, :f], SWIGLU_LIMIT)
        lin = jnp.clip(hid[:, f:], -SWIGLU_LIMIT, SWIGLU_LIMIT)
        act = glu * jax.nn.sigmoid(SWIGLU_ALPHA * glu) * (lin + 1.0)
        y_ref[...] = jnp.dot(act.astype(BF16), w2b_ref[...], preferred_element_type=F32) + b2_ref[0]

    @pl.when(b >= nb_ref[0])
    def _():
        y_ref[...] = jnp.zeros_like(y_ref)


def _experts(block_e, n_used, xs, w1, b1, w2, b2, n_blocks):
    d = xs.shape[1]
    f = w2.shape[1]
    br = EXPERT_ROWS
    blk = lambda b, be, nb: (jnp.minimum(b, nb[0] - 1), 0)
    per_e = lambda b, be, nb: (be[b], 0, 0)
    return pl.pallas_call(
        _expert_body,
        out_shape=jax.ShapeDtypeStruct((n_blocks * br, d), F32),
        grid_spec=pltpu.PrefetchScalarGridSpec(
            num_scalar_prefetch=2,
            grid=(n_blocks,),
            in_specs=[pl.BlockSpec((br, d), blk),
                      pl.BlockSpec((1, d, 2 * f), per_e),
                      pl.BlockSpec((1, 1, 2 * f), per_e),
                      pl.BlockSpec((1, f, d), per_e),
                      pl.BlockSpec((1, 1, d), per_e)],
            out_specs=pl.BlockSpec((br, d), lambda b, be, nb: (b, 0)),
            scratch_shapes=[pltpu.VMEM((d, 2 * f), BF16), pltpu.VMEM((f, d), BF16)]),
        compiler_params=_params("arbitrary"),
        name="experts",
    )(block_e, n_used, xs, w1, b1, w2, b2)


def _combine_body(len_ref, src_ref, dst_ref, slot_ref, x1_ref, gate_ref, g_ref,
                  y_hbm, o_ref, stage_ref, sem, *, final_norm):
    tm = x1_ref.shape[0]
    tile = pl.program_id(0)
    buf = tile % 2

    def copy_into(b):
        return lambda src, dst, size: pltpu.make_async_copy(
            y_hbm.at[pl.ds(dst, size)], stage_ref.at[b, pl.ds(src, size)], sem)

    @pl.when(tile == 0)
    def _():
        stage_ref[...] = jnp.zeros_like(stage_ref)
        _segment_copies(len_ref, src_ref, dst_ref, tile, copy_into(0), wait=False)

    _segment_copies(len_ref, src_ref, dst_ref, tile, copy_into(buf), wait=True)

    @pl.when(tile + 1 < pl.num_programs(0))
    def _():
        _segment_copies(len_ref, src_ref, dst_ref, tile + 1, copy_into(1 - buf), wait=False)

    slots = slot_ref[...]
    gate = gate_ref[...]
    col = lax.broadcasted_iota(I32, (tm, STAGE_ROWS), 1)
    weight = jnp.zeros((tm, STAGE_ROWS), F32)
    for k in range(TOP_K):
        weight = jnp.where(col == slots[:, k:k + 1], gate[:, k:k + 1], weight)
    moe = jnp.dot(weight.astype(BF16), stage_ref[buf].astype(BF16), preferred_element_type=F32)
    acc = x1_ref[...] + moe
    o_ref[...] = _rms(acc, g_ref[...]) if final_norm else acc


def _combine(seg_len, seg_src, seg_dst, slots, x1, gates, g_f, y, final_norm):
    n, d = x1.shape
    tm = MOE_TILE
    tiled = lambda w: pl.BlockSpec((tm, w), lambda i, *_: (i, 0))
    return pl.pallas_call(
        functools.partial(_combine_body, final_norm=final_norm),
        out_shape=jax.ShapeDtypeStruct((n, d), F32),
        grid_spec=pltpu.PrefetchScalarGridSpec(
            num_scalar_prefetch=3,
            grid=(n // tm,),
            in_specs=[tiled(TOP_K), tiled(d), tiled(TOP_K),
                      pl.BlockSpec((1, d), lambda i, *_: (0, 0)),
                      pl.BlockSpec(memory_space=pl.ANY)],
            out_specs=tiled(d),
            scratch_shapes=[pltpu.VMEM((2, STAGE_ROWS, d), F32),
                            pltpu.SemaphoreType.DMA]),
        compiler_params=_params("arbitrary"),
        name="combine",
    )(seg_len, seg_src, seg_dst, slots, x1, gates, g_f, y)


def _layer(x2, bsz, seq, p):
    n, d = x2.shape
    qkv_a, qkv_b = _proj(x2, p["g1"], p["w_in"], seq)
    o_a = _dilated(qkv_a, bsz, seq)
    o_b = _stick(qkv_b, bsz, seq)
    x1, h2, slots, gates, counts, before = _merge(
        x2, o_a, o_b, p["g1"], p["w_gate"], p["b_gate"], p["w_pa"], p["w_pb"], p["w_out"],
        p["g2"], p["w_r"], p["b_r"])

    br = EXPERT_ROWS
    n_tiles = n // MOE_TILE
    before = before.reshape(n_tiles, N_EXPERTS)
    seg_cnt = jnp.concatenate([before[1:], counts], axis=0) - before
    seg_len = (seg_cnt + SEG_ALIGN - 1) // SEG_ALIGN * SEG_ALIGN
    seg_src = jnp.cumsum(seg_len, axis=1) - seg_len
    rows_e = jnp.sum(seg_len, axis=0)
    padded = (rows_e + br - 1) // br * br
    ends = jnp.cumsum(padded)
    starts = ends - padded
    seg_dst = starts[None, :] + jnp.cumsum(seg_len, axis=0) - seg_len
    n_blocks = (n * TOP_K + n_tiles * N_EXPERTS * (SEG_ALIGN - 1)) // br + N_EXPERTS
    block_row = jnp.arange(n_blocks, dtype=I32) * br
    block_e = jnp.minimum(jnp.sum(ends[None, :] <= block_row[:, None], axis=1),
                          N_EXPERTS - 1).astype(I32)
    n_used = (ends[-1:] // br).astype(I32)
    pad_start = (starts + rows_e // br * br).astype(I32)
    tables = [t.astype(I32).reshape(n_tiles * N_EXPERTS) for t in (seg_len, seg_src, seg_dst)]

    xs = _dispatch(*tables, pad_start, n_used, slots, h2, n_blocks * br)
    y = _experts(block_e, n_used, xs, p["w1"], p["b1"], p["w2"], p["b2"], n_blocks)
    return (*tables, slots), x1, gates, y


def kernel(x, norm1_g, w_in, w_proj_a, w_proj_b, w_gate, b_gate, w_out, norm2_g,
           w_router, b_router, w1, b1, w2, b2, norm_f_g):
    bsz, seq, d = x.shape
    depth = w_in.shape[0]
    x2 = x.reshape(bsz * seq, d)
    for l in range(depth):
        p = dict(
            g1=norm1_g[l].reshape(1, d), w_in=w_in[l].astype(BF16),
            w_gate=w_gate[l].astype(BF16), b_gate=b_gate[l].reshape(1, 2 * d),
            w_pa=w_proj_a[l].astype(BF16), w_pb=w_proj_b[l].astype(BF16),
            w_out=w_out[l].astype(BF16), g2=norm2_g[l].reshape(1, d),
            w_r=w_router[l], b_r=b_router[l].reshape(1, N_EXPERTS),
            w1=w1[l], b1=b1[l].reshape(N_EXPERTS, 1, -1),
            w2=w2[l], b2=b2[l].reshape(N_EXPERTS, 1, -1))
        route, x1, gates, y = _layer(x2, bsz, seq, p)
        x2 = _combine(*route, x1, gates, norm_f_g.reshape(1, d), y, final_norm=(l == depth - 1))
    return x2.reshape(bsz, seq, d)
```

```python
import functools

import jax
import jax.numpy as jnp
from jax import lax
from jax.experimental import pallas as pl
from jax.experimental.pallas import tpu as pltpu

F32 = jnp.float32
BF16 = jnp.bfloat16
I32 = jnp.int32

HEAD_DIM = 64
WIDTH = 512
QKV = 3 * WIDTH
DILATED_GROUPS = ((128, 1), (512, 4), (2048, 16))
BAND = 128
ROPE_THETA = 500000.0
ROT_DIM = HEAD_DIM // 4
N_EXPERTS = 32
TOP_K = 4
SWIGLU_ALPHA = 1.702
SWIGLU_LIMIT = 7.0
NORM_EPS = 1e-5

LANES = 128
VMEM_LIMIT = 56 * 1024 * 1024

LOG2E = 1.4426950408889634

PROJ_ROWS = 512
MERGE_ROWS = 512
STICK_ROWS = 512
STICK_KEYS = 256
STICK_TRIP = 4
EXPERT_ROWS = 512
MOE_TILE = MERGE_ROWS
SEG_ALIGN = 8
STAGE_ROWS = MOE_TILE * TOP_K + N_EXPERTS * SEG_ALIGN
DIL_UNROLL = 16
MIX_ROWS = 512


def _rms(x, g):
    return x * lax.rsqrt(jnp.mean(x * x, axis=-1, keepdims=True) + NORM_EPS) * g


def _params(*sem):
    return pltpu.CompilerParams(dimension_semantics=sem, vmem_limit_bytes=VMEM_LIMIT)


def _const_spec(shape):
    return pl.BlockSpec(shape, lambda *_: (0,) * len(shape), pipeline_mode=pl.Buffered(1))


def _proj_body(x_ref, g_ref, w_ref, cos_ref, sa_ref, sb_ref, a_ref, b_ref):
    h = _rms(x_ref[...], g_ref[...]).astype(BF16)
    rep = WIDTH // LANES
    cos = jnp.concatenate([cos_ref[...]] * rep, axis=1)
    sa = jnp.concatenate([sa_ref[...]] * rep, axis=1)
    sb = jnp.concatenate([sb_ref[...]] * rep, axis=1)
    half = ROT_DIM // 2
    for c in range(6):
        acc = jnp.dot(h, w_ref[:, c * WIDTH:(c + 1) * WIDTH], preferred_element_type=F32)
        if c in (0, 1):
            acc = (acc * cos + pltpu.roll(acc, WIDTH - half, 1) * sa
                   + pltpu.roll(acc, half, 1) * sb)
        if c == 0:
            acc = acc * (HEAD_DIM ** -0.5)
        if c == 3:
            acc = acc * (HEAD_DIM ** -0.5 * LOG2E)
        dst = a_ref if c < 3 else b_ref
        dst[:, (c % 3) * WIDTH:(c % 3 + 1) * WIDTH] = acc.astype(dst.dtype)


def _rotary_tables(seq):
    half = ROT_DIM // 2
    inv_freq = jnp.float32(ROPE_THETA) ** (-jnp.arange(0, ROT_DIM, 2, dtype=F32) / ROT_DIM)
    ang = jnp.arange(seq, dtype=jnp.int32).astype(F32)[:, None] * inv_freq[None, :]
    cos, sin = jnp.cos(ang), jnp.sin(ang)
    ones = jnp.ones((seq, HEAD_DIM - ROT_DIM), F32)
    zeros_h = jnp.zeros((seq, half), F32)
    zeros_r = jnp.zeros((seq, HEAD_DIM - ROT_DIM), F32)
    cos_t = jnp.concatenate([cos, cos, ones], axis=1)
    sa_t = jnp.concatenate([-sin, zeros_h, zeros_r], axis=1)
    sb_t = jnp.concatenate([zeros_h, sin, zeros_r], axis=1)
    rep = LANES // HEAD_DIM
    return tuple(jnp.tile(t, (1, rep)) for t in (cos_t, sa_t, sb_t))


def _proj(x2, g1, w_in, seq):
    n, d = x2.shape
    tm = PROJ_ROWS
    tabs = _rotary_tables(seq)
    per_seq = seq // tm
    tab_spec = pl.BlockSpec((tm, LANES), lambda i: (i % per_seq, 0))
    return pl.pallas_call(
        _proj_body,
        out_shape=(jax.ShapeDtypeStruct((n, QKV), F32), jax.ShapeDtypeStruct((n, QKV), BF16)),
        grid=(n // tm,),
        in_specs=[pl.BlockSpec((tm, d), lambda i: (i, 0)),
                  _const_spec((1, d)),
                  _const_spec((d, 2 * QKV)),
                  tab_spec, tab_spec, tab_spec],
        out_specs=(pl.BlockSpec((tm, QKV), lambda i: (i, 0)),
                   pl.BlockSpec((tm, QKV), lambda i: (i, 0))),
        compiler_params=_params("parallel"),
        name="proj",
    )(x2, g1, w_in, *tabs)


def _dil_body(q_ref, k_ref, v_ref, o_ref, os_ref, ls_ref, *, seq):
    heads = LANES // HEAD_DIM
    lane = lax.broadcasted_iota(I32, (1, LANES), 1)
    masks = [(lane >= h * HEAD_DIM) & (lane < (h + 1) * HEAD_DIM) for h in range(heads)]
    qi = lax.broadcasted_iota(I32, (BAND, 2 * BAND), 0)
    ki = lax.broadcasted_iota(I32, (BAND, 2 * BAND), 1)
    neg = jnp.float32(-1e30)
    band_bias = jnp.where((ki >= qi) & (ki <= qi + BAND), 0.0, neg)
    first_bias = jnp.where(ki >= BAND, 0.0, neg)
    nt = (((1,), (1,)), ((), ()))
    n_blocks = seq // BAND

    def rows(ref, start, dil):
        if dil == 1:
            return ref[0, pl.ds(start, BAND), :]
        return ref[0, pl.ds(start, BAND, stride=dil), :]

    for g, (_, dil) in enumerate(DILATED_GROUPS):
        nb = n_blocks // dil

        def block(n, u, last, g=g, dil=dil, nb=nb):
            r = n // nb
            i = n % nb
            cur = r + i * (BAND * dil)
            q = rows(q_ref, cur, dil).astype(BF16)
            k_cur = rows(k_ref, cur, dil).astype(BF16)
            v_cur = rows(v_ref, cur, dil).astype(BF16)
            if u % nb != 0:
                k_prev, v_prev = last
                bias = band_bias
            elif nb <= DIL_UNROLL:
                k_prev, v_prev = k_cur, v_cur
                bias = band_bias + first_bias
            else:
                prev = r + jnp.maximum(i - 1, 0) * (BAND * dil)
                k_prev = rows(k_ref, prev, dil).astype(BF16)
                v_prev = rows(v_ref, prev, dil).astype(BF16)
                bias = band_bias + jnp.where(i > 0, 0.0, 1.0) * first_bias
            kc = jnp.concatenate([k_prev, k_cur], axis=0)
            vc = jnp.concatenate([v_prev, v_cur], axis=0)
            o_t = jnp.zeros((BAND, LANES), F32)
            l_t = jnp.zeros((BAND, LANES), F32)
            for h in range(heads):
                qh = jnp.where(masks[h], q, jnp.zeros_like(q))
                s = lax.dot_general(qh, kc, nt, preferred_element_type=F32) + bias
                m = jnp.max(s, axis=-1, keepdims=True)
                p = jnp.exp(s - m)
                l = jnp.sum(p, axis=-1, keepdims=True)
                pv = jnp.dot(p.astype(BF16), vc, preferred_element_type=F32)
                o_t = jnp.where(masks[h], pv / l, o_t)
                l_t = jnp.where(masks[h], m + jnp.log(l), l_t)
            if dil == 1:
                os_ref[g, pl.ds(cur, BAND), :] = o_t
                ls_ref[g, pl.ds(cur, BAND), :] = l_t
            else:
                os_ref[g, pl.ds(cur, BAND, stride=dil), :] = o_t
                ls_ref[g, pl.ds(cur, BAND, stride=dil), :] = l_t
            return k_cur, v_cur

        def step(t, carry, block=block):
            last = None
            for u in range(DIL_UNROLL):
                last = block(t * DIL_UNROLL + u, u, last)
            return carry

        lax.fori_loop(0, n_blocks // DIL_UNROLL, step, 0)

    def mix(c, carry):
        r0 = pl.multiple_of(c * MIX_ROWS, MIX_ROWS)
        ls = [ls_ref[g, pl.ds(r0, MIX_ROWS), :] for g in range(len(DILATED_GROUPS))]
        m = functools.reduce(jnp.maximum, ls)
        ws = [jnp.exp(l - m) for l in ls]
        num = sum(os_ref[g, pl.ds(r0, MIX_ROWS), :] * w for g, w in enumerate(ws))
        o_ref[0, pl.ds(r0, MIX_ROWS), :] = (num / sum(ws)).astype(BF16)
        return carry

    lax.fori_loop(0, seq // MIX_ROWS, mix, 0)


def _dilated(qkv_a, bsz, seq):
    assert all(w == BAND * d for w, d in DILATED_GROUPS)
    n_blocks = seq // BAND
    assert seq % BAND == 0 and n_blocks % DIL_UNROLL == 0 and seq % MIX_ROWS == 0
    for _, d in DILATED_GROUPS:
        nb = n_blocks // d
        assert n_blocks % d == 0 and (nb % DIL_UNROLL == 0 or DIL_UNROLL % nb == 0)
    a3 = qkv_a.reshape(bsz, seq, QKV)
    hb = WIDTH // LANES
    n_groups = len(DILATED_GROUPS)
    spec = lambda off: pl.BlockSpec((1, seq, LANES), lambda b, h: (b, 0, off * hb + h))
    return pl.pallas_call(
        functools.partial(_dil_body, seq=seq),
        out_shape=jax.ShapeDtypeStruct((bsz, seq, WIDTH), BF16),
        grid=(bsz, hb),
        in_specs=[spec(0), spec(1), spec(2)],
        out_specs=pl.BlockSpec((1, seq, LANES), lambda b, h: (b, 0, h)),
        scratch_shapes=[pltpu.VMEM((n_groups, seq, LANES), F32),
                        pltpu.VMEM((n_groups, seq, LANES), F32)],
        compiler_params=_params("parallel", "parallel"),
        name="dilated",
    )(a3, a3, a3).reshape(bsz * seq, WIDTH)


def _stick_body(q_ref, k_ref, v_ref, o_ref, acc_ref, carry_ref, z_ref, a_ref, *, tq, tk):
    i = pl.program_id(2)
    heads = LANES // HEAD_DIM
    lane = lax.broadcasted_iota(I32, (1, LANES), 1)
    masks = [(lane >= h * HEAD_DIM) & (lane < (h + 1) * HEAD_DIM) for h in range(heads)]
    lr = lax.broadcasted_iota(I32, (tk, tk), 0)
    lc = lax.broadcasted_iota(I32, (tk, tk), 1)
    later = (lr > lc).astype(BF16)
    nt = (((1,), (1,)), ((), ()))
    q = q_ref[0]
    qh = [jnp.where(mk, q, jnp.zeros_like(q)) for mk in masks]
    acc_ref[...] = jnp.zeros_like(acc_ref)
    carry_ref[...] = jnp.zeros_like(carry_ref)
    nd = tq // tk
    n_before = i * nd

    def first_row(j):
        return pl.multiple_of(jnp.maximum(j, 0) * tk, tk)

    def logits(j, top=0):
        kj = k_ref[0, pl.ds(first_row(j), tk), :]
        return [lax.dot_general(qh[h][top:], kj, nt, preferred_element_type=F32)
                for h in range(heads)]

    def weights(zs, j, diag, top=0):
        rows = tq - top
        if diag:
            ti = lax.broadcasted_iota(I32, (rows, tk), 0)
            si = lax.broadcasted_iota(I32, (rows, tk), 1)
            causal = (first_row(j) + si) < (i * tq + top + ti)
        out = []
        for h, z in enumerate(zs):
            neg_abs = pltpu.bitcast(pltpu.bitcast(z, jnp.uint32) | jnp.uint32(0x80000000), F32)
            sp = jnp.maximum(z, 0.0) + jnp.log2(1.0 + jnp.exp2(neg_abs))
            if diag:
                sp = jnp.where(causal, sp, 0.0)
            c = carry_ref[h, top:, :]
            after = (jnp.dot(sp.astype(BF16), later, preferred_element_type=F32)
                     + jnp.concatenate([c] * (tk // LANES), axis=1))
            a = jnp.exp2(z - sp - after)
            if diag:
                a = jnp.where(causal, a, 0.0)
            carry_ref[h, top:, :] = c + jnp.broadcast_to(jnp.sum(sp, axis=-1, keepdims=True),
                                                         (rows, LANES))
            out.append(a.astype(BF16))
        return out

    def values(a_s, j, top=0):
        vj = v_ref[0, pl.ds(first_row(j), tk), :]
        acc_ref[top:, :] += sum(
            jnp.dot(a_s[h], jnp.where(masks[h], vj, jnp.zeros_like(vj)), preferred_element_type=F32)
            for h in range(heads))

    def stage(slot, arrays, top=0):
        ref = z_ref if arrays[0].dtype == F32 else a_ref
        for h, arr in enumerate(arrays):
            ref[slot, h, top:, :] = arr

    def staged(ref, slot, top=0):
        return [ref[slot, h, top:, :] for h in range(heads)]

    upper, lower, top = n_before + 1, n_before, tk
    stage(0, logits(upper, top), top)
    stage(1, logits(lower))
    stage(0, weights(staged(z_ref, 0, top), upper, True, top), top)
    stage(0, logits(n_before - 1))
    stage(1, weights(staged(z_ref, 1), lower, True))
    values(staged(a_ref, 0, top), upper, top)

    def pipelined(t, p):
        j = n_before - 1 - t
        stage(1 - p, logits(j - 1))
        stage(p, weights(staged(z_ref, p), j, False))
        values(staged(a_ref, 1 - p), j + 1)

    def step(u, carry):
        for k in range(STICK_TRIP):
            pipelined(STICK_TRIP * u + k, k % 2)
        return carry

    lax.fori_loop(0, n_before // STICK_TRIP, step, 0)

    @pl.when(n_before % STICK_TRIP != 0)
    def _():
        for k in range(2):
            pipelined(n_before // STICK_TRIP * STICK_TRIP + k, k)

    values(staged(a_ref, 1), 0)

    o_ref[0] = acc_ref[...].astype(BF16)


def _stick(qkv_b, bsz, seq):
    tq, tk = STICK_ROWS, STICK_KEYS
    assert tq == 2 * tk and seq % tq == 0
    b3 = qkv_b.reshape(bsz, seq, QKV)
    hb = WIDTH // LANES
    heads = LANES // HEAD_DIM
    return pl.pallas_call(
        functools.partial(_stick_body, tq=tq, tk=tk),
        out_shape=jax.ShapeDtypeStruct((bsz, seq, WIDTH), BF16),
        grid=(bsz, hb, seq // tq),
        in_specs=[pl.BlockSpec((1, tq, LANES), lambda b, h, i: (b, i, h)),
                  pl.BlockSpec((1, seq, LANES), lambda b, h, i: (b, 0, hb + h)),
                  pl.BlockSpec((1, seq, LANES), lambda b, h, i: (b, 0, 2 * hb + h))],
        out_specs=pl.BlockSpec((1, tq, LANES), lambda b, h, i: (b, i, h)),
        scratch_shapes=[pltpu.VMEM((tq, LANES), F32),
                        pltpu.VMEM((heads, tq, LANES), F32),
                        pltpu.VMEM((2, heads, tq, tk), F32),
                        pltpu.VMEM((2, heads, tq, tk), BF16)],
        compiler_params=_params("parallel", "parallel", "parallel"),
        name="stick",
    )(b3, b3, b3).reshape(bsz * seq, WIDTH)


def _merge_body(x_ref, oa_ref, ob_ref,
                g1_ref, wg_ref, bg_ref, pa_ref, pb_ref, wo_ref, g2_ref, wrh_ref, wrl_ref, br_ref,
                x1_ref, h2_ref, slot_ref, gate_ref, cnt_ref, before_ref, carry_ref):
    tm, d = x_ref.shape

    @pl.when(pl.program_id(0) == 0)
    def _():
        carry_ref[...] = jnp.zeros_like(carry_ref)

    def dot(a, w_ref, lo, hi):
        return jnp.dot(a, w_ref[:, lo:hi], preferred_element_type=F32)

    def mix_and_route(rows):
        x = x_ref[rows, :]
        h = _rms(x, g1_ref[...]).astype(BF16)
        gate_a = jax.nn.sigmoid(dot(h, wg_ref, 0, d) + bg_ref[:, :d])
        gate_b = jax.nn.sigmoid(dot(h, wg_ref, d, 2 * d) + bg_ref[:, d:])
        merged = (gate_a * dot(oa_ref[rows, :], pa_ref, 0, d)
                  + gate_b * dot(ob_ref[rows, :], pb_ref, 0, d)).astype(BF16)
        x1 = x + jnp.concatenate([dot(merged, wo_ref, 0, d // 2), dot(merged, wo_ref, d // 2, d)],
                                 axis=1)
        x1_ref[rows, :] = x1
        h2 = _rms(x1, g2_ref[...])
        h2_ref[rows, :] = h2

        h2_hi = h2.astype(BF16)
        h2_lo = (h2 - h2_hi.astype(F32)).astype(BF16)
        work = (dot(h2_hi, wrh_ref, 0, N_EXPERTS) + dot(h2_lo, wrh_ref, 0, N_EXPERTS)
                + dot(h2_hi, wrl_ref, 0, N_EXPERTS) + br_ref[...])
        eidx = lax.broadcasted_iota(I32, work.shape, 1)
        tops, hots = [], []
        for _ in range(TOP_K):
            mk = jnp.max(work, axis=-1, keepdims=True)
            ik = jnp.min(jnp.where(work == mk, eidx, N_EXPERTS), axis=-1, keepdims=True)
            hot = eidx == ik
            work = jnp.where(hot, -jnp.inf, work)
            tops.append(mk)
            hots.append(hot)
        exps = [jnp.exp(t - tops[0]) for t in tops]
        den = exps[0] + exps[1] + exps[2] + exps[3]
        gate_ref[rows, :] = jnp.concatenate([e / den for e in exps], axis=1)
        return [hot.astype(F32) for hot in hots]

    hots = mix_and_route(slice(0, tm))

    cnt = sum(hots)
    ri = lax.broadcasted_iota(I32, (tm, tm), 0)
    ci = lax.broadcasted_iota(I32, (tm, tm), 1)
    earlier = (ci < ri).astype(BF16)
    rank = jnp.dot(earlier, cnt.astype(BF16), preferred_element_type=F32)
    tile_cnt = jnp.sum(cnt, axis=0, keepdims=True)
    seg_len = jnp.ceil(tile_cnt * (1.0 / SEG_ALIGN)) * SEG_ALIGN
    ei = lax.broadcasted_iota(I32, (N_EXPERTS, N_EXPERTS), 0)
    ej = lax.broadcasted_iota(I32, (N_EXPERTS, N_EXPERTS), 1)
    seg_start = jnp.dot(jnp.broadcast_to(seg_len, (8, N_EXPERTS)).astype(BF16),
                        (ei < ej).astype(BF16), preferred_element_type=F32)[0:1]
    place = rank + seg_start
    slots = [jnp.sum(hot * place, axis=-1, keepdims=True) for hot in hots]
    slot_ref[...] = jnp.concatenate(slots, axis=1).astype(I32)
    before_ref[0] = carry_ref[...].astype(I32)
    total = carry_ref[...] + tile_cnt
    carry_ref[...] = total
    cnt_ref[...] = total.astype(I32)


def _merge(x2, o_a, o_b, g1, w_gate, b_gate, w_pa, w_pb, w_out, g2, w_r, b_r):
    n, d = x2.shape
    w_r_hi = w_r.astype(BF16)
    w_r_lo = (w_r - w_r_hi.astype(F32)).astype(BF16)
    tm = MERGE_ROWS
    row = lambda w: pl.BlockSpec((tm, w), lambda i: (i, 0))
    return pl.pallas_call(
        _merge_body,
        out_shape=(jax.ShapeDtypeStruct((n, d), F32),
                   jax.ShapeDtypeStruct((n, d), F32),
                   jax.ShapeDtypeStruct((n, TOP_K), I32),
                   jax.ShapeDtypeStruct((n, TOP_K), F32),
                   jax.ShapeDtypeStruct((1, N_EXPERTS), I32),
                   jax.ShapeDtypeStruct((n // tm, 1, N_EXPERTS), I32)),
        grid=(n // tm,),
        in_specs=[row(d), row(WIDTH), row(WIDTH),
                  _const_spec((1, d)), _const_spec((d, 2 * d)), _const_spec((1, 2 * d)),
                  _const_spec((WIDTH, d)), _const_spec((WIDTH, d)), _const_spec((d, d)),
                  _const_spec((1, d)), _const_spec((d, N_EXPERTS)), _const_spec((d, N_EXPERTS)),
                  _const_spec((1, N_EXPERTS))],
        out_specs=(row(d), row(d), row(TOP_K), row(TOP_K),
                   pl.BlockSpec((1, N_EXPERTS), lambda i: (0, 0)),
                   pl.BlockSpec((1, 1, N_EXPERTS), lambda i: (i, 0, 0))),
        scratch_shapes=[pltpu.VMEM((1, N_EXPERTS), F32)],
        compiler_params=_params("arbitrary"),
        name="merge",
    )(x2, o_a, o_b, g1, w_gate, b_gate, w_pa, w_pb, w_out, g2, w_r_hi, w_r_lo, b_r)


def _segment_copies(len_ref, src_ref, dst_ref, tile, make_copy, wait):
    for e in range(N_EXPERTS):
        base = tile * N_EXPERTS + e
        length, src, dst = len_ref[base], src_ref[base], dst_ref[base]
        size = MOE_TILE
        while size >= SEG_ALIGN:
            done = (length // (2 * size)) * (2 * size)

            @pl.when((length & size) != 0)
            def _(size=size, done=done):
                cp = make_copy(pl.multiple_of(src + done, SEG_ALIGN),
                               pl.multiple_of(dst + done, SEG_ALIGN), size)
                if wait:
                    cp.wait()
                else:
                    cp.start()

            size //= 2


def _dispatch_body(len_ref, src_ref, dst_ref, pad_ref, nb_ref, slot_ref, h_ref,
                   xs_hbm, stage_ref, zero_ref, sem, zero_sem):
    tm = h_ref.shape[0]
    n_blocks = xs_hbm.shape[0] // EXPERT_ROWS
    tile = pl.program_id(0)
    buf = tile % 2

    @pl.when(tile == 0)
    def _():
        zero_ref[...] = jnp.zeros_like(zero_ref)
        for e in range(N_EXPERTS):
            r0 = pl.multiple_of(pad_ref[e], EXPERT_ROWS)
            pltpu.make_async_copy(zero_ref, xs_hbm.at[pl.ds(r0, EXPERT_ROWS)], zero_sem).start()
        for e in range(N_EXPERTS):
            pltpu.make_async_copy(zero_ref, xs_hbm.at[pl.ds(0, EXPERT_ROWS)], zero_sem).wait()

        def fill(b, carry):
            r0 = pl.multiple_of(b * EXPERT_ROWS, EXPERT_ROWS)
            cp = pltpu.make_async_copy(zero_ref, xs_hbm.at[pl.ds(r0, EXPERT_ROWS)], zero_sem)
            cp.start()
            cp.wait()
            return carry

        lax.fori_loop(nb_ref[0], n_blocks, fill, 0)

    slots = slot_ref[...]
    col = lax.broadcasted_iota(I32, (tm, STAGE_ROWS), 1)
    hit = col == slots[:, 0:1]
    for k in range(1, TOP_K):
        hit = hit | (col == slots[:, k:k + 1])
    pick = jnp.where(hit, 1.0, 0.0).astype(BF16)
    stage_ref[buf] = lax.dot_general(pick, h_ref[...].astype(BF16), (((0,), (0,)), ((), ())),
                                     preferred_element_type=F32)

    def copy_from(b):
        return lambda src, dst, size: pltpu.make_async_copy(
            stage_ref.at[b, pl.ds(src, size)], xs_hbm.at[pl.ds(dst, size)], sem)

    @pl.when(tile > 0)
    def _():
        _segment_copies(len_ref, src_ref, dst_ref, tile - 1, copy_from(1 - buf), wait=True)

    _segment_copies(len_ref, src_ref, dst_ref, tile, copy_from(buf), wait=False)

    @pl.when(tile == pl.num_programs(0) - 1)
    def _():
        _segment_copies(len_ref, src_ref, dst_ref, tile, copy_from(buf), wait=True)


def _dispatch(seg_len, seg_src, seg_dst, pad_start, n_used, slots, h2, n_rows):
    n, d = h2.shape
    tm = MOE_TILE
    tiled = lambda w: pl.BlockSpec((tm, w), lambda i, *_: (i, 0))
    return pl.pallas_call(
        _dispatch_body,
        out_shape=jax.ShapeDtypeStruct((n_rows, d), F32),
        grid_spec=pltpu.PrefetchScalarGridSpec(
            num_scalar_prefetch=5,
            grid=(n // tm,),
            in_specs=[tiled(TOP_K), tiled(d)],
            out_specs=pl.BlockSpec(memory_space=pl.ANY),
            scratch_shapes=[pltpu.VMEM((2, STAGE_ROWS, d), F32),
                            pltpu.VMEM((EXPERT_ROWS, d), F32),
                            pltpu.SemaphoreType.DMA,
                            pltpu.SemaphoreType.DMA]),
        compiler_params=_params("arbitrary"),
        name="dispatch",
    )(seg_len, seg_src, seg_dst, pad_start, n_used, slots, h2)


def _expert_body(be_ref, nb_ref, x_ref, w1_ref, b1_ref, w2_ref, b2_ref, y_ref, w1b_ref, w2b_ref):
    f = w2_ref.shape[1]
    b = pl.program_id(0)

    @pl.when((b == 0) | (be_ref[b] != be_ref[jnp.maximum(b - 1, 0)]))
    def _():
        w1b_ref[...] = w1_ref[0].astype(BF16)
        w2b_ref[...] = w2_ref[0].astype(BF16)

    @pl.when(b < nb_ref[0])
    def _():
        xb = x_ref[...].astype(BF16)
        hid = jnp.dot(xb, w1b_ref[...], preferred_element_type=F32) + b1_ref[0]
        glu = jnp.minimum(hid[:, :f], SWIGLU_LIMIT)
        lin = jnp.clip(hid[:, f:], -SWIGLU_LIMIT, SWIGLU_LIMIT)
        act = glu * jax.nn.sigmoid(SWIGLU_ALPHA * glu) * (lin + 1.0)
        y_ref[...] = jnp.dot(act.astype(BF16), w2b_ref[...], preferred_element_type=F32) + b2_ref[0]

    @pl.when(b >= nb_ref[0])
    def _():
        y_ref[...] = jnp.zeros_like(y_ref)


def _experts(block_e, n_used, xs, w1, b1, w2, b2, n_blocks):
    d = xs.shape[1]
    f = w2.shape[1]
    br = EXPERT_ROWS
    blk = lambda b, be, nb: (jnp.minimum(b, nb[0] - 1), 0)
    per_e = lambda b, be, nb: (be[b], 0, 0)
    return pl.pallas_call(
        _expert_body,
        out_shape=jax.ShapeDtypeStruct((n_blocks * br, d), F32),
        grid_spec=pltpu.PrefetchScalarGridSpec(
            num_scalar_prefetch=2,
            grid=(n_blocks,),
            in_specs=[pl.BlockSpec((br, d), blk),
                      pl.BlockSpec((1, d, 2 * f), per_e),
                      pl.BlockSpec((1, 1, 2 * f), per_e),
                      pl.BlockSpec((1, f, d), per_e),
                      pl.BlockSpec((1, 1, d), per_e)],
            out_specs=pl.BlockSpec((br, d), lambda b, be, nb: (b, 0)),
            scratch_shapes=[pltpu.VMEM((d, 2 * f), BF16), pltpu.VMEM((f, d), BF16)]),
        compiler_params=_params("arbitrary"),
        name="experts",
    )(block_e, n_used, xs, w1, b1, w2, b2)


def _combine_body(len_ref, src_ref, dst_ref, slot_ref, x1_ref, gate_ref, g_ref,
                  y_hbm, o_ref, stage_ref, sem, *, final_norm):
    tm = x1_ref.shape[0]
    tile = pl.program_id(0)
    buf = tile % 2

    def copy_into(b):
        return lambda src, dst, size: pltpu.make_async_copy(
            y_hbm.at[pl.ds(dst, size)], stage_ref.at[b, pl.ds(src, size)], sem)

    @pl.when(tile == 0)
    def _():
        stage_ref[...] = jnp.zeros_like(stage_ref)
        _segment_copies(len_ref, src_ref, dst_ref, tile, copy_into(0), wait=False)

    _segment_copies(len_ref, src_ref, dst_ref, tile, copy_into(buf), wait=True)

    @pl.when(tile + 1 < pl.num_programs(0))
    def _():
        _segment_copies(len_ref, src_ref, dst_ref, tile + 1, copy_into(1 - buf), wait=False)

    slots = slot_ref[...]
    gate = gate_ref[...]
    col = lax.broadcasted_iota(I32, (tm, STAGE_ROWS), 1)
    weight = jnp.zeros((tm, STAGE_ROWS), F32)
    for k in range(TOP_K):
        weight = jnp.where(col == slots[:, k:k + 1], gate[:, k:k + 1], weight)
    moe = jnp.dot(weight.astype(BF16), stage_ref[buf].astype(BF16), preferred_element_type=F32)
    acc = x1_ref[...] + moe
    o_ref[...] = _rms(acc, g_ref[...]) if final_norm else acc


def _combine(seg_len, seg_src, seg_dst, slots, x1, gates, g_f, y, final_norm):
    n, d = x1.shape
    tm = MOE_TILE
    tiled = lambda w: pl.BlockSpec((tm, w), lambda i, *_: (i, 0))
    return pl.pallas_call(
        functools.partial(_combine_body, final_norm=final_norm),
        out_shape=jax.ShapeDtypeStruct((n, d), F32),
        grid_spec=pltpu.PrefetchScalarGridSpec(
            num_scalar_prefetch=3,
            grid=(n // tm,),
            in_specs=[tiled(TOP_K), tiled(d), tiled(TOP_K),
                      pl.BlockSpec((1, d), lambda i, *_: (0, 0)),
                      pl.BlockSpec(memory_space=pl.ANY)],
            out_specs=tiled(d),
            scratch_shapes=[pltpu.VMEM((2, STAGE_ROWS, d), F32),
                            pltpu.SemaphoreType.DMA]),
        compiler_params=_params("arbitrary"),
        name="combine",
    )(seg_len, seg_src, seg_dst, slots, x1, gates, g_f, y)


def _layer(x2, bsz, seq, p):
    n, d = x2.shape
    qkv_a, qkv_b = _proj(x2, p["g1"], p["w_in"], seq)
    o_a = _dilated(qkv_a, bsz, seq)
    o_b = _stick(qkv_b, bsz, seq)
    x1, h2, slots, gates, counts, before = _merge(
        x2, o_a, o_b, p["g1"], p["w_gate"], p["b_gate"], p["w_pa"], p["w_pb"], p["w_out"],
        p["g2"], p["w_r"], p["b_r"])

    br = EXPERT_ROWS
    n_tiles = n // MOE_TILE
    before = before.reshape(n_tiles, N_EXPERTS)
    seg_cnt = jnp.concatenate([before[1:], counts], axis=0) - before
    seg_len = (seg_cnt + SEG_ALIGN - 1) // SEG_ALIGN * SEG_ALIGN
    seg_src = jnp.cumsum(seg_len, axis=1) - seg_len
    rows_e = jnp.sum(seg_len, axis=0)
    padded = (rows_e + br - 1) // br * br
    ends = jnp.cumsum(padded)
    starts = ends - padded
    seg_dst = starts[None, :] + jnp.cumsum(seg_len, axis=0) - seg_len
    n_blocks = (n * TOP_K + n_tiles * N_EXPERTS * (SEG_ALIGN - 1)) // br + N_EXPERTS
    block_row = jnp.arange(n_blocks, dtype=I32) * br
    block_e = jnp.minimum(jnp.sum(ends[None, :] <= block_row[:, None], axis=1),
                          N_EXPERTS - 1).astype(I32)
    n_used = (ends[-1:] // br).astype(I32)
    pad_start = (starts + rows_e // br * br).astype(I32)
    tables = [t.astype(I32).reshape(n_tiles * N_EXPERTS) for t in (seg_len, seg_src, seg_dst)]

    xs = _dispatch(*tables, pad_start, n_used, slots, h2, n_blocks * br)
    y = _experts(block_e, n_used, xs, p["w1"], p["b1"], p["w2"], p["b2"], n_blocks)
    return (*tables, slots), x1, gates, y


def kernel(x, norm1_g, w_in, w_proj_a, w_proj_b, w_gate, b_gate, w_out, norm2_g,
           w_router, b_router, w1, b1, w2, b2, norm_f_g):
    bsz, seq, d = x.shape
    depth = w_in.shape[0]
    x2 = x.reshape(bsz * seq, d)
    for l in range(depth):
        p = dict(
            g1=norm1_g[l].reshape(1, d), w_in=w_in[l].astype(BF16),
            w_gate=w_gate[l].astype(BF16), b_gate=b_gate[l].reshape(1, 2 * d),
            w_pa=w_proj_a[l].astype(BF16), w_pb=w_proj_b[l].astype(BF16),
            w_out=w_out[l].astype(BF16), g2=norm2_g[l].reshape(1, d),
            w_r=w_router[l], b_r=b_router[l].reshape(1, N_EXPERTS),
            w1=w1[l], b1=b1[l].reshape(N_EXPERTS, 1, -1),
            w2=w2[l], b2=b2[l].reshape(N_EXPERTS, 1, -1))
        route, x1, gates, y = _layer(x2, bsz, seq, p)
        x2 = _combine(*route, x1, gates, norm_f_g.reshape(1, d), y, final_norm=(l == depth - 1))
    return x2.reshape(bsz, seq, d)
```

```python
import functools

import jax
import jax.numpy as jnp
from jax import lax
from jax.experimental import pallas as pl
from jax.experimental.pallas import tpu as pltpu

F32 = jnp.float32
BF16 = jnp.bfloat16
I32 = jnp.int32

HEAD_DIM = 64
WIDTH = 512
QKV = 3 * WIDTH
DILATED_GROUPS = ((128, 1), (512, 4), (2048, 16))
BAND = 128
ROPE_THETA = 500000.0
ROT_DIM = HEAD_DIM // 4
N_EXPERTS = 32
TOP_K = 4
SWIGLU_ALPHA = 1.702
SWIGLU_LIMIT = 7.0
NORM_EPS = 1e-5

LANES = 128
VMEM_LIMIT = 56 * 1024 * 1024

LOG2E = 1.4426950408889634

PROJ_ROWS = 512
MERGE_ROWS = 512
STICK_ROWS = 512
STICK_KEYS = 256
STICK_TRIP = 4
EXPERT_ROWS = 512
MOE_TILE = MERGE_ROWS
SEG_ALIGN = 8
STAGE_ROWS = MOE_TILE * TOP_K + N_EXPERTS * SEG_ALIGN
DIL_UNROLL = 16
MIX_ROWS = 512


def _rms(x, g):
    return x * lax.rsqrt(jnp.mean(x * x, axis=-1, keepdims=True) + NORM_EPS) * g


def _params(*sem):
    return pltpu.CompilerParams(dimension_semantics=sem, vmem_limit_bytes=VMEM_LIMIT)


def _const_spec(shape):
    return pl.BlockSpec(shape, lambda *_: (0,) * len(shape), pipeline_mode=pl.Buffered(1))


def _proj_body(x_ref, g_ref, w_ref, cos_ref, sa_ref, sb_ref, a_ref, b_ref):
    h = _rms(x_ref[...], g_ref[...]).astype(BF16)
    rep = WIDTH // LANES
    cos = jnp.concatenate([cos_ref[...]] * rep, axis=1)
    sa = jnp.concatenate([sa_ref[...]] * rep, axis=1)
    sb = jnp.concatenate([sb_ref[...]] * rep, axis=1)
    half = ROT_DIM // 2
    for c in range(6):
        acc = jnp.dot(h, w_ref[:, c * WIDTH:(c + 1) * WIDTH], preferred_element_type=F32)
        if c in (0, 1):
            acc = (acc * cos + pltpu.roll(acc, WIDTH - half, 1) * sa
                   + pltpu.roll(acc, half, 1) * sb)
        if c == 0:
            acc = acc * (HEAD_DIM ** -0.5)
        if c == 3:
            acc = acc * (HEAD_DIM ** -0.5 * LOG2E)
        dst = a_ref if c < 3 else b_ref
        dst[:, (c % 3) * WIDTH:(c % 3 + 1) * WIDTH] = acc.astype(dst.dtype)


def _rotary_tables(seq):
    half = ROT_DIM // 2
    inv_freq = jnp.float32(ROPE_THETA) ** (-jnp.arange(0, ROT_DIM, 2, dtype=F32) / ROT_DIM)
    ang = jnp.arange(seq, dtype=jnp.int32).astype(F32)[:, None] * inv_freq[None, :]
    cos, sin = jnp.cos(ang), jnp.sin(ang)
    ones = jnp.ones((seq, HEAD_DIM - ROT_DIM), F32)
    zeros_h = jnp.zeros((seq, half), F32)
    zeros_r = jnp.zeros((seq, HEAD_DIM - ROT_DIM), F32)
    cos_t = jnp.concatenate([cos, cos, ones], axis=1)
    sa_t = jnp.concatenate([-sin, zeros_h, zeros_r], axis=1)
    sb_t = jnp.concatenate([zeros_h, sin, zeros_r], axis=1)
    rep = LANES // HEAD_DIM
    return tuple(jnp.tile(t, (1, rep)) for t in (cos_t, sa_t, sb_t))


def _proj(x2, g1, w_in, seq):
    n, d = x2.shape
    tm = PROJ_ROWS
    tabs = _rotary_tables(seq)
    per_seq = seq // tm
    tab_spec = pl.BlockSpec((tm, LANES), lambda i: (i % per_seq, 0))
    return pl.pallas_call(
        _proj_body,
        out_shape=(jax.ShapeDtypeStruct((n, QKV), F32), jax.ShapeDtypeStruct((n, QKV), BF16)),
        grid=(n // tm,),
        in_specs=[pl.BlockSpec((tm, d), lambda i: (i, 0)),
                  _const_spec((1, d)),
                  _const_spec((d, 2 * QKV)),
                  tab_spec, tab_spec, tab_spec],
        out_specs=(pl.BlockSpec((tm, QKV), lambda i: (i, 0)),
                   pl.BlockSpec((tm, QKV), lambda i: (i, 0))),
        compiler_params=_params("parallel"),
        name="proj",
    )(x2, g1, w_in, *tabs)


def _dil_body(q_ref, k_ref, v_ref, o_ref, os_ref, ls_ref, *, seq):
    heads = LANES // HEAD_DIM
    lane = lax.broadcasted_iota(I32, (1, LANES), 1)
    masks = [(lane >= h * HEAD_DIM) & (lane < (h + 1) * HEAD_DIM) for h in range(heads)]
    qi = lax.broadcasted_iota(I32, (BAND, 2 * BAND), 0)
    ki = lax.broadcasted_iota(I32, (BAND, 2 * BAND), 1)
    neg = jnp.float32(-1e30)
    band_bias = jnp.where((ki >= qi) & (ki <= qi + BAND), 0.0, neg)
    first_bias = jnp.where(ki >= BAND, 0.0, neg)
    nt = (((1,), (1,)), ((), ()))
    n_blocks = seq // BAND

    def rows(ref, start, dil):
        if dil == 1:
            return ref[0, pl.ds(start, BAND), :]
        return ref[0, pl.ds(start, BAND, stride=dil), :]

    for g, (_, dil) in enumerate(DILATED_GROUPS):
        nb = n_blocks // dil

        def block(n, u, last, g=g, dil=dil, nb=nb):
            r = n // nb
            i = n % nb
            cur = r + i * (BAND * dil)
            q = rows(q_ref, cur, dil).astype(BF16)
            k_cur = rows(k_ref, cur, dil).astype(BF16)
            v_cur = rows(v_ref, cur, dil).astype(BF16)
            if u % nb != 0:
                k_prev, v_prev = last
                bias = band_bias
            elif nb <= DIL_UNROLL:
                k_prev, v_prev = k_cur, v_cur
                bias = band_bias + first_bias
            else:
                prev = r + jnp.maximum(i - 1, 0) * (BAND * dil)
                k_prev = rows(k_ref, prev, dil).astype(BF16)
                v_prev = rows(v_ref, prev, dil).astype(BF16)
                bias = band_bias + jnp.where(i > 0, 0.0, 1.0) * first_bias
            kc = jnp.concatenate([k_prev, k_cur], axis=0)
            vc = jnp.concatenate([v_prev, v_cur], axis=0)
            o_t = jnp.zeros((BAND, LANES), F32)
            l_t = jnp.zeros((BAND, LANES), F32)
            for h in range(heads):
                qh = jnp.where(masks[h], q, jnp.zeros_like(q))
                s = lax.dot_general(qh, kc, nt, preferred_element_type=F32) + bias
                m = jnp.max(s, axis=-1, keepdims=True)
                p = jnp.exp(s - m)
                l = jnp.sum(p, axis=-1, keepdims=True)
                pv = jnp.dot(p.astype(BF16), vc, preferred_element_type=F32)
                o_t = jnp.where(masks[h], pv / l, o_t)
                l_t = jnp.where(masks[h], m + jnp.log(l), l_t)
            if dil == 1:
                os_ref[g, pl.ds(cur, BAND), :] = o_t
                ls_ref[g, pl.ds(cur, BAND), :] = l_t
            else:
                os_ref[g, pl.ds(cur, BAND, stride=dil), :] = o_t
                ls_ref[g, pl.ds(cur, BAND, stride=dil), :] = l_t
            return k_cur, v_cur

        def step(t, carry, block=block):
            last = None
            for u in range(DIL_UNROLL):
                last = block(t * DIL_UNROLL + u, u, last)
            return carry

        lax.fori_loop(0, n_blocks // DIL_UNROLL, step, 0)

    def mix(c, carry):
        r0 = pl.multiple_of(c * MIX_ROWS, MIX_ROWS)
        ls = [ls_ref[g, pl.ds(r0, MIX_ROWS), :] for g in range(len(DILATED_GROUPS))]
        m = functools.reduce(jnp.maximum, ls)
        ws = [jnp.exp(l - m) for l in ls]
        num = sum(os_ref[g, pl.ds(r0, MIX_ROWS), :] * w for g, w in enumerate(ws))
        o_ref[0, pl.ds(r0, MIX_ROWS), :] = (num / sum(ws)).astype(BF16)
        return carry

    lax.fori_loop(0, seq // MIX_ROWS, mix, 0)


def _dilated(qkv_a, bsz, seq):
    assert all(w == BAND * d for w, d in DILATED_GROUPS)
    n_blocks = seq // BAND
    assert seq % BAND == 0 and n_blocks % DIL_UNROLL == 0 and seq % MIX_ROWS == 0
    for _, d in DILATED_GROUPS:
        nb = n_blocks // d
        assert n_blocks % d == 0 and (nb % DIL_UNROLL == 0 or DIL_UNROLL % nb == 0)
    a3 = qkv_a.reshape(bsz, seq, QKV)
    hb = WIDTH // LANES
    n_groups = len(DILATED_GROUPS)
    spec = lambda off: pl.BlockSpec((1, seq, LANES), lambda b, h: (b, 0, off * hb + h))
    return pl.pallas_call(
        functools.partial(_dil_body, seq=seq),
        out_shape=jax.ShapeDtypeStruct((bsz, seq, WIDTH), BF16),
        grid=(bsz, hb),
        in_specs=[spec(0), spec(1), spec(2)],
        out_specs=pl.BlockSpec((1, seq, LANES), lambda b, h: (b, 0, h)),
        scratch_shapes=[pltpu.VMEM((n_groups, seq, LANES), F32),
                        pltpu.VMEM((n_groups, seq, LANES), F32)],
        compiler_params=_params("parallel", "parallel"),
        name="dilated",
    )(a3, a3, a3).reshape(bsz * seq, WIDTH)


def _stick_body(q_ref, k_ref, v_ref, o_ref, acc_ref, carry_ref, z_ref, a_ref, *, tq, tk):
    i = pl.program_id(2)
    heads = LANES // HEAD_DIM
    lane = lax.broadcasted_iota(I32, (1, LANES), 1)
    masks = [(lane >= h * HEAD_DIM) & (lane < (h + 1) * HEAD_DIM) for h in range(heads)]
    lr = lax.broadcasted_iota(I32, (tk, tk), 0)
    lc = lax.broadcasted_iota(I32, (tk, tk), 1)
    later = (lr > lc).astype(BF16)
    nt = (((1,), (1,)), ((), ()))
    q = q_ref[0]
    qh = [jnp.where(mk, q, jnp.zeros_like(q)) for mk in masks]
    acc_ref[...] = jnp.zeros_like(acc_ref)
    carry_ref[...] = jnp.zeros_like(carry_ref)
    nd = tq // tk
    n_before = i * nd

    def first_row(j):
        return pl.multiple_of(jnp.maximum(j, 0) * tk, tk)

    def logits(j, top=0):
        kj = k_ref[0, pl.ds(first_row(j), tk), :]
        return [lax.dot_general(qh[h][top:], kj, nt, preferred_element_type=F32)
                for h in range(heads)]

    def weights(zs, j, diag, top=0):
        rows = tq - top
        if diag:
            ti = lax.broadcasted_iota(I32, (rows, tk), 0)
            si = lax.broadcasted_iota(I32, (rows, tk), 1)
            causal = (first_row(j) + si) < (i * tq + top + ti)
        out = []
        for h, z in enumerate(zs):
            neg_abs = pltpu.bitcast(pltpu.bitcast(z, jnp.uint32) | jnp.uint32(0x80000000), F32)
            sp = jnp.maximum(z, 0.0) + jnp.log2(1.0 + jnp.exp2(neg_abs))
            if diag:
                sp = jnp.where(causal, sp, 0.0)
            c = carry_ref[h, top:, :]
            after = (jnp.dot(sp.astype(BF16), later, preferred_element_type=F32)
                     + jnp.concatenate([c] * (tk // LANES), axis=1))
            a = jnp.exp2(z - sp - after)
            if diag:
                a = jnp.where(causal, a, 0.0)
            carry_ref[h, top:, :] = jnp.broadcast_to((after + sp)[:, 0:1], (rows, LANES))
            out.append(a.astype(BF16))
        return out

    def values(a_s, j, top=0):
        vj = v_ref[0, pl.ds(first_row(j), tk), :]
        acc_ref[top:, :] += sum(
            jnp.dot(a_s[h], jnp.where(masks[h], vj, jnp.zeros_like(vj)), preferred_element_type=F32)
            for h in range(heads))

    def stage(slot, arrays, top=0):
        ref = z_ref if arrays[0].dtype == F32 else a_ref
        for h, arr in enumerate(arrays):
            ref[slot, h, top:, :] = arr

    def staged(ref, slot, top=0):
        return [ref[slot, h, top:, :] for h in range(heads)]

    upper, lower, top = n_before + 1, n_before, tk
    stage(0, logits(upper, top), top)
    stage(1, logits(lower))
    stage(0, weights(staged(z_ref, 0, top), upper, True, top), top)
    stage(0, logits(n_before - 1))
    stage(1, weights(staged(z_ref, 1), lower, True))
    values(staged(a_ref, 0, top), upper, top)

    def pipelined(t, p):
        j = n_before - 1 - t
        stage(1 - p, logits(j - 1))
        stage(p, weights(staged(z_ref, p), j, False))
        values(staged(a_ref, 1 - p), j + 1)

    def step(u, carry):
        for k in range(STICK_TRIP):
            pipelined(STICK_TRIP * u + k, k % 2)
        return carry

    lax.fori_loop(0, n_before // STICK_TRIP, step, 0)

    @pl.when(n_before % STICK_TRIP != 0)
    def _():
        for k in range(2):
            pipelined(n_before // STICK_TRIP * STICK_TRIP + k, k)

    values(staged(a_ref, 1), 0)

    o_ref[0] = acc_ref[...].astype(BF16)


def _stick(qkv_b, bsz, seq):
    tq, tk = STICK_ROWS, STICK_KEYS
    assert tq == 2 * tk and seq % tq == 0
    b3 = qkv_b.reshape(bsz, seq, QKV)
    hb = WIDTH // LANES
    heads = LANES // HEAD_DIM
    return pl.pallas_call(
        functools.partial(_stick_body, tq=tq, tk=tk),
        out_shape=jax.ShapeDtypeStruct((bsz, seq, WIDTH), BF16),
        grid=(bsz, hb, seq // tq),
        in_specs=[pl.BlockSpec((1, tq, LANES), lambda b, h, i: (b, i, h)),
                  pl.BlockSpec((1, seq, LANES), lambda b, h, i: (b, 0, hb + h)),
                  pl.BlockSpec((1, seq, LANES), lambda b, h, i: (b, 0, 2 * hb + h))],
        out_specs=pl.BlockSpec((1, tq, LANES), lambda b, h, i: (b, i, h)),
        scratch_shapes=[pltpu.VMEM((tq, LANES), F32),
                        pltpu.VMEM((heads, tq, LANES), F32),
                        pltpu.VMEM((2, heads, tq, tk), F32),
                        pltpu.VMEM((2, heads, tq, tk), BF16)],
        compiler_params=_params("parallel", "parallel", "parallel"),
        name="stick",
    )(b3, b3, b3).reshape(bsz * seq, WIDTH)


def _merge_body(x_ref, oa_ref, ob_ref,
                g1_ref, wg_ref, bg_ref, pa_ref, pb_ref, wo_ref, g2_ref, wrh_ref, wrl_ref, br_ref,
                x1_ref, h2_ref, slot_ref, gate_ref, cnt_ref, before_ref, carry_ref):
    tm, d = x_ref.shape

    @pl.when(pl.program_id(0) == 0)
    def _():
        carry_ref[...] = jnp.zeros_like(carry_ref)

    def dot(a, w_ref, lo, hi):
        return jnp.dot(a, w_ref[:, lo:hi], preferred_element_type=F32)

    def mix_and_route(rows):
        x = x_ref[rows, :]
        h = _rms(x, g1_ref[...]).astype(BF16)
        gate_a = jax.nn.sigmoid(dot(h, wg_ref, 0, d) + bg_ref[:, :d])
        gate_b = jax.nn.sigmoid(dot(h, wg_ref, d, 2 * d) + bg_ref[:, d:])
        merged = (gate_a * dot(oa_ref[rows, :], pa_ref, 0, d)
                  + gate_b * dot(ob_ref[rows, :], pb_ref, 0, d)).astype(BF16)
        x1 = x + jnp.concatenate([dot(merged, wo_ref, 0, d // 2), dot(merged, wo_ref, d // 2, d)],
                                 axis=1)
        x1_ref[rows, :] = x1
        h2 = _rms(x1, g2_ref[...])
        h2_ref[rows, :] = h2

        h2_hi = h2.astype(BF16)
        h2_lo = (h2 - h2_hi.astype(F32)).astype(BF16)
        work = (dot(h2_hi, wrh_ref, 0, N_EXPERTS) + dot(h2_lo, wrh_ref, 0, N_EXPERTS)
                + dot(h2_hi, wrl_ref, 0, N_EXPERTS) + br_ref[...])
        eidx = lax.broadcasted_iota(I32, work.shape, 1)
        tops, hots = [], []
        for _ in range(TOP_K):
            mk = jnp.max(work, axis=-1, keepdims=True)
            ik = jnp.min(jnp.where(work == mk, eidx, N_EXPERTS), axis=-1, keepdims=True)
            hot = eidx == ik
            work = jnp.where(hot, -jnp.inf, work)
            tops.append(mk)
            hots.append(hot)
        exps = [jnp.exp(t - tops[0]) for t in tops]
        den = exps[0] + exps[1] + exps[2] + exps[3]
        gate_ref[rows, :] = jnp.concatenate([e / den for e in exps], axis=1)
        return [hot.astype(F32) for hot in hots]

    hots = mix_and_route(slice(0, tm))

    cnt = sum(hots)
    ri = lax.broadcasted_iota(I32, (tm, tm), 0)
    ci = lax.broadcasted_iota(I32, (tm, tm), 1)
    earlier = (ci < ri).astype(BF16)
    rank = jnp.dot(earlier, cnt.astype(BF16), preferred_element_type=F32)
    tile_cnt = jnp.sum(cnt, axis=0, keepdims=True)
    seg_len = jnp.ceil(tile_cnt * (1.0 / SEG_ALIGN)) * SEG_ALIGN
    ei = lax.broadcasted_iota(I32, (N_EXPERTS, N_EXPERTS), 0)
    ej = lax.broadcasted_iota(I32, (N_EXPERTS, N_EXPERTS), 1)
    seg_start = jnp.dot(jnp.broadcast_to(seg_len, (8, N_EXPERTS)).astype(BF16),
                        (ei < ej).astype(BF16), preferred_element_type=F32)[0:1]
    place = rank + seg_start
    slots = [jnp.sum(hot * place, axis=-1, keepdims=True) for hot in hots]
    slot_ref[...] = jnp.concatenate(slots, axis=1).astype(I32)
    before_ref[0] = carry_ref[...].astype(I32)
    total = carry_ref[...] + tile_cnt
    carry_ref[...] = total
    cnt_ref[...] = total.astype(I32)


def _merge(x2, o_a, o_b, g1, w_gate, b_gate, w_pa, w_pb, w_out, g2, w_r, b_r):
    n, d = x2.shape
    w_r_hi = w_r.astype(BF16)
    w_r_lo = (w_r - w_r_hi.astype(F32)).astype(BF16)
    tm = MERGE_ROWS
    row = lambda w: pl.BlockSpec((tm, w), lambda i: (i, 0))
    return pl.pallas_call(
        _merge_body,
        out_shape=(jax.ShapeDtypeStruct((n, d), F32),
                   jax.ShapeDtypeStruct((n, d), F32),
                   jax.ShapeDtypeStruct((n, TOP_K), I32),
                   jax.ShapeDtypeStruct((n, TOP_K), F32),
                   jax.ShapeDtypeStruct((1, N_EXPERTS), I32),
                   jax.ShapeDtypeStruct((n // tm, 1, N_EXPERTS), I32)),
        grid=(n // tm,),
        in_specs=[row(d), row(WIDTH), row(WIDTH),
                  _const_spec((1, d)), _const_spec((d, 2 * d)), _const_spec((1, 2 * d)),
                  _const_spec((WIDTH, d)), _const_spec((WIDTH, d)), _const_spec((d, d)),
                  _const_spec((1, d)), _const_spec((d, N_EXPERTS)), _const_spec((d, N_EXPERTS)),
                  _const_spec((1, N_EXPERTS))],
        out_specs=(row(d), row(d), row(TOP_K), row(TOP_K),
                   pl.BlockSpec((1, N_EXPERTS), lambda i: (0, 0)),
                   pl.BlockSpec((1, 1, N_EXPERTS), lambda i: (i, 0, 0))),
        scratch_shapes=[pltpu.VMEM((1, N_EXPERTS), F32)],
        compiler_params=_params("arbitrary"),
        name="merge",
    )(x2, o_a, o_b, g1, w_gate, b_gate, w_pa, w_pb, w_out, g2, w_r_hi, w_r_lo, b_r)


def _segment_copies(len_ref, src_ref, dst_ref, tile, make_copy, wait):
    for e in range(N_EXPERTS):
        base = tile * N_EXPERTS + e
        length, src, dst = len_ref[base], src_ref[base], dst_ref[base]
        size = MOE_TILE
        while size >= SEG_ALIGN:
            done = (length // (2 * size)) * (2 * size)

            @pl.when((length & size) != 0)
            def _(size=size, done=done):
                cp = make_copy(pl.multiple_of(src + done, SEG_ALIGN),
                               pl.multiple_of(dst + done, SEG_ALIGN), size)
                if wait:
                    cp.wait()
                else:
                    cp.start()

            size //= 2


def _dispatch_body(len_ref, src_ref, dst_ref, pad_ref, nb_ref, slot_ref, h_ref,
                   xs_hbm, stage_ref, zero_ref, sem, zero_sem):
    tm = h_ref.shape[0]
    n_blocks = xs_hbm.shape[0] // EXPERT_ROWS
    tile = pl.program_id(0)
    buf = tile % 2

    @pl.when(tile == 0)
    def _():
        zero_ref[...] = jnp.zeros_like(zero_ref)
        for e in range(N_EXPERTS):
            r0 = pl.multiple_of(pad_ref[e], EXPERT_ROWS)
            pltpu.make_async_copy(zero_ref, xs_hbm.at[pl.ds(r0, EXPERT_ROWS)], zero_sem).start()
        for e in range(N_EXPERTS):
            pltpu.make_async_copy(zero_ref, xs_hbm.at[pl.ds(0, EXPERT_ROWS)], zero_sem).wait()

        def fill(b, carry):
            r0 = pl.multiple_of(b * EXPERT_ROWS, EXPERT_ROWS)
            cp = pltpu.make_async_copy(zero_ref, xs_hbm.at[pl.ds(r0, EXPERT_ROWS)], zero_sem)
            cp.start()
            cp.wait()
            return carry

        lax.fori_loop(nb_ref[0], n_blocks, fill, 0)

    slots = slot_ref[...]
    col = lax.broadcasted_iota(I32, (tm, STAGE_ROWS), 1)
    hit = col == slots[:, 0:1]
    for k in range(1, TOP_K):
        hit = hit | (col == slots[:, k:k + 1])
    pick = jnp.where(hit, 1.0, 0.0).astype(BF16)
    stage_ref[buf] = lax.dot_general(pick, h_ref[...].astype(BF16), (((0,), (0,)), ((), ())),
                                     preferred_element_type=F32)

    def copy_from(b):
        return lambda src, dst, size: pltpu.make_async_copy(
            stage_ref.at[b, pl.ds(src, size)], xs_hbm.at[pl.ds(dst, size)], sem)

    @pl.when(tile > 0)
    def _():
        _segment_copies(len_ref, src_ref, dst_ref, tile - 1, copy_from(1 - buf), wait=True)

    _segment_copies(len_ref, src_ref, dst_ref, tile, copy_from(buf), wait=False)

    @pl.when(tile == pl.num_programs(0) - 1)
    def _():
        _segment_copies(len_ref, src_ref, dst_ref, tile, copy_from(buf), wait=True)


def _dispatch(seg_len, seg_src, seg_dst, pad_start, n_used, slots, h2, n_rows):
    n, d = h2.shape
    tm = MOE_TILE
    tiled = lambda w: pl.BlockSpec((tm, w), lambda i, *_: (i, 0))
    return pl.pallas_call(
        _dispatch_body,
        out_shape=jax.ShapeDtypeStruct((n_rows, d), F32),
        grid_spec=pltpu.PrefetchScalarGridSpec(
            num_scalar_prefetch=5,
            grid=(n // tm,),
            in_specs=[tiled(TOP_K), tiled(d)],
            out_specs=pl.BlockSpec(memory_space=pl.ANY),
            scratch_shapes=[pltpu.VMEM((2, STAGE_ROWS, d), F32),
                            pltpu.VMEM((EXPERT_ROWS, d), F32),
                            pltpu.SemaphoreType.DMA,
                            pltpu.SemaphoreType.DMA]),
        compiler_params=_params("arbitrary"),
        name="dispatch",
    )(seg_len, seg_src, seg_dst, pad_start, n_used, slots, h2)


def _expert_body(be_ref, nb_ref, x_ref, w1_ref, b1_ref, w2_ref, b2_ref, y_ref, w1b_ref, w2b_ref):
    f = w2_ref.shape[1]
    b = pl.program_id(0)

    @pl.when((b == 0) | (be_ref[b] != be_ref[jnp.maximum(b - 1, 0)]))
    def _():
        w1b_ref[...] = w1_ref[0].astype(BF16)
        w2b_ref[...] = w2_ref[0].astype(BF16)

    @pl.when(b < nb_ref[0])
    def _():
        xb = x_ref[...].astype(BF16)
        hid = jnp.dot(xb, w1b_ref[...], preferred_element_type=F32) + b1_ref[0]
        glu = jnp.minimum(hid[:, :f], SWIGLU_LIMIT)
        lin = jnp.clip(hid[:, f:], -SWIGLU_LIMIT, SWIGLU_LIMIT)
        act = glu * jax.nn.sigmoid(SWIGLU_ALPHA * glu) * (lin + 1.0)
        y_ref[...] = jnp.dot(act.astype(BF16), w2b_ref[...], preferred_element_type=F32) + b2_ref[0]

    @pl.when(b >= nb_ref[0])
    def _():
        y_ref[...] = jnp.zeros_like(y_ref)


def _experts(block_e, n_used, xs, w1, b1, w2, b2, n_blocks):
    d = xs.shape[1]
    f = w2.shape[1]
    br = EXPERT_ROWS
    blk = lambda b, be, nb: (jnp.minimum(b, nb[0] - 1), 0)
    per_e = lambda b, be, nb: (be[b], 0, 0)
    return pl.pallas_call(
        _expert_body,
        out_shape=jax.ShapeDtypeStruct((n_blocks * br, d), F32),
        grid_spec=pltpu.PrefetchScalarGridSpec(
            num_scalar_prefetch=2,
            grid=(n_blocks,),
            in_specs=[pl.BlockSpec((br, d), blk),
                      pl.BlockSpec((1, d, 2 * f), per_e),
                      pl.BlockSpec((1, 1, 2 * f), per_e),
                      pl.BlockSpec((1, f, d), per_e),
                      pl.BlockSpec((1, 1, d), per_e)],
            out_specs=pl.BlockSpec((br, d), lambda b, be, nb: (b, 0)),
            scratch_shapes=[pltpu.VMEM((d, 2 * f), BF16), pltpu.VMEM((f, d), BF16)]),
        compiler_params=_params("arbitrary"),
        name="experts",
    )(block_e, n_used, xs, w1, b1, w2, b2)


def _combine_body(len_ref, src_ref, dst_ref, slot_ref, x1_ref, gate_ref, g_ref,
                  y_hbm, o_ref, stage_ref, sem, *, final_norm):
    tm = x1_ref.shape[0]
    tile = pl.program_id(0)
    buf = tile % 2

    def copy_into(b):
        return lambda src, dst, size: pltpu.make_async_copy(
            y_hbm.at[pl.ds(dst, size)], stage_ref.at[b, pl.ds(src, size)], sem)

    @pl.when(tile == 0)
    def _():
        stage_ref[...] = jnp.zeros_like(stage_ref)
        _segment_copies(len_ref, src_ref, dst_ref, tile, copy_into(0), wait=False)

    _segment_copies(len_ref, src_ref, dst_ref, tile, copy_into(buf), wait=True)

    @pl.when(tile + 1 < pl.num_programs(0))
    def _():
        _segment_copies(len_ref, src_ref, dst_ref, tile + 1, copy_into(1 - buf), wait=False)

    slots = slot_ref[...]
    gate = gate_ref[...]
    col = lax.broadcasted_iota(I32, (tm, STAGE_ROWS), 1)
    weight = jnp.zeros((tm, STAGE_ROWS), F32)
    for k in range(TOP_K):
        weight = jnp.where(col == slots[:, k:k + 1], gate[:, k:k + 1], weight)
    moe = jnp.dot(weight.astype(BF16), stage_ref[buf].astype(BF16), preferred_element_type=F32)
    acc = x1_ref[...] + moe
    o_ref[...] = _rms(acc, g_ref[...]) if final_norm else acc


def _combine(seg_len, seg_src, seg_dst, slots, x1, gates, g_f, y, final_norm):
    n, d = x1.shape
    tm = MOE_TILE
    tiled = lambda w: pl.BlockSpec((tm, w), lambda i, *_: (i, 0))
    return pl.pallas_call(
        functools.partial(_combine_body, final_norm=final_norm),
        out_shape=jax.ShapeDtypeStruct((n, d), F32),
        grid_spec=pltpu.PrefetchScalarGridSpec(
            num_scalar_prefetch=3,
            grid=(n // tm,),
            in_specs=[tiled(TOP_K), tiled(d), tiled(TOP_K),
                      pl.BlockSpec((1, d), lambda i, *_: (0, 0)),
                      pl.BlockSpec(memory_space=pl.ANY)],
            out_specs=tiled(d),
            scratch_shapes=[pltpu.VMEM((2, STAGE_ROWS, d), F32),
                            pltpu.SemaphoreType.DMA]),
        compiler_params=_params("arbitrary"),
        name="combine",
    )(seg_len, seg_src, seg_dst, slots, x1, gates, g_f, y)


def _layer(x2, bsz, seq, p):
    n, d = x2.shape
    qkv_a, qkv_b = _proj(x2, p["g1"], p["w_in"], seq)
    o_a = _dilated(qkv_a, bsz, seq)
    o_b = _stick(qkv_b, bsz, seq)
    x1, h2, slots, gates, counts, before = _merge(
        x2, o_a, o_b, p["g1"], p["w_gate"], p["b_gate"], p["w_pa"], p["w_pb"], p["w_out"],
        p["g2"], p["w_r"], p["b_r"])

    br = EXPERT_ROWS
    n_tiles = n // MOE_TILE
    before = before.reshape(n_tiles, N_EXPERTS)
    seg_cnt = jnp.concatenate([before[1:], counts], axis=0) - before
    seg_len = (seg_cnt + SEG_ALIGN - 1) // SEG_ALIGN * SEG_ALIGN
    seg_src = jnp.cumsum(seg_len, axis=1) - seg_len
    rows_e = jnp.sum(seg_len, axis=0)
    padded = (rows_e + br - 1) // br * br
    ends = jnp.cumsum(padded)
    starts = ends - padded
    seg_dst = starts[None, :] + jnp.cumsum(seg_len, axis=0) - seg_len
    n_blocks = (n * TOP_K + n_tiles * N_EXPERTS * (SEG_ALIGN - 1)) // br + N_EXPERTS
    block_row = jnp.arange(n_blocks, dtype=I32) * br
    block_e = jnp.minimum(jnp.sum(ends[None, :] <= block_row[:, None], axis=1),
                          N_EXPERTS - 1).astype(I32)
    n_used = (ends[-1:] // br).astype(I32)
    pad_start = (starts + rows_e // br * br).astype(I32)
    tables = [t.astype(I32).reshape(n_tiles * N_EXPERTS) for t in (seg_len, seg_src, seg_dst)]

    xs = _dispatch(*tables, pad_start, n_used, slots, h2, n_blocks * br)
    y = _experts(block_e, n_used, xs, p["w1"], p["b1"], p["w2"], p["b2"], n_blocks)
    return (*tables, slots), x1, gates, y


def kernel(x, norm1_g, w_in, w_proj_a, w_proj_b, w_gate, b_gate, w_out, norm2_g,
           w_router, b_router, w1, b1, w2, b2, norm_f_g):
    bsz, seq, d = x.shape
    depth = w_in.shape[0]
    x2 = x.reshape(bsz * seq, d)
    for l in range(depth):
        p = dict(
            g1=norm1_g[l].reshape(1, d), w_in=w_in[l].astype(BF16),
            w_gate=w_gate[l].astype(BF16), b_gate=b_gate[l].reshape(1, 2 * d),
            w_pa=w_proj_a[l].astype(BF16), w_pb=w_proj_b[l].astype(BF16),
            w_out=w_out[l].astype(BF16), g2=norm2_g[l].reshape(1, d),
            w_r=w_router[l], b_r=b_router[l].reshape(1, N_EXPERTS),
            w1=w1[l], b1=b1[l].reshape(N_EXPERTS, 1, -1),
            w2=w2[l], b2=b2[l].reshape(N_EXPERTS, 1, -1))
        route, x1, gates, y = _layer(x2, bsz, seq, p)
        x2 = _combine(*route, x1, gates, norm_f_g.reshape(1, d), y, final_norm=(l == depth - 1))
    return x2.reshape(bsz, seq, d)
```

```python
import functools

import jax
import jax.numpy as jnp
from jax import lax
from jax.experimental import pallas as pl
from jax.experimental.pallas import tpu as pltpu

F32 = jnp.float32
BF16 = jnp.bfloat16
I32 = jnp.int32

HEAD_DIM = 64
WIDTH = 512
QKV = 3 * WIDTH
DILATED_GROUPS = ((128, 1), (512, 4), (2048, 16))
BAND = 128
ROPE_THETA = 500000.0
ROT_DIM = HEAD_DIM // 4
N_EXPERTS = 32
TOP_K = 4
SWIGLU_ALPHA = 1.702
SWIGLU_LIMIT = 7.0
NORM_EPS = 1e-5

LANES = 128
VMEM_LIMIT = 56 * 1024 * 1024

LOG2E = 1.4426950408889634

PROJ_ROWS = 512
MERGE_ROWS = 512
STICK_ROWS = 512
STICK_KEYS = 256
STICK_TRIP = 4
EXPERT_ROWS = 512
MOE_TILE = MERGE_ROWS
SEG_ALIGN = 8
SEG_RARE = 128
STAGE_ROWS = MOE_TILE * TOP_K + N_EXPERTS * SEG_ALIGN
DIL_UNROLL = 16
MIX_ROWS = 512


def _rms(x, g):
    return x * lax.rsqrt(jnp.mean(x * x, axis=-1, keepdims=True) + NORM_EPS) * g


def _params(*sem):
    return pltpu.CompilerParams(dimension_semantics=sem, vmem_limit_bytes=VMEM_LIMIT)


def _const_spec(shape):
    return pl.BlockSpec(shape, lambda *_: (0,) * len(shape), pipeline_mode=pl.Buffered(1))


def _proj_body(x_ref, g_ref, w_ref, cos_ref, sa_ref, sb_ref, a_ref, b_ref):
    h = _rms(x_ref[...], g_ref[...]).astype(BF16)
    rep = WIDTH // LANES
    cos = jnp.concatenate([cos_ref[...]] * rep, axis=1)
    sa = jnp.concatenate([sa_ref[...]] * rep, axis=1)
    sb = jnp.concatenate([sb_ref[...]] * rep, axis=1)
    half = ROT_DIM // 2
    for c in range(6):
        acc = jnp.dot(h, w_ref[:, c * WIDTH:(c + 1) * WIDTH], preferred_element_type=F32)
        if c in (0, 1):
            acc = (acc * cos + pltpu.roll(acc, WIDTH - half, 1) * sa
                   + pltpu.roll(acc, half, 1) * sb)
        if c == 0:
            acc = acc * (HEAD_DIM ** -0.5)
        if c == 3:
            acc = acc * (HEAD_DIM ** -0.5 * LOG2E)
        dst = a_ref if c < 3 else b_ref
        dst[:, (c % 3) * WIDTH:(c % 3 + 1) * WIDTH] = acc.astype(dst.dtype)


def _rotary_tables(seq):
    half = ROT_DIM // 2
    inv_freq = jnp.float32(ROPE_THETA) ** (-jnp.arange(0, ROT_DIM, 2, dtype=F32) / ROT_DIM)
    ang = jnp.arange(seq, dtype=jnp.int32).astype(F32)[:, None] * inv_freq[None, :]
    cos, sin = jnp.cos(ang), jnp.sin(ang)
    ones = jnp.ones((seq, HEAD_DIM - ROT_DIM), F32)
    zeros_h = jnp.zeros((seq, half), F32)
    zeros_r = jnp.zeros((seq, HEAD_DIM - ROT_DIM), F32)
    cos_t = jnp.concatenate([cos, cos, ones], axis=1)
    sa_t = jnp.concatenate([-sin, zeros_h, zeros_r], axis=1)
    sb_t = jnp.concatenate([zeros_h, sin, zeros_r], axis=1)
    rep = LANES // HEAD_DIM
    return tuple(jnp.tile(t, (1, rep)) for t in (cos_t, sa_t, sb_t))


def _proj(x2, g1, w_in, seq):
    n, d = x2.shape
    tm = PROJ_ROWS
    tabs = _rotary_tables(seq)
    per_seq = seq // tm
    tab_spec = pl.BlockSpec((tm, LANES), lambda i: (i % per_seq, 0))
    return pl.pallas_call(
        _proj_body,
        out_shape=(jax.ShapeDtypeStruct((n, QKV), F32), jax.ShapeDtypeStruct((n, QKV), BF16)),
        grid=(n // tm,),
        in_specs=[pl.BlockSpec((tm, d), lambda i: (i, 0)),
                  _const_spec((1, d)),
                  _const_spec((d, 2 * QKV)),
                  tab_spec, tab_spec, tab_spec],
        out_specs=(pl.BlockSpec((tm, QKV), lambda i: (i, 0)),
                   pl.BlockSpec((tm, QKV), lambda i: (i, 0))),
        compiler_params=_params("parallel"),
        name="proj",
    )(x2, g1, w_in, *tabs)


def _dil_body(q_ref, k_ref, v_ref, o_ref, os_ref, ls_ref, *, seq):
    heads = LANES // HEAD_DIM
    lane = lax.broadcasted_iota(I32, (1, LANES), 1)
    masks = [(lane >= h * HEAD_DIM) & (lane < (h + 1) * HEAD_DIM) for h in range(heads)]
    qi = lax.broadcasted_iota(I32, (BAND, 2 * BAND), 0)
    ki = lax.broadcasted_iota(I32, (BAND, 2 * BAND), 1)
    neg = jnp.float32(-1e30)
    band_bias = jnp.where((ki >= qi) & (ki <= qi + BAND), 0.0, neg)
    first_bias = jnp.where(ki >= BAND, 0.0, neg)
    nt = (((1,), (1,)), ((), ()))
    n_blocks = seq // BAND

    def rows(ref, start, dil):
        if dil == 1:
            return ref[0, pl.ds(start, BAND), :]
        return ref[0, pl.ds(start, BAND, stride=dil), :]

    for g, (_, dil) in enumerate(DILATED_GROUPS):
        nb = n_blocks // dil

        def block(n, u, last, g=g, dil=dil, nb=nb):
            r = n // nb
            i = n % nb
            cur = r + i * (BAND * dil)
            q = rows(q_ref, cur, dil).astype(BF16)
            k_cur = rows(k_ref, cur, dil).astype(BF16)
            v_cur = rows(v_ref, cur, dil).astype(BF16)
            if u % nb != 0:
                k_prev, v_prev = last
                bias = band_bias
            elif nb <= DIL_UNROLL:
                k_prev, v_prev = k_cur, v_cur
                bias = band_bias + first_bias
            else:
                prev = r + jnp.maximum(i - 1, 0) * (BAND * dil)
                k_prev = rows(k_ref, prev, dil).astype(BF16)
                v_prev = rows(v_ref, prev, dil).astype(BF16)
                bias = band_bias + jnp.where(i > 0, 0.0, 1.0) * first_bias
            kc = jnp.concatenate([k_prev, k_cur], axis=0)
            vc = jnp.concatenate([v_prev, v_cur], axis=0)
            o_t = jnp.zeros((BAND, LANES), F32)
            l_t = jnp.zeros((BAND, LANES), F32)
            for h in range(heads):
                qh = jnp.where(masks[h], q, jnp.zeros_like(q))
                s = lax.dot_general(qh, kc, nt, preferred_element_type=F32) + bias
                m = jnp.max(s, axis=-1, keepdims=True)
                p = jnp.exp(s - m)
                l = jnp.sum(p, axis=-1, keepdims=True)
                pv = jnp.dot(p.astype(BF16), vc, preferred_element_type=F32)
                o_t = jnp.where(masks[h], pv / l, o_t)
                l_t = jnp.where(masks[h], m + jnp.log(l), l_t)
            if dil == 1:
                os_ref[g, pl.ds(cur, BAND), :] = o_t
                ls_ref[g, pl.ds(cur, BAND), :] = l_t
            else:
                os_ref[g, pl.ds(cur, BAND, stride=dil), :] = o_t
                ls_ref[g, pl.ds(cur, BAND, stride=dil), :] = l_t
            return k_cur, v_cur

        def step(t, carry, block=block):
            last = None
            for u in range(DIL_UNROLL):
                last = block(t * DIL_UNROLL + u, u, last)
            return carry

        lax.fori_loop(0, n_blocks // DIL_UNROLL, step, 0)

    def mix(c, carry):
        r0 = pl.multiple_of(c * MIX_ROWS, MIX_ROWS)
        ls = [ls_ref[g, pl.ds(r0, MIX_ROWS), :] for g in range(len(DILATED_GROUPS))]
        m = functools.reduce(jnp.maximum, ls)
        ws = [jnp.exp(l - m) for l in ls]
        num = sum(os_ref[g, pl.ds(r0, MIX_ROWS), :] * w for g, w in enumerate(ws))
        o_ref[0, pl.ds(r0, MIX_ROWS), :] = (num / sum(ws)).astype(BF16)
        return carry

    lax.fori_loop(0, seq // MIX_ROWS, mix, 0)


def _dilated(qkv_a, bsz, seq):
    assert all(w == BAND * d for w, d in DILATED_GROUPS)
    n_blocks = seq // BAND
    assert seq % BAND == 0 and n_blocks % DIL_UNROLL == 0 and seq % MIX_ROWS == 0
    for _, d in DILATED_GROUPS:
        nb = n_blocks // d
        assert n_blocks % d == 0 and (nb % DIL_UNROLL == 0 or DIL_UNROLL % nb == 0)
    a3 = qkv_a.reshape(bsz, seq, QKV)
    hb = WIDTH // LANES
    n_groups = len(DILATED_GROUPS)
    spec = lambda off: pl.BlockSpec((1, seq, LANES), lambda b, h: (b, 0, off * hb + h))
    return pl.pallas_call(
        functools.partial(_dil_body, seq=seq),
        out_shape=jax.ShapeDtypeStruct((bsz, seq, WIDTH), BF16),
        grid=(bsz, hb),
        in_specs=[spec(0), spec(1), spec(2)],
        out_specs=pl.BlockSpec((1, seq, LANES), lambda b, h: (b, 0, h)),
        scratch_shapes=[pltpu.VMEM((n_groups, seq, LANES), F32),
                        pltpu.VMEM((n_groups, seq, LANES), F32)],
        compiler_params=_params("parallel", "parallel"),
        name="dilated",
    )(a3, a3, a3).reshape(bsz * seq, WIDTH)


def _stick_body(q_ref, k_ref, v_ref, o_ref, acc_ref, carry_ref, z_ref, a_ref, *, tq, tk):
    i = pl.program_id(2)
    heads = LANES // HEAD_DIM
    lane = lax.broadcasted_iota(I32, (1, LANES), 1)
    masks = [(lane >= h * HEAD_DIM) & (lane < (h + 1) * HEAD_DIM) for h in range(heads)]
    lr = lax.broadcasted_iota(I32, (tk, tk), 0)
    lc = lax.broadcasted_iota(I32, (tk, tk), 1)
    later = (lr > lc).astype(BF16)
    nt = (((1,), (1,)), ((), ()))
    q = q_ref[0]
    qh = [jnp.where(mk, q, jnp.zeros_like(q)) for mk in masks]
    acc_ref[...] = jnp.zeros_like(acc_ref)
    carry_ref[...] = jnp.zeros_like(carry_ref)
    nd = tq // tk
    n_before = i * nd

    def first_row(j):
        return pl.multiple_of(jnp.maximum(j, 0) * tk, tk)

    def logits(j, top=0):
        kj = k_ref[0, pl.ds(first_row(j), tk), :]
        return [lax.dot_general(qh[h][top:], kj, nt, preferred_element_type=F32)
                for h in range(heads)]

    def weights(zs, j, diag, top=0):
        rows = tq - top
        if diag:
            ti = lax.broadcasted_iota(I32, (rows, tk), 0)
            si = lax.broadcasted_iota(I32, (rows, tk), 1)
            causal = (first_row(j) + si) < (i * tq + top + ti)
        out = []
        for h, z in enumerate(zs):
            neg_abs = pltpu.bitcast(pltpu.bitcast(z, jnp.uint32) | jnp.uint32(0x80000000), F32)
            sp = jnp.maximum(z, 0.0) + jnp.log2(1.0 + jnp.exp2(neg_abs))
            if diag:
                sp = jnp.where(causal, sp, 0.0)
            c = carry_ref[h, top:, :]
            after = (jnp.dot(sp.astype(BF16), later, preferred_element_type=F32)
                     + jnp.concatenate([c] * (tk // LANES), axis=1))
            a = jnp.exp2(z - sp - after)
            if diag:
                a = jnp.where(causal, a, 0.0)
            carry_ref[h, top:, :] = jnp.broadcast_to((after + sp)[:, 0:1], (rows, LANES))
            out.append(a.astype(BF16))
        return out

    def values(a_s, j, top=0):
        vj = v_ref[0, pl.ds(first_row(j), tk), :]
        acc_ref[top:, :] += sum(
            jnp.dot(a_s[h], jnp.where(masks[h], vj, jnp.zeros_like(vj)), preferred_element_type=F32)
            for h in range(heads))

    def stage(slot, arrays, top=0):
        ref = z_ref if arrays[0].dtype == F32 else a_ref
        for h, arr in enumerate(arrays):
            ref[slot, h, top:, :] = arr

    def staged(ref, slot, top=0):
        return [ref[slot, h, top:, :] for h in range(heads)]

    upper, lower, top = n_before + 1, n_before, tk
    stage(0, logits(upper, top), top)
    stage(1, logits(lower))
    stage(0, weights(staged(z_ref, 0, top), upper, True, top), top)
    stage(0, logits(n_before - 1))
    stage(1, weights(staged(z_ref, 1), lower, True))
    values(staged(a_ref, 0, top), upper, top)

    def pipelined(t, p):
        j = n_before - 1 - t
        stage(1 - p, logits(j - 1))
        stage(p, weights(staged(z_ref, p), j, False))
        values(staged(a_ref, 1 - p), j + 1)

    def step(u, carry):
        for k in range(STICK_TRIP):
            pipelined(STICK_TRIP * u + k, k % 2)
        return carry

    lax.fori_loop(0, n_before // STICK_TRIP, step, 0)

    @pl.when(n_before % STICK_TRIP != 0)
    def _():
        for k in range(2):
            pipelined(n_before // STICK_TRIP * STICK_TRIP + k, k)

    values(staged(a_ref, 1), 0)

    o_ref[0] = acc_ref[...].astype(BF16)


def _stick(qkv_b, bsz, seq):
    tq, tk = STICK_ROWS, STICK_KEYS
    assert tq == 2 * tk and seq % tq == 0
    b3 = qkv_b.reshape(bsz, seq, QKV)
    hb = WIDTH // LANES
    heads = LANES // HEAD_DIM
    return pl.pallas_call(
        functools.partial(_stick_body, tq=tq, tk=tk),
        out_shape=jax.ShapeDtypeStruct((bsz, seq, WIDTH), BF16),
        grid=(bsz, hb, seq // tq),
        in_specs=[pl.BlockSpec((1, tq, LANES), lambda b, h, i: (b, i, h)),
                  pl.BlockSpec((1, seq, LANES), lambda b, h, i: (b, 0, hb + h)),
                  pl.BlockSpec((1, seq, LANES), lambda b, h, i: (b, 0, 2 * hb + h))],
        out_specs=pl.BlockSpec((1, tq, LANES), lambda b, h, i: (b, i, h)),
        scratch_shapes=[pltpu.VMEM((tq, LANES), F32),
                        pltpu.VMEM((heads, tq, LANES), F32),
                        pltpu.VMEM((2, heads, tq, tk), F32),
                        pltpu.VMEM((2, heads, tq, tk), BF16)],
        compiler_params=_params("parallel", "parallel", "parallel"),
        name="stick",
    )(b3, b3, b3).reshape(bsz * seq, WIDTH)


def _merge_body(x_ref, oa_ref, ob_ref,
                g1_ref, wg_ref, bg_ref, pa_ref, pb_ref, wo_ref, g2_ref, wrh_ref, wrl_ref, br_ref,
                x1_ref, h2_ref, slot_ref, gate_ref, cnt_ref, before_ref, carry_ref):
    tm, d = x_ref.shape

    @pl.when(pl.program_id(0) == 0)
    def _():
        carry_ref[...] = jnp.zeros_like(carry_ref)

    def dot(a, w_ref, lo, hi):
        return jnp.dot(a, w_ref[:, lo:hi], preferred_element_type=F32)

    def mix_and_route(rows):
        x = x_ref[rows, :]
        h = _rms(x, g1_ref[...]).astype(BF16)
        gate_a = jax.nn.sigmoid(dot(h, wg_ref, 0, d) + bg_ref[:, :d])
        gate_b = jax.nn.sigmoid(dot(h, wg_ref, d, 2 * d) + bg_ref[:, d:])
        merged = (gate_a * dot(oa_ref[rows, :], pa_ref, 0, d)
                  + gate_b * dot(ob_ref[rows, :], pb_ref, 0, d)).astype(BF16)
        x1 = x + jnp.concatenate([dot(merged, wo_ref, 0, d // 2), dot(merged, wo_ref, d // 2, d)],
                                 axis=1)
        x1_ref[rows, :] = x1
        h2 = _rms(x1, g2_ref[...])
        h2_ref[rows, :] = h2

        h2_hi = h2.astype(BF16)
        h2_lo = (h2 - h2_hi.astype(F32)).astype(BF16)
        work = (dot(h2_hi, wrh_ref, 0, N_EXPERTS) + dot(h2_lo, wrh_ref, 0, N_EXPERTS)
                + dot(h2_hi, wrl_ref, 0, N_EXPERTS) + br_ref[...])
        eidx = lax.broadcasted_iota(I32, work.shape, 1)
        tops, hots = [], []
        for _ in range(TOP_K):
            mk = jnp.max(work, axis=-1, keepdims=True)
            ik = jnp.min(jnp.where(work == mk, eidx, N_EXPERTS), axis=-1, keepdims=True)
            hot = eidx == ik
            work = jnp.where(hot, -jnp.inf, work)
            tops.append(mk)
            hots.append(hot)
        exps = [jnp.exp(t - tops[0]) for t in tops]
        den = exps[0] + exps[1] + exps[2] + exps[3]
        gate_ref[rows, :] = jnp.concatenate([e / den for e in exps], axis=1)
        return [hot.astype(F32) for hot in hots]

    hots = mix_and_route(slice(0, tm))

    cnt = sum(hots)
    ri = lax.broadcasted_iota(I32, (tm, tm), 0)
    ci = lax.broadcasted_iota(I32, (tm, tm), 1)
    earlier = (ci < ri).astype(BF16)
    rank = jnp.dot(earlier, cnt.astype(BF16), preferred_element_type=F32)
    tile_cnt = jnp.sum(cnt, axis=0, keepdims=True)
    seg_len = jnp.ceil(tile_cnt * (1.0 / SEG_ALIGN)) * SEG_ALIGN
    ei = lax.broadcasted_iota(I32, (N_EXPERTS, N_EXPERTS), 0)
    ej = lax.broadcasted_iota(I32, (N_EXPERTS, N_EXPERTS), 1)
    seg_start = jnp.dot(jnp.broadcast_to(seg_len, (8, N_EXPERTS)).astype(BF16),
                        (ei < ej).astype(BF16), preferred_element_type=F32)[0:1]
    place = rank + seg_start
    slots = [jnp.sum(hot * place, axis=-1, keepdims=True) for hot in hots]
    slot_ref[...] = jnp.concatenate(slots, axis=1).astype(I32)
    before_ref[0] = carry_ref[...].astype(I32)
    total = carry_ref[...] + tile_cnt
    carry_ref[...] = total
    cnt_ref[...] = total.astype(I32)


def _merge(x2, o_a, o_b, g1, w_gate, b_gate, w_pa, w_pb, w_out, g2, w_r, b_r):
    n, d = x2.shape
    w_r_hi = w_r.astype(BF16)
    w_r_lo = (w_r - w_r_hi.astype(F32)).astype(BF16)
    tm = MERGE_ROWS
    row = lambda w: pl.BlockSpec((tm, w), lambda i: (i, 0))
    return pl.pallas_call(
        _merge_body,
        out_shape=(jax.ShapeDtypeStruct((n, d), F32),
                   jax.ShapeDtypeStruct((n, d), F32),
                   jax.ShapeDtypeStruct((n, TOP_K), I32),
                   jax.ShapeDtypeStruct((n, TOP_K), F32),
                   jax.ShapeDtypeStruct((1, N_EXPERTS), I32),
                   jax.ShapeDtypeStruct((n // tm, 1, N_EXPERTS), I32)),
        grid=(n // tm,),
        in_specs=[row(d), row(WIDTH), row(WIDTH),
                  _const_spec((1, d)), _const_spec((d, 2 * d)), _const_spec((1, 2 * d)),
                  _const_spec((WIDTH, d)), _const_spec((WIDTH, d)), _const_spec((d, d)),
                  _const_spec((1, d)), _const_spec((d, N_EXPERTS)), _const_spec((d, N_EXPERTS)),
                  _const_spec((1, N_EXPERTS))],
        out_specs=(row(d), row(d), row(TOP_K), row(TOP_K),
                   pl.BlockSpec((1, N_EXPERTS), lambda i: (0, 0)),
                   pl.BlockSpec((1, 1, N_EXPERTS), lambda i: (i, 0, 0))),
        scratch_shapes=[pltpu.VMEM((1, N_EXPERTS), F32)],
        compiler_params=_params("arbitrary"),
        name="merge",
    )(x2, o_a, o_b, g1, w_gate, b_gate, w_pa, w_pb, w_out, g2, w_r_hi, w_r_lo, b_r)


def _segment_copies(len_ref, src_ref, dst_ref, tile, make_copy, wait):
    sizes = [MOE_TILE >> s for s in range(MOE_TILE.bit_length()) if MOE_TILE >> s >= SEG_ALIGN]
    rare = [s for s in sizes if s >= SEG_RARE]

    for e in range(N_EXPERTS):
        base = tile * N_EXPERTS + e
        length, src, dst = len_ref[base], src_ref[base], dst_ref[base]

        def piece(size, length=length, src=src, dst=dst):
            done = (length // (2 * size)) * (2 * size)

            @pl.when((length & size) != 0)
            def _():
                cp = make_copy(pl.multiple_of(src + done, SEG_ALIGN),
                               pl.multiple_of(dst + done, SEG_ALIGN), size)
                if wait:
                    cp.wait()
                else:
                    cp.start()

        @pl.when(length >= SEG_RARE)
        def _():
            for size in rare:
                piece(size)

        for size in sizes[len(rare):]:
            piece(size)


def _dispatch_body(len_ref, src_ref, dst_ref, pad_ref, nb_ref, slot_ref, h_ref,
                   xs_hbm, stage_ref, zero_ref, sem, zero_sem):
    tm = h_ref.shape[0]
    n_blocks = xs_hbm.shape[0] // EXPERT_ROWS
    tile = pl.program_id(0)
    buf = tile % 2

    @pl.when(tile == 0)
    def _():
        zero_ref[...] = jnp.zeros_like(zero_ref)
        for e in range(N_EXPERTS):
            r0 = pl.multiple_of(pad_ref[e], EXPERT_ROWS)
            pltpu.make_async_copy(zero_ref, xs_hbm.at[pl.ds(r0, EXPERT_ROWS)], zero_sem).start()
        for e in range(N_EXPERTS):
            pltpu.make_async_copy(zero_ref, xs_hbm.at[pl.ds(0, EXPERT_ROWS)], zero_sem).wait()

        def fill(b, carry):
            r0 = pl.multiple_of(b * EXPERT_ROWS, EXPERT_ROWS)
            cp = pltpu.make_async_copy(zero_ref, xs_hbm.at[pl.ds(r0, EXPERT_ROWS)], zero_sem)
            cp.start()
            cp.wait()
            return carry

        lax.fori_loop(nb_ref[0], n_blocks, fill, 0)

    slots = slot_ref[...]
    col = lax.broadcasted_iota(I32, (tm, STAGE_ROWS), 1)
    hit = col == slots[:, 0:1]
    for k in range(1, TOP_K):
        hit = hit | (col == slots[:, k:k + 1])
    pick = jnp.where(hit, 1.0, 0.0).astype(BF16)
    stage_ref[buf] = lax.dot_general(pick, h_ref[...].astype(BF16), (((0,), (0,)), ((), ())),
                                     preferred_element_type=F32)

    def copy_from(b):
        return lambda src, dst, size: pltpu.make_async_copy(
            stage_ref.at[b, pl.ds(src, size)], xs_hbm.at[pl.ds(dst, size)], sem)

    @pl.when(tile > 0)
    def _():
        _segment_copies(len_ref, src_ref, dst_ref, tile - 1, copy_from(1 - buf), wait=True)

    _segment_copies(len_ref, src_ref, dst_ref, tile, copy_from(buf), wait=False)

    @pl.when(tile == pl.num_programs(0) - 1)
    def _():
        _segment_copies(len_ref, src_ref, dst_ref, tile, copy_from(buf), wait=True)


def _dispatch(seg_len, seg_src, seg_dst, pad_start, n_used, slots, h2, n_rows):
    n, d = h2.shape
    tm = MOE_TILE
    tiled = lambda w: pl.BlockSpec((tm, w), lambda i, *_: (i, 0))
    return pl.pallas_call(
        _dispatch_body,
        out_shape=jax.ShapeDtypeStruct((n_rows, d), F32),
        grid_spec=pltpu.PrefetchScalarGridSpec(
            num_scalar_prefetch=5,
            grid=(n // tm,),
            in_specs=[tiled(TOP_K), tiled(d)],
            out_specs=pl.BlockSpec(memory_space=pl.ANY),
            scratch_shapes=[pltpu.VMEM((2, STAGE_ROWS, d), F32),
                            pltpu.VMEM((EXPERT_ROWS, d), F32),
                            pltpu.SemaphoreType.DMA,
                            pltpu.SemaphoreType.DMA]),
        compiler_params=_params("arbitrary"),
        name="dispatch",
    )(seg_len, seg_src, seg_dst, pad_start, n_used, slots, h2)


def _expert_body(be_ref, nb_ref, x_ref, w1_ref, b1_ref, w2_ref, b2_ref, y_ref, w1b_ref, w2b_ref):
    f = w2_ref.shape[1]
    b = pl.program_id(0)

    @pl.when((b == 0) | (be_ref[b] != be_ref[jnp.maximum(b - 1, 0)]))
    def _():
        w1b_ref[...] = w1_ref[0].astype(BF16)
        w2b_ref[...] = w2_ref[0].astype(BF16)

    @pl.when(b < nb_ref[0])
    def _():
        xb = x_ref[...].astype(BF16)
        hid = jnp.dot(xb, w1b_ref[...], preferred_element_type=F32) + b1_ref[0]
        glu = jnp.minimum(hid[:, :f], SWIGLU_LIMIT)
        lin = jnp.clip(hid[:, f:], -SWIGLU_LIMIT, SWIGLU_LIMIT)
        act = glu * jax.nn.sigmoid(SWIGLU_ALPHA * glu) * (lin + 1.0)
        y_ref[...] = jnp.dot(act.astype(BF16), w2b_ref[...], preferred_element_type=F32) + b2_ref[0]

    @pl.when(b >= nb_ref[0])
    def _():
        y_ref[...] = jnp.zeros_like(y_ref)


def _experts(block_e, n_used, xs, w1, b1, w2, b2, n_blocks):
    d = xs.shape[1]
    f = w2.shape[1]
    br = EXPERT_ROWS
    blk = lambda b, be, nb: (jnp.minimum(b, nb[0] - 1), 0)
    per_e = lambda b, be, nb: (be[b], 0, 0)
    return pl.pallas_call(
        _expert_body,
        out_shape=jax.ShapeDtypeStruct((n_blocks * br, d), F32),
        grid_spec=pltpu.PrefetchScalarGridSpec(
            num_scalar_prefetch=2,
            grid=(n_blocks,),
            in_specs=[pl.BlockSpec((br, d), blk),
                      pl.BlockSpec((1, d, 2 * f), per_e),
                      pl.BlockSpec((1, 1, 2 * f), per_e),
                      pl.BlockSpec((1, f, d), per_e),
                      pl.BlockSpec((1, 1, d), per_e)],
            out_specs=pl.BlockSpec((br, d), lambda b, be, nb: (b, 0)),
            scratch_shapes=[pltpu.VMEM((d, 2 * f), BF16), pltpu.VMEM((f, d), BF16)]),
        compiler_params=_params("arbitrary"),
        name="experts",
    )(block_e, n_used, xs, w1, b1, w2, b2)


def _combine_body(len_ref, src_ref, dst_ref, slot_ref, x1_ref, gate_ref, g_ref,
                  y_hbm, o_ref, stage_ref, sem, *, final_norm):
    tm = x1_ref.shape[0]
    tile = pl.program_id(0)
    buf = tile % 2

    def copy_into(b):
        return lambda src, dst, size: pltpu.make_async_copy(
            y_hbm.at[pl.ds(dst, size)], stage_ref.at[b, pl.ds(src, size)], sem)

    @pl.when(tile == 0)
    def _():
        stage_ref[...] = jnp.zeros_like(stage_ref)
        _segment_copies(len_ref, src_ref, dst_ref, tile, copy_into(0), wait=False)

    _segment_copies(len_ref, src_ref, dst_ref, tile, copy_into(buf), wait=True)

    @pl.when(tile + 1 < pl.num_programs(0))
    def _():
        _segment_copies(len_ref, src_ref, dst_ref, tile + 1, copy_into(1 - buf), wait=False)

    slots = slot_ref[...]
    gate = gate_ref[...]
    col = lax.broadcasted_iota(I32, (tm, STAGE_ROWS), 1)
    weight = jnp.zeros((tm, STAGE_ROWS), F32)
    for k in range(TOP_K):
        weight = jnp.where(col == slots[:, k:k + 1], gate[:, k:k + 1], weight)
    moe = jnp.dot(weight.astype(BF16), stage_ref[buf].astype(BF16), preferred_element_type=F32)
    acc = x1_ref[...] + moe
    o_ref[...] = _rms(acc, g_ref[...]) if final_norm else acc


def _combine(seg_len, seg_src, seg_dst, slots, x1, gates, g_f, y, final_norm):
    n, d = x1.shape
    tm = MOE_TILE
    tiled = lambda w: pl.BlockSpec((tm, w), lambda i, *_: (i, 0))
    return pl.pallas_call(
        functools.partial(_combine_body, final_norm=final_norm),
        out_shape=jax.ShapeDtypeStruct((n, d), F32),
        grid_spec=pltpu.PrefetchScalarGridSpec(
            num_scalar_prefetch=3,
            grid=(n // tm,),
            in_specs=[tiled(TOP_K), tiled(d), tiled(TOP_K),
                      pl.BlockSpec((1, d), lambda i, *_: (0, 0)),
                      pl.BlockSpec(memory_space=pl.ANY)],
            out_specs=tiled(d),
            scratch_shapes=[pltpu.VMEM((2, STAGE_ROWS, d), F32),
                            pltpu.SemaphoreType.DMA]),
        compiler_params=_params("arbitrary"),
        name="combine",
    )(seg_len, seg_src, seg_dst, slots, x1, gates, g_f, y)


def _layer(x2, bsz, seq, p):
    n, d = x2.shape
    qkv_a, qkv_b = _proj(x2, p["g1"], p["w_in"], seq)
    o_a = _dilated(qkv_a, bsz, seq)
    o_b = _stick(qkv_b, bsz, seq)
    x1, h2, slots, gates, counts, before = _merge(
        x2, o_a, o_b, p["g1"], p["w_gate"], p["b_gate"], p["w_pa"], p["w_pb"], p["w_out"],
        p["g2"], p["w_r"], p["b_r"])

    br = EXPERT_ROWS
    n_tiles = n // MOE_TILE
    before = before.reshape(n_tiles, N_EXPERTS)
    seg_cnt = jnp.concatenate([before[1:], counts], axis=0) - before
    seg_len = (seg_cnt + SEG_ALIGN - 1) // SEG_ALIGN * SEG_ALIGN
    seg_src = jnp.cumsum(seg_len, axis=1) - seg_len
    rows_e = jnp.sum(seg_len, axis=0)
    padded = (rows_e + br - 1) // br * br
    ends = jnp.cumsum(padded)
    starts = ends - padded
    seg_dst = starts[None, :] + jnp.cumsum(seg_len, axis=0) - seg_len
    n_blocks = (n * TOP_K + n_tiles * N_EXPERTS * (SEG_ALIGN - 1)) // br + N_EXPERTS
    block_row = jnp.arange(n_blocks, dtype=I32) * br
    block_e = jnp.minimum(jnp.sum(ends[None, :] <= block_row[:, None], axis=1),
                          N_EXPERTS - 1).astype(I32)
    n_used = (ends[-1:] // br).astype(I32)
    pad_start = (starts + rows_e // br * br).astype(I32)
    tables = [t.astype(I32).reshape(n_tiles * N_EXPERTS) for t in (seg_len, seg_src, seg_dst)]

    xs = _dispatch(*tables, pad_start, n_used, slots, h2, n_blocks * br)
    y = _experts(block_e, n_used, xs, p["w1"], p["b1"], p["w2"], p["b2"], n_blocks)
    return (*tables, slots), x1, gates, y


def kernel(x, norm1_g, w_in, w_proj_a, w_proj_b, w_gate, b_gate, w_out, norm2_g,
           w_router, b_router, w1, b1, w2, b2, norm_f_g):
    bsz, seq, d = x.shape
    depth = w_in.shape[0]
    x2 = x.reshape(bsz * seq, d)
    for l in range(depth):
        p = dict(
            g1=norm1_g[l].reshape(1, d), w_in=w_in[l].astype(BF16),
            w_gate=w_gate[l].astype(BF16), b_gate=b_gate[l].reshape(1, 2 * d),
            w_pa=w_proj_a[l].astype(BF16), w_pb=w_proj_b[l].astype(BF16),
            w_out=w_out[l].astype(BF16), g2=norm2_g[l].reshape(1, d),
            w_r=w_router[l], b_r=b_router[l].reshape(1, N_EXPERTS),
            w1=w1[l], b1=b1[l].reshape(N_EXPERTS, 1, -1),
            w2=w2[l], b2=b2[l].reshape(N_EXPERTS, 1, -1))
        route, x1, gates, y = _layer(x2, bsz, seq, p)
        x2 = _combine(*route, x1, gates, norm_f_g.reshape(1, d), y, final_norm=(l == depth - 1))
    return x2.reshape(bsz, seq, d)
```

```python
import functools

import jax
import jax.numpy as jnp
from jax import lax
from jax.experimental import pallas as pl
from jax.experimental.pallas import tpu as pltpu

F32 = jnp.float32
BF16 = jnp.bfloat16
I32 = jnp.int32

HEAD_DIM = 64
WIDTH = 512
QKV = 3 * WIDTH
DILATED_GROUPS = ((128, 1), (512, 4), (2048, 16))
BAND = 128
ROPE_THETA = 500000.0
ROT_DIM = HEAD_DIM // 4
N_EXPERTS = 32
TOP_K = 4
SWIGLU_ALPHA = 1.702
SWIGLU_LIMIT = 7.0
NORM_EPS = 1e-5

LANES = 128
VMEM_LIMIT = 56 * 1024 * 1024

LOG2E = 1.4426950408889634

PROJ_ROWS = 512
MERGE_ROWS = 512
STICK_ROWS = 512
STICK_KEYS = 256
STICK_TRIP = 4
EXPERT_ROWS = 512
MOE_TILE = MERGE_ROWS
SEG_ALIGN = 8
STAGE_ROWS = MOE_TILE * TOP_K + N_EXPERTS * SEG_ALIGN
DIL_UNROLL = 16
MIX_ROWS = 512


def _rms(x, g):
    return x * lax.rsqrt(jnp.mean(x * x, axis=-1, keepdims=True) + NORM_EPS) * g


def _params(*sem):
    return pltpu.CompilerParams(dimension_semantics=sem, vmem_limit_bytes=VMEM_LIMIT)


def _const_spec(shape):
    return pl.BlockSpec(shape, lambda *_: (0,) * len(shape), pipeline_mode=pl.Buffered(1))


def _proj_body(x_ref, g_ref, w_ref, cos_ref, sa_ref, sb_ref, a_ref, b_ref):
    h = _rms(x_ref[...], g_ref[...]).astype(BF16)
    rep = WIDTH // LANES
    cos = jnp.concatenate([cos_ref[...]] * rep, axis=1)
    sa = jnp.concatenate([sa_ref[...]] * rep, axis=1)
    sb = jnp.concatenate([sb_ref[...]] * rep, axis=1)
    half = ROT_DIM // 2
    for c in range(6):
        acc = jnp.dot(h, w_ref[:, c * WIDTH:(c + 1) * WIDTH], preferred_element_type=F32)
        if c in (0, 1):
            acc = (acc * cos + pltpu.roll(acc, WIDTH - half, 1) * sa
                   + pltpu.roll(acc, half, 1) * sb)
        if c == 0:
            acc = acc * (HEAD_DIM ** -0.5)
        if c == 3:
            acc = acc * (HEAD_DIM ** -0.5 * LOG2E)
        dst = a_ref if c < 3 else b_ref
        dst[:, (c % 3) * WIDTH:(c % 3 + 1) * WIDTH] = acc.astype(dst.dtype)


def _rotary_tables(seq):
    half = ROT_DIM // 2
    inv_freq = jnp.float32(ROPE_THETA) ** (-jnp.arange(0, ROT_DIM, 2, dtype=F32) / ROT_DIM)
    ang = jnp.arange(seq, dtype=jnp.int32).astype(F32)[:, None] * inv_freq[None, :]
    cos, sin = jnp.cos(ang), jnp.sin(ang)
    ones = jnp.ones((seq, HEAD_DIM - ROT_DIM), F32)
    zeros_h = jnp.zeros((seq, half), F32)
    zeros_r = jnp.zeros((seq, HEAD_DIM - ROT_DIM), F32)
    cos_t = jnp.concatenate([cos, cos, ones], axis=1)
    sa_t = jnp.concatenate([-sin, zeros_h, zeros_r], axis=1)
    sb_t = jnp.concatenate([zeros_h, sin, zeros_r], axis=1)
    rep = LANES // HEAD_DIM
    return tuple(jnp.tile(t, (1, rep)) for t in (cos_t, sa_t, sb_t))


def _proj(x2, g1, w_in, seq):
    n, d = x2.shape
    tm = PROJ_ROWS
    tabs = _rotary_tables(seq)
    per_seq = seq // tm
    tab_spec = pl.BlockSpec((tm, LANES), lambda i: (i % per_seq, 0))
    return pl.pallas_call(
        _proj_body,
        out_shape=(jax.ShapeDtypeStruct((n, QKV), F32), jax.ShapeDtypeStruct((n, QKV), BF16)),
        grid=(n // tm,),
        in_specs=[pl.BlockSpec((tm, d), lambda i: (i, 0)),
                  _const_spec((1, d)),
                  _const_spec((d, 2 * QKV)),
                  tab_spec, tab_spec, tab_spec],
        out_specs=(pl.BlockSpec((tm, QKV), lambda i: (i, 0)),
                   pl.BlockSpec((tm, QKV), lambda i: (i, 0))),
        compiler_params=_params("parallel"),
        name="proj",
    )(x2, g1, w_in, *tabs)


def _dil_body(q_ref, k_ref, v_ref, o_ref, os_ref, ls_ref, *, seq):
    heads = LANES // HEAD_DIM
    lane = lax.broadcasted_iota(I32, (1, LANES), 1)
    masks = [(lane >= h * HEAD_DIM) & (lane < (h + 1) * HEAD_DIM) for h in range(heads)]
    qi = lax.broadcasted_iota(I32, (BAND, 2 * BAND), 0)
    ki = lax.broadcasted_iota(I32, (BAND, 2 * BAND), 1)
    neg = jnp.float32(-1e30)
    band_bias = jnp.where((ki >= qi) & (ki <= qi + BAND), 0.0, neg)
    first_bias = jnp.where(ki >= BAND, 0.0, neg)
    nt = (((1,), (1,)), ((), ()))
    n_blocks = seq // BAND

    def rows(ref, start, dil):
        if dil == 1:
            return ref[0, pl.ds(start, BAND), :]
        return ref[0, pl.ds(start, BAND, stride=dil), :]

    for g, (_, dil) in enumerate(DILATED_GROUPS):
        nb = n_blocks // dil

        def block(n, u, last, g=g, dil=dil, nb=nb):
            r = n // nb
            i = n % nb
            cur = r + i * (BAND * dil)
            q = rows(q_ref, cur, dil).astype(BF16)
            k_cur = rows(k_ref, cur, dil).astype(BF16)
            v_cur = rows(v_ref, cur, dil).astype(BF16)
            if u % nb != 0:
                k_prev, v_prev = last
                bias = band_bias
            elif nb <= DIL_UNROLL:
                k_prev, v_prev = k_cur, v_cur
                bias = band_bias + first_bias
            else:
                prev = r + jnp.maximum(i - 1, 0) * (BAND * dil)
                k_prev = rows(k_ref, prev, dil).astype(BF16)
                v_prev = rows(v_ref, prev, dil).astype(BF16)
                bias = band_bias + jnp.where(i > 0, 0.0, 1.0) * first_bias
            kc = jnp.concatenate([k_prev, k_cur], axis=0)
            vc = jnp.concatenate([v_prev, v_cur], axis=0)
            o_t = jnp.zeros((BAND, LANES), F32)
            l_t = jnp.zeros((BAND, LANES), F32)
            for h in range(heads):
                qh = jnp.where(masks[h], q, jnp.zeros_like(q))
                s = lax.dot_general(qh, kc, nt, preferred_element_type=F32) + bias
                m = jnp.max(s, axis=-1, keepdims=True)
                p = jnp.exp(s - m)
                l = jnp.sum(p, axis=-1, keepdims=True)
                pv = jnp.dot(p.astype(BF16), vc, preferred_element_type=F32)
                o_t = jnp.where(masks[h], pv / l, o_t)
                l_t = jnp.where(masks[h], m + jnp.log(l), l_t)
            if dil == 1:
                os_ref[g, pl.ds(cur, BAND), :] = o_t
                ls_ref[g, pl.ds(cur, BAND), :] = l_t
            else:
                os_ref[g, pl.ds(cur, BAND, stride=dil), :] = o_t
                ls_ref[g, pl.ds(cur, BAND, stride=dil), :] = l_t
            return k_cur, v_cur

        def step(t, carry, block=block):
            last = None
            for u in range(DIL_UNROLL):
                last = block(t * DIL_UNROLL + u, u, last)
            return carry

        lax.fori_loop(0, n_blocks // DIL_UNROLL, step, 0)

    def mix(c, carry):
        r0 = pl.multiple_of(c * MIX_ROWS, MIX_ROWS)
        ls = [ls_ref[g, pl.ds(r0, MIX_ROWS), :] for g in range(len(DILATED_GROUPS))]
        m = functools.reduce(jnp.maximum, ls)
        ws = [jnp.exp(l - m) for l in ls]
        num = sum(os_ref[g, pl.ds(r0, MIX_ROWS), :] * w for g, w in enumerate(ws))
        o_ref[0, pl.ds(r0, MIX_ROWS), :] = (num / sum(ws)).astype(BF16)
        return carry

    lax.fori_loop(0, seq // MIX_ROWS, mix, 0)


def _dilated(qkv_a, bsz, seq):
    assert all(w == BAND * d for w, d in DILATED_GROUPS)
    n_blocks = seq // BAND
    assert seq % BAND == 0 and n_blocks % DIL_UNROLL == 0 and seq % MIX_ROWS == 0
    for _, d in DILATED_GROUPS:
        nb = n_blocks // d
        assert n_blocks % d == 0 and (nb % DIL_UNROLL == 0 or DIL_UNROLL % nb == 0)
    a3 = qkv_a.reshape(bsz, seq, QKV)
    hb = WIDTH // LANES
    n_groups = len(DILATED_GROUPS)
    spec = lambda off: pl.BlockSpec((1, seq, LANES), lambda b, h: (b, 0, off * hb + h))
    return pl.pallas_call(
        functools.partial(_dil_body, seq=seq),
        out_shape=jax.ShapeDtypeStruct((bsz, seq, WIDTH), BF16),
        grid=(bsz, hb),
        in_specs=[spec(0), spec(1), spec(2)],
        out_specs=pl.BlockSpec((1, seq, LANES), lambda b, h: (b, 0, h)),
        scratch_shapes=[pltpu.VMEM((n_groups, seq, LANES), F32),
                        pltpu.VMEM((n_groups, seq, LANES), F32)],
        compiler_params=_params("parallel", "parallel"),
        name="dilated",
    )(a3, a3, a3).reshape(bsz * seq, WIDTH)


def _stick_body(q_ref, k_ref, v_ref, o_ref, acc_ref, carry_ref, z_ref, a_ref, *, tq, tk):
    i = pl.program_id(2)
    heads = LANES // HEAD_DIM
    lane = lax.broadcasted_iota(I32, (1, LANES), 1)
    masks = [(lane >= h * HEAD_DIM) & (lane < (h + 1) * HEAD_DIM) for h in range(heads)]
    lr = lax.broadcasted_iota(I32, (tk, tk), 0)
    lc = lax.broadcasted_iota(I32, (tk, tk), 1)
    later = (lr > lc).astype(BF16)
    nt = (((1,), (1,)), ((), ()))
    q = q_ref[0]
    qh = [jnp.where(mk, q, jnp.zeros_like(q)) for mk in masks]
    acc_ref[...] = jnp.zeros_like(acc_ref)
    carry_ref[...] = jnp.zeros_like(carry_ref)
    nd = tq // tk
    n_before = i * nd

    def first_row(j):
        return pl.multiple_of(jnp.maximum(j, 0) * tk, tk)

    def logits(j, top=0):
        kj = k_ref[0, pl.ds(first_row(j), tk), :]
        return [lax.dot_general(qh[h][top:], kj, nt, preferred_element_type=F32)
                for h in range(heads)]

    def weights(zs, j, diag, top=0):
        rows = tq - top
        if diag:
            ti = lax.broadcasted_iota(I32, (rows, tk), 0)
            si = lax.broadcasted_iota(I32, (rows, tk), 1)
            causal = (first_row(j) + si) < (i * tq + top + ti)
        out = []
        for h, z in enumerate(zs):
            neg_abs = pltpu.bitcast(pltpu.bitcast(z, jnp.uint32) | jnp.uint32(0x80000000), F32)
            sp = jnp.maximum(z, 0.0) + jnp.log2(1.0 + jnp.exp2(neg_abs))
            if diag:
                sp = jnp.where(causal, sp, 0.0)
            c = carry_ref[h, top:, :]
            after = (jnp.dot(sp.astype(BF16), later, preferred_element_type=F32)
                     + jnp.concatenate([c] * (tk // LANES), axis=1))
            a = jnp.exp2(z - sp - after)
            if diag:
                a = jnp.where(causal, a, 0.0)
            carry_ref[h, top:, :] = jnp.broadcast_to((after + sp)[:, 0:1], (rows, LANES))
            out.append(a.astype(BF16))
        return out

    def values(a_s, j, top=0):
        vj = v_ref[0, pl.ds(first_row(j), tk), :]
        acc_ref[top:, :] += sum(
            jnp.dot(a_s[h], jnp.where(masks[h], vj, jnp.zeros_like(vj)), preferred_element_type=F32)
            for h in range(heads))

    def stage(slot, arrays, top=0):
        ref = z_ref if arrays[0].dtype == F32 else a_ref
        for h, arr in enumerate(arrays):
            ref[slot, h, top:, :] = arr

    def staged(ref, slot, top=0):
        return [ref[slot, h, top:, :] for h in range(heads)]

    upper, lower, top = n_before + 1, n_before, tk
    stage(0, logits(upper, top), top)
    stage(1, logits(lower))
    stage(0, weights(staged(z_ref, 0, top), upper, True, top), top)
    stage(0, logits(n_before - 1))
    stage(1, weights(staged(z_ref, 1), lower, True))
    values(staged(a_ref, 0, top), upper, top)

    def pipelined(t, p):
        j = n_before - 1 - t
        stage(1 - p, logits(j - 1))
        stage(p, weights(staged(z_ref, p), j, False))
        values(staged(a_ref, 1 - p), j + 1)

    def step(u, carry):
        for k in range(STICK_TRIP):
            pipelined(STICK_TRIP * u + k, k % 2)
        return carry

    lax.fori_loop(0, n_before // STICK_TRIP, step, 0)

    @pl.when(n_before % STICK_TRIP != 0)
    def _():
        for k in range(2):
            pipelined(n_before // STICK_TRIP * STICK_TRIP + k, k)

    values(staged(a_ref, 1), 0)

    o_ref[0] = acc_ref[...].astype(BF16)


def _stick(qkv_b, bsz, seq):
    tq, tk = STICK_ROWS, STICK_KEYS
    assert tq == 2 * tk and seq % tq == 0
    b3 = qkv_b.reshape(bsz, seq, QKV)
    hb = WIDTH // LANES
    heads = LANES // HEAD_DIM
    return pl.pallas_call(
        functools.partial(_stick_body, tq=tq, tk=tk),
        out_shape=jax.ShapeDtypeStruct((bsz, seq, WIDTH), BF16),
        grid=(bsz, hb, seq // tq),
        in_specs=[pl.BlockSpec((1, tq, LANES), lambda b, h, i: (b, i, h)),
                  pl.BlockSpec((1, seq, LANES), lambda b, h, i: (b, 0, hb + h)),
                  pl.BlockSpec((1, seq, LANES), lambda b, h, i: (b, 0, 2 * hb + h))],
        out_specs=pl.BlockSpec((1, tq, LANES), lambda b, h, i: (b, i, h)),
        scratch_shapes=[pltpu.VMEM((tq, LANES), F32),
                        pltpu.VMEM((heads, tq, LANES), F32),
                        pltpu.VMEM((2, heads, tq, tk), F32),
                        pltpu.VMEM((2, heads, tq, tk), BF16)],
        compiler_params=_params("parallel", "parallel", "parallel"),
        name="stick",
    )(b3, b3, b3).reshape(bsz * seq, WIDTH)


def _merge_body(x_ref, oa_ref, ob_ref,
                g1_ref, wg_ref, bg_ref, pa_ref, pb_ref, wo_ref, g2_ref, wrh_ref, wrl_ref, br_ref,
                x1_ref, h2_ref, slot_ref, gate_ref, cnt_ref, before_ref, carry_ref, prev_ref):
    tm, d = x_ref.shape
    step = pl.program_id(0)

    @pl.when(step == 0)
    def _():
        carry_ref[...] = jnp.zeros_like(carry_ref)
        prev_ref[...] = jnp.zeros_like(prev_ref)

    def dot(a, w_ref, lo, hi):
        return jnp.dot(a, w_ref[:, lo:hi], preferred_element_type=F32)

    routed = prev_ref[...]
    r_hi = routed.astype(BF16)
    r_lo = (routed - r_hi.astype(F32)).astype(BF16)
    work = (dot(r_hi, wrh_ref, 0, N_EXPERTS) + dot(r_lo, wrh_ref, 0, N_EXPERTS)
            + dot(r_hi, wrl_ref, 0, N_EXPERTS) + br_ref[...])
    eidx = lax.broadcasted_iota(I32, work.shape, 1)
    tops, hots = [], []
    for _ in range(TOP_K):
        mk = jnp.max(work, axis=-1, keepdims=True)
        ik = jnp.min(jnp.where(work == mk, eidx, N_EXPERTS), axis=-1, keepdims=True)
        hot = eidx == ik
        work = jnp.where(hot, -jnp.inf, work)
        tops.append(mk)
        hots.append(hot)
    exps = [jnp.exp(t - tops[0]) for t in tops]
    den = exps[0] + exps[1] + exps[2] + exps[3]
    gate_ref[...] = jnp.concatenate([e / den for e in exps], axis=1)
    real = jnp.where(step > 0, 1.0, 0.0)
    hots = [hot.astype(F32) * real for hot in hots]

    x = x_ref[...]
    h = _rms(x, g1_ref[...]).astype(BF16)
    gate_a = jax.nn.sigmoid(dot(h, wg_ref, 0, d) + bg_ref[:, :d])
    gate_b = jax.nn.sigmoid(dot(h, wg_ref, d, 2 * d) + bg_ref[:, d:])
    merged = (gate_a * dot(oa_ref[...], pa_ref, 0, d)
              + gate_b * dot(ob_ref[...], pb_ref, 0, d)).astype(BF16)
    x1 = x + jnp.concatenate([dot(merged, wo_ref, 0, d // 2), dot(merged, wo_ref, d // 2, d)],
                             axis=1)
    x1_ref[...] = x1
    h2 = _rms(x1, g2_ref[...])
    h2_ref[...] = h2
    prev_ref[...] = h2

    cnt = sum(hots)
    ri = lax.broadcasted_iota(I32, (tm, tm), 0)
    ci = lax.broadcasted_iota(I32, (tm, tm), 1)
    earlier = (ci < ri).astype(BF16)
    rank = jnp.dot(earlier, cnt.astype(BF16), preferred_element_type=F32)
    tile_cnt = jnp.sum(cnt, axis=0, keepdims=True)
    seg_len = jnp.ceil(tile_cnt * (1.0 / SEG_ALIGN)) * SEG_ALIGN
    ei = lax.broadcasted_iota(I32, (N_EXPERTS, N_EXPERTS), 0)
    ej = lax.broadcasted_iota(I32, (N_EXPERTS, N_EXPERTS), 1)
    seg_start = jnp.dot(jnp.broadcast_to(seg_len, (8, N_EXPERTS)).astype(BF16),
                        (ei < ej).astype(BF16), preferred_element_type=F32)[0:1]
    place = rank + seg_start
    slots = [jnp.sum(hot * place, axis=-1, keepdims=True) for hot in hots]
    slot_ref[...] = jnp.concatenate(slots, axis=1).astype(I32)
    before_ref[0] = carry_ref[...].astype(I32)
    total = carry_ref[...] + tile_cnt
    carry_ref[...] = total
    cnt_ref[...] = total.astype(I32)


def _merge(x2, o_a, o_b, g1, w_gate, b_gate, w_pa, w_pb, w_out, g2, w_r, b_r):
    n, d = x2.shape
    w_r_hi = w_r.astype(BF16)
    w_r_lo = (w_r - w_r_hi.astype(F32)).astype(BF16)
    tm = MERGE_ROWS
    n_tiles = n // tm
    row = lambda w: pl.BlockSpec((tm, w), lambda i: (jnp.minimum(i, n_tiles - 1), 0))
    late = lambda w: pl.BlockSpec((tm, w), lambda i: (jnp.maximum(i - 1, 0), 0))
    outs = pl.pallas_call(
        _merge_body,
        out_shape=(jax.ShapeDtypeStruct((n, d), F32),
                   jax.ShapeDtypeStruct((n, d), F32),
                   jax.ShapeDtypeStruct((n, TOP_K), I32),
                   jax.ShapeDtypeStruct((n, TOP_K), F32),
                   jax.ShapeDtypeStruct((1, N_EXPERTS), I32),
                   jax.ShapeDtypeStruct((n // tm, 1, N_EXPERTS), I32)),
        grid=(n_tiles + 1,),
        in_specs=[row(d), row(WIDTH), row(WIDTH),
                  _const_spec((1, d)), _const_spec((d, 2 * d)), _const_spec((1, 2 * d)),
                  _const_spec((WIDTH, d)), _const_spec((WIDTH, d)), _const_spec((d, d)),
                  _const_spec((1, d)), _const_spec((d, N_EXPERTS)), _const_spec((d, N_EXPERTS)),
                  _const_spec((1, N_EXPERTS))],
        out_specs=(row(d), row(d), late(TOP_K), late(TOP_K),
                   pl.BlockSpec((1, N_EXPERTS), lambda i: (0, 0)),
                   pl.BlockSpec((1, 1, N_EXPERTS), lambda i: (jnp.maximum(i - 1, 0), 0, 0))),
        scratch_shapes=[pltpu.VMEM((1, N_EXPERTS), F32), pltpu.VMEM((tm, d), F32)],
        compiler_params=_params("arbitrary"),
        name="merge",
    )(x2, o_a, o_b, g1, w_gate, b_gate, w_pa, w_pb, w_out, g2, w_r_hi, w_r_lo, b_r)
    return outs


def _segment_copies(len_ref, src_ref, dst_ref, tile, make_copy, wait):
    for e in range(N_EXPERTS):
        base = tile * N_EXPERTS + e
        length, src, dst = len_ref[base], src_ref[base], dst_ref[base]
        size = MOE_TILE
        while size >= SEG_ALIGN:
            done = (length // (2 * size)) * (2 * size)

            @pl.when((length & size) != 0)
            def _(size=size, done=done):
                cp = make_copy(pl.multiple_of(src + done, SEG_ALIGN),
                               pl.multiple_of(dst + done, SEG_ALIGN), size)
                if wait:
                    cp.wait()
                else:
                    cp.start()

            size //= 2


def _dispatch_body(len_ref, src_ref, dst_ref, pad_ref, nb_ref, slot_ref, h_ref,
                   xs_hbm, stage_ref, zero_ref, sem, zero_sem):
    tm = h_ref.shape[0]
    n_blocks = xs_hbm.shape[0] // EXPERT_ROWS
    tile = pl.program_id(0)
    buf = tile % 2

    @pl.when(tile == 0)
    def _():
        zero_ref[...] = jnp.zeros_like(zero_ref)
        for e in range(N_EXPERTS):
            r0 = pl.multiple_of(pad_ref[e], EXPERT_ROWS)
            pltpu.make_async_copy(zero_ref, xs_hbm.at[pl.ds(r0, EXPERT_ROWS)], zero_sem).start()
        for e in range(N_EXPERTS):
            pltpu.make_async_copy(zero_ref, xs_hbm.at[pl.ds(0, EXPERT_ROWS)], zero_sem).wait()

        def fill(b, carry):
            r0 = pl.multiple_of(b * EXPERT_ROWS, EXPERT_ROWS)
            cp = pltpu.make_async_copy(zero_ref, xs_hbm.at[pl.ds(r0, EXPERT_ROWS)], zero_sem)
            cp.start()
            cp.wait()
            return carry

        lax.fori_loop(nb_ref[0], n_blocks, fill, 0)

    slots = slot_ref[...]
    col = lax.broadcasted_iota(I32, (tm, STAGE_ROWS), 1)
    hit = col == slots[:, 0:1]
    for k in range(1, TOP_K):
        hit = hit | (col == slots[:, k:k + 1])
    pick = jnp.where(hit, 1.0, 0.0).astype(BF16)
    stage_ref[buf] = lax.dot_general(pick, h_ref[...].astype(BF16), (((0,), (0,)), ((), ())),
                                     preferred_element_type=F32)

    def copy_from(b):
        return lambda src, dst, size: pltpu.make_async_copy(
            stage_ref.at[b, pl.ds(src, size)], xs_hbm.at[pl.ds(dst, size)], sem)

    @pl.when(tile > 0)
    def _():
        _segment_copies(len_ref, src_ref, dst_ref, tile - 1, copy_from(1 - buf), wait=True)

    _segment_copies(len_ref, src_ref, dst_ref, tile, copy_from(buf), wait=False)

    @pl.when(tile == pl.num_programs(0) - 1)
    def _():
        _segment_copies(len_ref, src_ref, dst_ref, tile, copy_from(buf), wait=True)


def _dispatch(seg_len, seg_src, seg_dst, pad_start, n_used, slots, h2, n_rows):
    n, d = h2.shape
    tm = MOE_TILE
    tiled = lambda w: pl.BlockSpec((tm, w), lambda i, *_: (i, 0))
    return pl.pallas_call(
        _dispatch_body,
        out_shape=jax.ShapeDtypeStruct((n_rows, d), F32),
        grid_spec=pltpu.PrefetchScalarGridSpec(
            num_scalar_prefetch=5,
            grid=(n // tm,),
            in_specs=[tiled(TOP_K), tiled(d)],
            out_specs=pl.BlockSpec(memory_space=pl.ANY),
            scratch_shapes=[pltpu.VMEM((2, STAGE_ROWS, d), F32),
                            pltpu.VMEM((EXPERT_ROWS, d), F32),
                            pltpu.SemaphoreType.DMA,
                            pltpu.SemaphoreType.DMA]),
        compiler_params=_params("arbitrary"),
        name="dispatch",
    )(seg_len, seg_src, seg_dst, pad_start, n_used, slots, h2)


def _expert_body(be_ref, nb_ref, x_ref, w1_ref, b1_ref, w2_ref, b2_ref, y_ref, w1b_ref, w2b_ref):
    f = w2_ref.shape[1]
    b = pl.program_id(0)

    @pl.when((b == 0) | (be_ref[b] != be_ref[jnp.maximum(b - 1, 0)]))
    def _():
        w1b_ref[...] = w1_ref[0].astype(BF16)
        w2b_ref[...] = w2_ref[0].astype(BF16)

    @pl.when(b < nb_ref[0])
    def _():
        xb = x_ref[...].astype(BF16)
        hid = jnp.dot(xb, w1b_ref[...], preferred_element_type=F32) + b1_ref[0]
        glu = jnp.minimum(hid[:, :f], SWIGLU_LIMIT)
        lin = jnp.clip(hid[:, f:], -SWIGLU_LIMIT, SWIGLU_LIMIT)
        act = glu * jax.nn.sigmoid(SWIGLU_ALPHA * glu) * (lin + 1.0)
        y_ref[...] = jnp.dot(act.astype(BF16), w2b_ref[...], preferred_element_type=F32) + b2_ref[0]

    @pl.when(b >= nb_ref[0])
    def _():
        y_ref[...] = jnp.zeros_like(y_ref)


def _experts(block_e, n_used, xs, w1, b1, w2, b2, n_blocks):
    d = xs.shape[1]
    f = w2.shape[1]
    br = EXPERT_ROWS
    blk = lambda b, be, nb: (jnp.minimum(b, nb[0] - 1), 0)
    per_e = lambda b, be, nb: (be[b], 0, 0)
    return pl.pallas_call(
        _expert_body,
        out_shape=jax.ShapeDtypeStruct((n_blocks * br, d), F32),
        grid_spec=pltpu.PrefetchScalarGridSpec(
            num_scalar_prefetch=2,
            grid=(n_blocks,),
            in_specs=[pl.BlockSpec((br, d), blk),
                      pl.BlockSpec((1, d, 2 * f), per_e),
                      pl.BlockSpec((1, 1, 2 * f), per_e),
                      pl.BlockSpec((1, f, d), per_e),
                      pl.BlockSpec((1, 1, d), per_e)],
            out_specs=pl.BlockSpec((br, d), lambda b, be, nb: (b, 0)),
            scratch_shapes=[pltpu.VMEM((d, 2 * f), BF16), pltpu.VMEM((f, d), BF16)]),
        compiler_params=_params("arbitrary"),
        name="experts",
    )(block_e, n_used, xs, w1, b1, w2, b2)


def _combine_body(len_ref, src_ref, dst_ref, slot_ref, x1_ref, gate_ref, g_ref,
                  y_hbm, o_ref, stage_ref, sem, *, final_norm):
    tm = x1_ref.shape[0]
    tile = pl.program_id(0)
    buf = tile % 2

    def copy_into(b):
        return lambda src, dst, size: pltpu.make_async_copy(
            y_hbm.at[pl.ds(dst, size)], stage_ref.at[b, pl.ds(src, size)], sem)

    @pl.when(tile == 0)
    def _():
        stage_ref[...] = jnp.zeros_like(stage_ref)
        _segment_copies(len_ref, src_ref, dst_ref, tile, copy_into(0), wait=False)

    _segment_copies(len_ref, src_ref, dst_ref, tile, copy_into(buf), wait=True)

    @pl.when(tile + 1 < pl.num_programs(0))
    def _():
        _segment_copies(len_ref, src_ref, dst_ref, tile + 1, copy_into(1 - buf), wait=False)

    slots = slot_ref[...]
    gate = gate_ref[...]
    col = lax.broadcasted_iota(I32, (tm, STAGE_ROWS), 1)
    weight = jnp.zeros((tm, STAGE_ROWS), F32)
    for k in range(TOP_K):
        weight = jnp.where(col == slots[:, k:k + 1], gate[:, k:k + 1], weight)
    moe = jnp.dot(weight.astype(BF16), stage_ref[buf].astype(BF16), preferred_element_type=F32)
    acc = x1_ref[...] + moe
    o_ref[...] = _rms(acc, g_ref[...]) if final_norm else acc


def _combine(seg_len, seg_src, seg_dst, slots, x1, gates, g_f, y, final_norm):
    n, d = x1.shape
    tm = MOE_TILE
    tiled = lambda w: pl.BlockSpec((tm, w), lambda i, *_: (i, 0))
    return pl.pallas_call(
        functools.partial(_combine_body, final_norm=final_norm),
        out_shape=jax.ShapeDtypeStruct((n, d), F32),
        grid_spec=pltpu.PrefetchScalarGridSpec(
            num_scalar_prefetch=3,
            grid=(n // tm,),
            in_specs=[tiled(TOP_K), tiled(d), tiled(TOP_K),
                      pl.BlockSpec((1, d), lambda i, *_: (0, 0)),
                      pl.BlockSpec(memory_space=pl.ANY)],
            out_specs=tiled(d),
            scratch_shapes=[pltpu.VMEM((2, STAGE_ROWS, d), F32),
                            pltpu.SemaphoreType.DMA]),
        compiler_params=_params("arbitrary"),
        name="combine",
    )(seg_len, seg_src, seg_dst, slots, x1, gates, g_f, y)


def _layer(x2, bsz, seq, p):
    n, d = x2.shape
    qkv_a, qkv_b = _proj(x2, p["g1"], p["w_in"], seq)
    o_a = _dilated(qkv_a, bsz, seq)
    o_b = _stick(qkv_b, bsz, seq)
    x1, h2, slots, gates, counts, before = _merge(
        x2, o_a, o_b, p["g1"], p["w_gate"], p["b_gate"], p["w_pa"], p["w_pb"], p["w_out"],
        p["g2"], p["w_r"], p["b_r"])

    br = EXPERT_ROWS
    n_tiles = n // MOE_TILE
    before = before.reshape(n_tiles, N_EXPERTS)
    seg_cnt = jnp.concatenate([before[1:], counts], axis=0) - before
    seg_len = (seg_cnt + SEG_ALIGN - 1) // SEG_ALIGN * SEG_ALIGN
    seg_src = jnp.cumsum(seg_len, axis=1) - seg_len
    rows_e = jnp.sum(seg_len, axis=0)
    padded = (rows_e + br - 1) // br * br
    ends = jnp.cumsum(padded)
    starts = ends - padded
    seg_dst = starts[None, :] + jnp.cumsum(seg_len, axis=0) - seg_len
    n_blocks = (n * TOP_K + n_tiles * N_EXPERTS * (SEG_ALIGN - 1)) // br + N_EXPERTS
    block_row = jnp.arange(n_blocks, dtype=I32) * br
    block_e = jnp.minimum(jnp.sum(ends[None, :] <= block_row[:, None], axis=1),
                          N_EXPERTS - 1).astype(I32)
    n_used = (ends[-1:] // br).astype(I32)
    pad_start = (starts + rows_e // br * br).astype(I32)
    tables = [t.astype(I32).reshape(n_tiles * N_EXPERTS) for t in (seg_len, seg_src, seg_dst)]

    xs = _dispatch(*tables, pad_start, n_used, slots, h2, n_blocks * br)
    y = _experts(block_e, n_used, xs, p["w1"], p["b1"], p["w2"], p["b2"], n_blocks)
    return (*tables, slots), x1, gates, y


def kernel(x, norm1_g, w_in, w_proj_a, w_proj_b, w_gate, b_gate, w_out, norm2_g,
           w_router, b_router, w1, b1, w2, b2, norm_f_g):
    bsz, seq, d = x.shape
    depth = w_in.shape[0]
    x2 = x.reshape(bsz * seq, d)
    for l in range(depth):
        p = dict(
            g1=norm1_g[l].reshape(1, d), w_in=w_in[l].astype(BF16),
            w_gate=w_gate[l].astype(BF16), b_gate=b_gate[l].reshape(1, 2 * d),
            w_pa=w_proj_a[l].astype(BF16), w_pb=w_proj_b[l].astype(BF16),
            w_out=w_out[l].astype(BF16), g2=norm2_g[l].reshape(1, d),
            w_r=w_router[l], b_r=b_router[l].reshape(1, N_EXPERTS),
            w1=w1[l], b1=b1[l].reshape(N_EXPERTS, 1, -1),
            w2=w2[l], b2=b2[l].reshape(N_EXPERTS, 1, -1))
        route, x1, gates, y = _layer(x2, bsz, seq, p)
        x2 = _combine(*route, x1, gates, norm_f_g.reshape(1, d), y, final_norm=(l == depth - 1))
    return x2.reshape(bsz, seq, d)
```

```python
import functools

import jax
import jax.numpy as jnp
from jax import lax
from jax.experimental import pallas as pl
from jax.experimental.pallas import tpu as pltpu

F32 = jnp.float32
BF16 = jnp.bfloat16
I32 = jnp.int32

HEAD_DIM = 64
WIDTH = 512
QKV = 3 * WIDTH
DILATED_GROUPS = ((128, 1), (512, 4), (2048, 16))
BAND = 128
ROPE_THETA = 500000.0
ROT_DIM = HEAD_DIM // 4
N_EXPERTS = 32
TOP_K = 4
SWIGLU_ALPHA = 1.702
SWIGLU_LIMIT = 7.0
NORM_EPS = 1e-5

LANES = 128
VMEM_LIMIT = 56 * 1024 * 1024

LOG2E = 1.4426950408889634

PROJ_ROWS = 512
MERGE_ROWS = 512
STICK_ROWS = 512
STICK_KEYS = 256
STICK_TRIP = 4
EXPERT_ROWS = 512
MOE_TILE = MERGE_ROWS
SEG_ALIGN = 8
STAGE_ROWS = MOE_TILE * TOP_K + N_EXPERTS * SEG_ALIGN
DIL_UNROLL = 16
MIX_ROWS = 512


def _rms(x, g):
    return x * lax.rsqrt(jnp.mean(x * x, axis=-1, keepdims=True) + NORM_EPS) * g


def _params(*sem):
    return pltpu.CompilerParams(dimension_semantics=sem, vmem_limit_bytes=VMEM_LIMIT)


def _const_spec(shape):
    return pl.BlockSpec(shape, lambda *_: (0,) * len(shape), pipeline_mode=pl.Buffered(1))


def _proj_body(x_ref, g_ref, w_ref, cos_ref, sa_ref, sb_ref, a_ref, b_ref):
    h = _rms(x_ref[...], g_ref[...]).astype(BF16)
    rep = WIDTH // LANES
    cos = jnp.concatenate([cos_ref[...]] * rep, axis=1)
    sa = jnp.concatenate([sa_ref[...]] * rep, axis=1)
    sb = jnp.concatenate([sb_ref[...]] * rep, axis=1)
    half = ROT_DIM // 2
    for c in range(6):
        acc = jnp.dot(h, w_ref[:, c * WIDTH:(c + 1) * WIDTH], preferred_element_type=F32)
        if c in (0, 1):
            acc = (acc * cos + pltpu.roll(acc, WIDTH - half, 1) * sa
                   + pltpu.roll(acc, half, 1) * sb)
        if c == 0:
            acc = acc * (HEAD_DIM ** -0.5)
        if c == 3:
            acc = acc * (HEAD_DIM ** -0.5 * LOG2E)
        dst = a_ref if c < 3 else b_ref
        dst[:, (c % 3) * WIDTH:(c % 3 + 1) * WIDTH] = acc.astype(dst.dtype)


def _rotary_tables(seq):
    half = ROT_DIM // 2
    inv_freq = jnp.float32(ROPE_THETA) ** (-jnp.arange(0, ROT_DIM, 2, dtype=F32) / ROT_DIM)
    ang = jnp.arange(seq, dtype=jnp.int32).astype(F32)[:, None] * inv_freq[None, :]
    cos, sin = jnp.cos(ang), jnp.sin(ang)
    ones = jnp.ones((seq, HEAD_DIM - ROT_DIM), F32)
    zeros_h = jnp.zeros((seq, half), F32)
    zeros_r = jnp.zeros((seq, HEAD_DIM - ROT_DIM), F32)
    cos_t = jnp.concatenate([cos, cos, ones], axis=1)
    sa_t = jnp.concatenate([-sin, zeros_h, zeros_r], axis=1)
    sb_t = jnp.concatenate([zeros_h, sin, zeros_r], axis=1)
    rep = LANES // HEAD_DIM
    return tuple(jnp.tile(t, (1, rep)) for t in (cos_t, sa_t, sb_t))


def _proj(x2, g1, w_in, seq):
    n, d = x2.shape
    tm = PROJ_ROWS
    tabs = _rotary_tables(seq)
    per_seq = seq // tm
    tab_spec = pl.BlockSpec((tm, LANES), lambda i: (i % per_seq, 0))
    return pl.pallas_call(
        _proj_body,
        out_shape=(jax.ShapeDtypeStruct((n, QKV), F32), jax.ShapeDtypeStruct((n, QKV), BF16)),
        grid=(n // tm,),
        in_specs=[pl.BlockSpec((tm, d), lambda i: (i, 0)),
                  _const_spec((1, d)),
                  _const_spec((d, 2 * QKV)),
                  tab_spec, tab_spec, tab_spec],
        out_specs=(pl.BlockSpec((tm, QKV), lambda i: (i, 0)),
                   pl.BlockSpec((tm, QKV), lambda i: (i, 0))),
        compiler_params=_params("parallel"),
        name="proj",
    )(x2, g1, w_in, *tabs)


def _dil_body(q_ref, k_ref, v_ref, o_ref, os_ref, ls_ref, *, seq):
    heads = LANES // HEAD_DIM
    lane = lax.broadcasted_iota(I32, (1, LANES), 1)
    masks = [(lane >= h * HEAD_DIM) & (lane < (h + 1) * HEAD_DIM) for h in range(heads)]
    qi = lax.broadcasted_iota(I32, (BAND, 2 * BAND), 0)
    ki = lax.broadcasted_iota(I32, (BAND, 2 * BAND), 1)
    neg = jnp.float32(-1e30)
    band_bias = jnp.where((ki >= qi) & (ki <= qi + BAND), 0.0, neg)
    first_bias = jnp.where(ki >= BAND, 0.0, neg)
    nt = (((1,), (1,)), ((), ()))
    n_blocks = seq // BAND

    def rows(ref, start, dil):
        if dil == 1:
            return ref[0, pl.ds(start, BAND), :]
        return ref[0, pl.ds(start, BAND, stride=dil), :]

    for g, (_, dil) in enumerate(DILATED_GROUPS):
        nb = n_blocks // dil

        def block(n, u, last, g=g, dil=dil, nb=nb):
            r = n // nb
            i = n % nb
            cur = r + i * (BAND * dil)
            q = rows(q_ref, cur, dil).astype(BF16)
            k_cur = rows(k_ref, cur, dil).astype(BF16)
            v_cur = rows(v_ref, cur, dil).astype(BF16)
            if u % nb != 0:
                k_prev, v_prev = last
                bias = band_bias
            elif nb <= DIL_UNROLL:
                k_prev, v_prev = k_cur, v_cur
                bias = band_bias + first_bias
            else:
                prev = r + jnp.maximum(i - 1, 0) * (BAND * dil)
                k_prev = rows(k_ref, prev, dil).astype(BF16)
                v_prev = rows(v_ref, prev, dil).astype(BF16)
                bias = band_bias + jnp.where(i > 0, 0.0, 1.0) * first_bias
            kc = jnp.concatenate([k_prev, k_cur], axis=0)
            vc = jnp.concatenate([v_prev, v_cur], axis=0)
            o_t = jnp.zeros((BAND, LANES), F32)
            l_t = jnp.zeros((BAND, LANES), F32)
            for h in range(heads):
                qh = jnp.where(masks[h], q, jnp.zeros_like(q))
                s = lax.dot_general(qh, kc, nt, preferred_element_type=F32) + bias
                m = jnp.max(s, axis=-1, keepdims=True)
                p = jnp.exp(s - m)
                l = jnp.sum(p, axis=-1, keepdims=True)
                pv = jnp.dot(p.astype(BF16), vc, preferred_element_type=F32)
                o_t = jnp.where(masks[h], pv / l, o_t)
                l_t = jnp.where(masks[h], m + jnp.log(l), l_t)
            if dil == 1:
                os_ref[g, pl.ds(cur, BAND), :] = o_t
                ls_ref[g, pl.ds(cur, BAND), :] = l_t
            else:
                os_ref[g, pl.ds(cur, BAND, stride=dil), :] = o_t
                ls_ref[g, pl.ds(cur, BAND, stride=dil), :] = l_t
            return k_cur, v_cur

        def step(t, carry, block=block):
            last = None
            for u in range(DIL_UNROLL):
                last = block(t * DIL_UNROLL + u, u, last)
            return carry

        lax.fori_loop(0, n_blocks // DIL_UNROLL, step, 0)

    def mix(c, carry):
        r0 = pl.multiple_of(c * MIX_ROWS, MIX_ROWS)
        ls = [ls_ref[g, pl.ds(r0, MIX_ROWS), :] for g in range(len(DILATED_GROUPS))]
        m = functools.reduce(jnp.maximum, ls)
        ws = [jnp.exp(l - m) for l in ls]
        num = sum(os_ref[g, pl.ds(r0, MIX_ROWS), :] * w for g, w in enumerate(ws))
        o_ref[0, pl.ds(r0, MIX_ROWS), :] = (num / sum(ws)).astype(BF16)
        return carry

    lax.fori_loop(0, seq // MIX_ROWS, mix, 0)


def _dilated(qkv_a, bsz, seq):
    assert all(w == BAND * d for w, d in DILATED_GROUPS)
    n_blocks = seq // BAND
    assert seq % BAND == 0 and n_blocks % DIL_UNROLL == 0 and seq % MIX_ROWS == 0
    for _, d in DILATED_GROUPS:
        nb = n_blocks // d
        assert n_blocks % d == 0 and (nb % DIL_UNROLL == 0 or DIL_UNROLL % nb == 0)
    a3 = qkv_a.reshape(bsz, seq, QKV)
    hb = WIDTH // LANES
    n_groups = len(DILATED_GROUPS)
    spec = lambda off: pl.BlockSpec((1, seq, LANES), lambda b, h: (b, 0, off * hb + h))
    return pl.pallas_call(
        functools.partial(_dil_body, seq=seq),
        out_shape=jax.ShapeDtypeStruct((bsz, seq, WIDTH), BF16),
        grid=(bsz, hb),
        in_specs=[spec(0), spec(1), spec(2)],
        out_specs=pl.BlockSpec((1, seq, LANES), lambda b, h: (b, 0, h)),
        scratch_shapes=[pltpu.VMEM((n_groups, seq, LANES), F32),
                        pltpu.VMEM((n_groups, seq, LANES), F32)],
        compiler_params=_params("parallel", "parallel"),
        name="dilated",
    )(a3, a3, a3).reshape(bsz * seq, WIDTH)


def _stick_body(q_ref, k_ref, v_ref, o_ref, acc_ref, carry_ref, z_ref, a_ref, *, tq, tk):
    i = pl.program_id(2)
    heads = LANES // HEAD_DIM
    lane = lax.broadcasted_iota(I32, (1, LANES), 1)
    masks = [(lane >= h * HEAD_DIM) & (lane < (h + 1) * HEAD_DIM) for h in range(heads)]
    lr = lax.broadcasted_iota(I32, (tk, tk), 0)
    lc = lax.broadcasted_iota(I32, (tk, tk), 1)
    later = (lr > lc).astype(BF16)
    nt = (((1,), (1,)), ((), ()))
    q = q_ref[0]
    qh = [jnp.where(mk, q, jnp.zeros_like(q)) for mk in masks]
    acc_ref[...] = jnp.zeros_like(acc_ref)
    carry_ref[...] = jnp.zeros_like(carry_ref)
    nd = tq // tk
    n_before = i * nd

    def first_row(j):
        return pl.multiple_of(jnp.maximum(j, 0) * tk, tk)

    def logits(j, top=0):
        kj = k_ref[0, pl.ds(first_row(j), tk), :]
        return [lax.dot_general(qh[h][top:], kj, nt, preferred_element_type=F32)
                for h in range(heads)]

    def weights(zs, j, diag, top=0):
        rows = tq - top
        if diag:
            ti = lax.broadcasted_iota(I32, (rows, tk), 0)
            si = lax.broadcasted_iota(I32, (rows, tk), 1)
            causal = (first_row(j) + si) < (i * tq + top + ti)
        out = []
        for h, z in enumerate(zs):
            neg_abs = pltpu.bitcast(pltpu.bitcast(z, jnp.uint32) | jnp.uint32(0x80000000), F32)
            sp = jnp.maximum(z, 0.0) + jnp.log2(1.0 + jnp.exp2(neg_abs))
            if diag:
                sp = jnp.where(causal, sp, 0.0)
            c = carry_ref[h, top:, :]
            after = (jnp.dot(sp.astype(BF16), later, preferred_element_type=F32)
                     + jnp.concatenate([c] * (tk // LANES), axis=1))
            a = jnp.exp2(z - sp - after)
            if diag:
                a = jnp.where(causal, a, 0.0)
            carry_ref[h, top:, :] = jnp.broadcast_to((after + sp)[:, 0:1], (rows, LANES))
            out.append(a.astype(BF16))
        return out

    def values(a_s, j, top=0):
        vj = v_ref[0, pl.ds(first_row(j), tk), :]
        acc_ref[top:, :] += sum(
            jnp.dot(a_s[h], jnp.where(masks[h], vj, jnp.zeros_like(vj)), preferred_element_type=F32)
            for h in range(heads))

    def stage(slot, arrays, top=0):
        ref = z_ref if arrays[0].dtype == F32 else a_ref
        for h, arr in enumerate(arrays):
            ref[slot, h, top:, :] = arr

    def staged(ref, slot, top=0):
        return [ref[slot, h, top:, :] for h in range(heads)]

    upper, lower, top = n_before + 1, n_before, tk
    stage(0, logits(upper, top), top)
    stage(1, logits(lower))
    stage(0, weights(staged(z_ref, 0, top), upper, True, top), top)
    stage(0, logits(n_before - 1))
    stage(1, weights(staged(z_ref, 1), lower, True))
    values(staged(a_ref, 0, top), upper, top)

    def pipelined(t, p):
        j = n_before - 1 - t
        stage(1 - p, logits(j - 1))
        stage(p, weights(staged(z_ref, p), j, False))
        values(staged(a_ref, 1 - p), j + 1)

    def step(u, carry):
        for k in range(STICK_TRIP):
            pipelined(STICK_TRIP * u + k, k % 2)
        return carry

    lax.fori_loop(0, n_before // STICK_TRIP, step, 0)

    @pl.when(n_before % STICK_TRIP != 0)
    def _():
        for k in range(2):
            pipelined(n_before // STICK_TRIP * STICK_TRIP + k, k)

    values(staged(a_ref, 1), 0)

    o_ref[0] = acc_ref[...].astype(BF16)


def _stick(qkv_b, bsz, seq):
    tq, tk = STICK_ROWS, STICK_KEYS
    assert tq == 2 * tk and seq % tq == 0
    b3 = qkv_b.reshape(bsz, seq, QKV)
    hb = WIDTH // LANES
    heads = LANES // HEAD_DIM
    return pl.pallas_call(
        functools.partial(_stick_body, tq=tq, tk=tk),
        out_shape=jax.ShapeDtypeStruct((bsz, seq, WIDTH), BF16),
        grid=(bsz, hb, seq // tq),
        in_specs=[pl.BlockSpec((1, tq, LANES), lambda b, h, i: (b, i, h)),
                  pl.BlockSpec((1, seq, LANES), lambda b, h, i: (b, 0, hb + h)),
                  pl.BlockSpec((1, seq, LANES), lambda b, h, i: (b, 0, 2 * hb + h))],
        out_specs=pl.BlockSpec((1, tq, LANES), lambda b, h, i: (b, i, h)),
        scratch_shapes=[pltpu.VMEM((tq, LANES), F32),
                        pltpu.VMEM((heads, tq, LANES), F32),
                        pltpu.VMEM((2, heads, tq, tk), F32),
                        pltpu.VMEM((2, heads, tq, tk), BF16)],
        compiler_params=_params("parallel", "parallel", "parallel"),
        name="stick",
    )(b3, b3, b3).reshape(bsz * seq, WIDTH)


def _merge_body(x_ref, oa_ref, ob_ref,
                g1_ref, wg_ref, bg_ref, pa_ref, pb_ref, wo_ref, g2_ref, wrh_ref, wrl_ref, br_ref,
                x1_ref, h2_ref, slot_ref, gate_ref, cnt_ref, before_ref, carry_ref, prev_ref):
    tm, d = x_ref.shape
    step = pl.program_id(0)

    @pl.when(step == 0)
    def _():
        carry_ref[...] = jnp.zeros_like(carry_ref)
        prev_ref[...] = jnp.zeros_like(prev_ref)

    def dot(a, w_ref, lo, hi):
        return jnp.dot(a, w_ref[:, lo:hi], preferred_element_type=F32)

    routed = prev_ref[...]
    r_hi = routed.astype(BF16)
    r_lo = (routed - r_hi.astype(F32)).astype(BF16)
    work = (dot(r_hi, wrh_ref, 0, N_EXPERTS) + dot(r_lo, wrh_ref, 0, N_EXPERTS)
            + dot(r_hi, wrl_ref, 0, N_EXPERTS) + br_ref[...])
    eidx = lax.broadcasted_iota(I32, work.shape, 1)
    tops, hots = [], []
    for _ in range(TOP_K):
        mk = jnp.max(work, axis=-1, keepdims=True)
        ik = jnp.min(jnp.where(work == mk, eidx, N_EXPERTS), axis=-1, keepdims=True)
        hot = eidx == ik
        work = jnp.where(hot, -jnp.inf, work)
        tops.append(mk)
        hots.append(hot)
    exps = [jnp.exp(t - tops[0]) for t in tops]
    den = exps[0] + exps[1] + exps[2] + exps[3]
    gate_ref[...] = jnp.concatenate([e / den for e in exps], axis=1)
    real = jnp.where(step > 0, 1.0, 0.0)
    hots = [hot.astype(F32) * real for hot in hots]

    x = x_ref[...]
    h = _rms(x, g1_ref[...]).astype(BF16)
    gate_a = jax.nn.sigmoid(dot(h, wg_ref, 0, d) + bg_ref[:, :d])
    gate_b = jax.nn.sigmoid(dot(h, wg_ref, d, 2 * d) + bg_ref[:, d:])
    merged = (gate_a * dot(oa_ref[...], pa_ref, 0, d)
              + gate_b * dot(ob_ref[...], pb_ref, 0, d)).astype(BF16)
    x1 = x + jnp.concatenate([dot(merged, wo_ref, 0, d // 2), dot(merged, wo_ref, d // 2, d)],
                             axis=1)
    x1_ref[...] = x1
    h2 = _rms(x1, g2_ref[...])
    h2_ref[...] = h2
    prev_ref[...] = h2

    cnt = sum(hots)
    ri = lax.broadcasted_iota(I32, (tm, tm), 0)
    ci = lax.broadcasted_iota(I32, (tm, tm), 1)
    earlier = (ci < ri).astype(BF16)
    rank = jnp.dot(earlier, cnt.astype(BF16), preferred_element_type=F32)
    tile_cnt = jnp.sum(cnt, axis=0, keepdims=True)
    seg_len = jnp.ceil(tile_cnt * (1.0 / SEG_ALIGN)) * SEG_ALIGN
    ei = lax.broadcasted_iota(I32, (N_EXPERTS, N_EXPERTS), 0)
    ej = lax.broadcasted_iota(I32, (N_EXPERTS, N_EXPERTS), 1)
    seg_start = jnp.dot(jnp.broadcast_to(seg_len, (8, N_EXPERTS)).astype(BF16),
                        (ei < ej).astype(BF16), preferred_element_type=F32)[0:1]
    place = rank + seg_start
    slots = [jnp.sum(hot * place, axis=-1, keepdims=True) for hot in hots]
    slot_ref[...] = jnp.concatenate(slots, axis=1).astype(I32)
    before_ref[0] = carry_ref[...].astype(I32)
    total = carry_ref[...] + tile_cnt
    carry_ref[...] = total
    cnt_ref[...] = total.astype(I32)


def _merge(x2, o_a, o_b, g1, w_gate, b_gate, w_pa, w_pb, w_out, g2, w_r, b_r):
    n, d = x2.shape
    w_r_hi = w_r.astype(BF16)
    w_r_lo = (w_r - w_r_hi.astype(F32)).astype(BF16)
    tm = MERGE_ROWS
    n_tiles = n // tm
    row = lambda w: pl.BlockSpec((tm, w), lambda i: (jnp.minimum(i, n_tiles - 1), 0))
    late = lambda w: pl.BlockSpec((tm, w), lambda i: (jnp.maximum(i - 1, 0), 0))
    outs = pl.pallas_call(
        _merge_body,
        out_shape=(jax.ShapeDtypeStruct((n, d), F32),
                   jax.ShapeDtypeStruct((n, d), F32),
                   jax.ShapeDtypeStruct((n, TOP_K), I32),
                   jax.ShapeDtypeStruct((n, TOP_K), F32),
                   jax.ShapeDtypeStruct((1, N_EXPERTS), I32),
                   jax.ShapeDtypeStruct((n // tm, 1, N_EXPERTS), I32)),
        grid=(n_tiles + 1,),
        in_specs=[row(d), row(WIDTH), row(WIDTH),
                  _const_spec((1, d)), _const_spec((d, 2 * d)), _const_spec((1, 2 * d)),
                  _const_spec((WIDTH, d)), _const_spec((WIDTH, d)), _const_spec((d, d)),
                  _const_spec((1, d)), _const_spec((d, N_EXPERTS)), _const_spec((d, N_EXPERTS)),
                  _const_spec((1, N_EXPERTS))],
        out_specs=(row(d), row(d), late(TOP_K), late(TOP_K),
                   pl.BlockSpec((1, N_EXPERTS), lambda i: (0, 0)),
                   pl.BlockSpec((1, 1, N_EXPERTS), lambda i: (jnp.maximum(i - 1, 0), 0, 0))),
        scratch_shapes=[pltpu.VMEM((1, N_EXPERTS), F32), pltpu.VMEM((tm, d), F32)],
        compiler_params=_params("arbitrary"),
        name="merge",
    )(x2, o_a, o_b, g1, w_gate, b_gate, w_pa, w_pb, w_out, g2, w_r_hi, w_r_lo, b_r)
    return outs


def _segment_copies(len_ref, src_ref, dst_ref, tile, make_copy):
    for e in range(N_EXPERTS):
        base = tile * N_EXPERTS + e
        length, src, dst = len_ref[base], src_ref[base], dst_ref[base]
        size = MOE_TILE
        while size >= SEG_ALIGN:
            done = (length // (2 * size)) * (2 * size)

            @pl.when((length & size) != 0)
            def _(size=size, done=done):
                make_copy(pl.multiple_of(src + done, SEG_ALIGN),
                          pl.multiple_of(dst + done, SEG_ALIGN), size).start()

            size //= 2


def _segment_wait(len_ref, tile, make_copy):
    total = len_ref[tile * N_EXPERTS]
    for e in range(1, N_EXPERTS):
        total = total + len_ref[tile * N_EXPERTS + e]
    size = 1 << (STAGE_ROWS.bit_length() - 1)
    while size >= SEG_ALIGN:
        @pl.when((total & size) != 0)
        def _(size=size):
            make_copy(0, 0, size).wait()

        size //= 2


def _dispatch_body(len_ref, src_ref, dst_ref, pad_ref, nb_ref, slot_ref, h_ref,
                   xs_hbm, stage_ref, zero_ref, sem, zero_sem):
    tm = h_ref.shape[0]
    n_blocks = xs_hbm.shape[0] // EXPERT_ROWS
    tile = pl.program_id(0)
    buf = tile % 2

    @pl.when(tile == 0)
    def _():
        zero_ref[...] = jnp.zeros_like(zero_ref)
        for e in range(N_EXPERTS):
            r0 = pl.multiple_of(pad_ref[e], EXPERT_ROWS)
            pltpu.make_async_copy(zero_ref, xs_hbm.at[pl.ds(r0, EXPERT_ROWS)], zero_sem).start()
        for e in range(N_EXPERTS):
            pltpu.make_async_copy(zero_ref, xs_hbm.at[pl.ds(0, EXPERT_ROWS)], zero_sem).wait()

        def fill(b, carry):
            r0 = pl.multiple_of(b * EXPERT_ROWS, EXPERT_ROWS)
            cp = pltpu.make_async_copy(zero_ref, xs_hbm.at[pl.ds(r0, EXPERT_ROWS)], zero_sem)
            cp.start()
            cp.wait()
            return carry

        lax.fori_loop(nb_ref[0], n_blocks, fill, 0)

    slots = slot_ref[...]
    col = lax.broadcasted_iota(I32, (tm, STAGE_ROWS), 1)
    hit = col == slots[:, 0:1]
    for k in range(1, TOP_K):
        hit = hit | (col == slots[:, k:k + 1])
    pick = jnp.where(hit, 1.0, 0.0).astype(BF16)
    stage_ref[buf] = lax.dot_general(pick, h_ref[...].astype(BF16), (((0,), (0,)), ((), ())),
                                     preferred_element_type=F32)

    def copy_from(b):
        return lambda src, dst, size: pltpu.make_async_copy(
            stage_ref.at[b, pl.ds(src, size)], xs_hbm.at[pl.ds(dst, size)], sem)

    @pl.when(tile > 0)
    def _():
        _segment_wait(len_ref, tile - 1, copy_from(1 - buf))

    _segment_copies(len_ref, src_ref, dst_ref, tile, copy_from(buf))

    @pl.when(tile == pl.num_programs(0) - 1)
    def _():
        _segment_wait(len_ref, tile, copy_from(buf))


def _dispatch(seg_len, seg_src, seg_dst, pad_start, n_used, slots, h2, n_rows):
    n, d = h2.shape
    tm = MOE_TILE
    tiled = lambda w: pl.BlockSpec((tm, w), lambda i, *_: (i, 0))
    return pl.pallas_call(
        _dispatch_body,
        out_shape=jax.ShapeDtypeStruct((n_rows, d), F32),
        grid_spec=pltpu.PrefetchScalarGridSpec(
            num_scalar_prefetch=5,
            grid=(n // tm,),
            in_specs=[tiled(TOP_K), tiled(d)],
            out_specs=pl.BlockSpec(memory_space=pl.ANY),
            scratch_shapes=[pltpu.VMEM((2, STAGE_ROWS, d), F32),
                            pltpu.VMEM((EXPERT_ROWS, d), F32),
                            pltpu.SemaphoreType.DMA,
                            pltpu.SemaphoreType.DMA]),
        compiler_params=_params("arbitrary"),
        name="dispatch",
    )(seg_len, seg_src, seg_dst, pad_start, n_used, slots, h2)


def _expert_body(be_ref, nb_ref, x_ref, w1_ref, b1_ref, w2_ref, b2_ref, y_ref, w1b_ref, w2b_ref):
    f = w2_ref.shape[1]
    b = pl.program_id(0)

    @pl.when((b == 0) | (be_ref[b] != be_ref[jnp.maximum(b - 1, 0)]))
    def _():
        w1b_ref[...] = w1_ref[0].astype(BF16)
        w2b_ref[...] = w2_ref[0].astype(BF16)

    @pl.when(b < nb_ref[0])
    def _():
        xb = x_ref[...].astype(BF16)
        hid = jnp.dot(xb, w1b_ref[...], preferred_element_type=F32) + b1_ref[0]
        glu = jnp.minimum(hid[:, :f], SWIGLU_LIMIT)
        lin = jnp.clip(hid[:, f:], -SWIGLU_LIMIT, SWIGLU_LIMIT)
        act = glu * jax.nn.sigmoid(SWIGLU_ALPHA * glu) * (lin + 1.0)
        y_ref[...] = jnp.dot(act.astype(BF16), w2b_ref[...], preferred_element_type=F32) + b2_ref[0]

    @pl.when(b >= nb_ref[0])
    def _():
        y_ref[...] = jnp.zeros_like(y_ref)


def _experts(block_e, n_used, xs, w1, b1, w2, b2, n_blocks):
    d = xs.shape[1]
    f = w2.shape[1]
    br = EXPERT_ROWS
    blk = lambda b, be, nb: (jnp.minimum(b, nb[0] - 1), 0)
    per_e = lambda b, be, nb: (be[b], 0, 0)
    return pl.pallas_call(
        _expert_body,
        out_shape=jax.ShapeDtypeStruct((n_blocks * br, d), F32),
        grid_spec=pltpu.PrefetchScalarGridSpec(
            num_scalar_prefetch=2,
            grid=(n_blocks,),
            in_specs=[pl.BlockSpec((br, d), blk),
                      pl.BlockSpec((1, d, 2 * f), per_e),
                      pl.BlockSpec((1, 1, 2 * f), per_e),
                      pl.BlockSpec((1, f, d), per_e),
                      pl.BlockSpec((1, 1, d), per_e)],
            out_specs=pl.BlockSpec((br, d), lambda b, be, nb: (b, 0)),
            scratch_shapes=[pltpu.VMEM((d, 2 * f), BF16), pltpu.VMEM((f, d), BF16)]),
        compiler_params=_params("arbitrary"),
        name="experts",
    )(block_e, n_used, xs, w1, b1, w2, b2)


def _combine_body(len_ref, src_ref, dst_ref, slot_ref, x1_ref, gate_ref, g_ref,
                  y_hbm, o_ref, stage_ref, sem, *, final_norm):
    tm = x1_ref.shape[0]
    tile = pl.program_id(0)
    buf = tile % 2

    def copy_into(b):
        return lambda src, dst, size: pltpu.make_async_copy(
            y_hbm.at[pl.ds(dst, size)], stage_ref.at[b, pl.ds(src, size)], sem)

    @pl.when(tile == 0)
    def _():
        stage_ref[...] = jnp.zeros_like(stage_ref)
        _segment_copies(len_ref, src_ref, dst_ref, tile, copy_into(0))

    _segment_wait(len_ref, tile, copy_into(buf))

    @pl.when(tile + 1 < pl.num_programs(0))
    def _():
        _segment_copies(len_ref, src_ref, dst_ref, tile + 1, copy_into(1 - buf))

    slots = slot_ref[...]
    gate = gate_ref[...]
    col = lax.broadcasted_iota(I32, (tm, STAGE_ROWS), 1)
    weight = jnp.zeros((tm, STAGE_ROWS), F32)
    for k in range(TOP_K):
        weight = jnp.where(col == slots[:, k:k + 1], gate[:, k:k + 1], weight)
    moe = jnp.dot(weight.astype(BF16), stage_ref[buf].astype(BF16), preferred_element_type=F32)
    acc = x1_ref[...] + moe
    o_ref[...] = _rms(acc, g_ref[...]) if final_norm else acc


def _combine(seg_len, seg_src, seg_dst, slots, x1, gates, g_f, y, final_norm):
    n, d = x1.shape
    tm = MOE_TILE
    tiled = lambda w: pl.BlockSpec((tm, w), lambda i, *_: (i, 0))
    return pl.pallas_call(
        functools.partial(_combine_body, final_norm=final_norm),
        out_shape=jax.ShapeDtypeStruct((n, d), F32),
        grid_spec=pltpu.PrefetchScalarGridSpec(
            num_scalar_prefetch=3,
            grid=(n // tm,),
            in_specs=[tiled(TOP_K), tiled(d), tiled(TOP_K),
                      pl.BlockSpec((1, d), lambda i, *_: (0, 0)),
                      pl.BlockSpec(memory_space=pl.ANY)],
            out_specs=tiled(d),
            scratch_shapes=[pltpu.VMEM((2, STAGE_ROWS, d), F32),
                            pltpu.SemaphoreType.DMA]),
        compiler_params=_params("arbitrary"),
        name="combine",
    )(seg_len, seg_src, seg_dst, slots, x1, gates, g_f, y)


def _layer(x2, bsz, seq, p):
    n, d = x2.shape
    qkv_a, qkv_b = _proj(x2, p["g1"], p["w_in"], seq)
    o_a = _dilated(qkv_a, bsz, seq)
    o_b = _stick(qkv_b, bsz, seq)
    x1, h2, slots, gates, counts, before = _merge(
        x2, o_a, o_b, p["g1"], p["w_gate"], p["b_gate"], p["w_pa"], p["w_pb"], p["w_out"],
        p["g2"], p["w_r"], p["b_r"])

    br = EXPERT_ROWS
    n_tiles = n // MOE_TILE
    before = before.reshape(n_tiles, N_EXPERTS)
    seg_cnt = jnp.concatenate([before[1:], counts], axis=0) - before
    seg_len = (seg_cnt + SEG_ALIGN - 1) // SEG_ALIGN * SEG_ALIGN
    seg_src = jnp.cumsum(seg_len, axis=1) - seg_len
    rows_e = jnp.sum(seg_len, axis=0)
    padded = (rows_e + br - 1) // br * br
    ends = jnp.cumsum(padded)
    starts = ends - padded
    seg_dst = starts[None, :] + jnp.cumsum(seg_len, axis=0) - seg_len
    n_blocks = (n * TOP_K + n_tiles * N_EXPERTS * (SEG_ALIGN - 1)) // br + N_EXPERTS
    block_row = jnp.arange(n_blocks, dtype=I32) * br
    block_e = jnp.minimum(jnp.sum(ends[None, :] <= block_row[:, None], axis=1),
                          N_EXPERTS - 1).astype(I32)
    n_used = (ends[-1:] // br).astype(I32)
    pad_start = (starts + rows_e // br * br).astype(I32)
    tables = [t.astype(I32).reshape(n_tiles * N_EXPERTS) for t in (seg_len, seg_src, seg_dst)]

    xs = _dispatch(*tables, pad_start, n_used, slots, h2, n_blocks * br)
    y = _experts(block_e, n_used, xs, p["w1"], p["b1"], p["w2"], p["b2"], n_blocks)
    return (*tables, slots), x1, gates, y


def kernel(x, norm1_g, w_in, w_proj_a, w_proj_b, w_gate, b_gate, w_out, norm2_g,
           w_router, b_router, w1, b1, w2, b2, norm_f_g):
    bsz, seq, d = x.shape
    depth = w_in.shape[0]
    x2 = x.reshape(bsz * seq, d)
    for l in range(depth):
        p = dict(
            g1=norm1_g[l].reshape(1, d), w_in=w_in[l].astype(BF16),
            w_gate=w_gate[l].astype(BF16), b_gate=b_gate[l].reshape(1, 2 * d),
            w_pa=w_proj_a[l].astype(BF16), w_pb=w_proj_b[l].astype(BF16),
            w_out=w_out[l].astype(BF16), g2=norm2_g[l].reshape(1, d),
            w_r=w_router[l], b_r=b_router[l].reshape(1, N_EXPERTS),
            w1=w1[l], b1=b1[l].reshape(N_EXPERTS, 1, -1),
            w2=w2[l], b2=b2[l].reshape(N_EXPERTS, 1, -1))
        route, x1, gates, y = _layer(x2, bsz, seq, p)
        x2 = _combine(*route, x1, gates, norm_f_g.reshape(1, d), y, final_norm=(l == depth - 1))
    return x2.reshape(bsz, seq, d)
```

```python
import functools

import jax
import jax.numpy as jnp
from jax import lax
from jax.experimental import pallas as pl
from jax.experimental.pallas import tpu as pltpu

F32 = jnp.float32
BF16 = jnp.bfloat16
I32 = jnp.int32

HEAD_DIM = 64
WIDTH = 512
QKV = 3 * WIDTH
DILATED_GROUPS = ((128, 1), (512, 4), (2048, 16))
BAND = 128
ROPE_THETA = 500000.0
ROT_DIM = HEAD_DIM // 4
N_EXPERTS = 32
TOP_K = 4
SWIGLU_ALPHA = 1.702
SWIGLU_LIMIT = 7.0
NORM_EPS = 1e-5

LANES = 128
VMEM_LIMIT = 56 * 1024 * 1024

LOG2E = 1.4426950408889634

PROJ_ROWS = 512
MERGE_ROWS = 512
STICK_ROWS = 512
STICK_KEYS = 256
STICK_TRIP = 4
EXPERT_ROWS = 512
MOE_TILE = MERGE_ROWS
SEG_ALIGN = 8
STAGE_ROWS = MOE_TILE * TOP_K + N_EXPERTS * SEG_ALIGN
DIL_UNROLL = 32
MIX_ROWS = 512


def _rms(x, g):
    return x * lax.rsqrt(jnp.mean(x * x, axis=-1, keepdims=True) + NORM_EPS) * g


def _params(*sem):
    return pltpu.CompilerParams(dimension_semantics=sem, vmem_limit_bytes=VMEM_LIMIT)


def _const_spec(shape):
    return pl.BlockSpec(shape, lambda *_: (0,) * len(shape), pipeline_mode=pl.Buffered(1))


def _proj_body(x_ref, g_ref, w_ref, cos_ref, sa_ref, sb_ref, a_ref, b_ref):
    h = _rms(x_ref[...], g_ref[...]).astype(BF16)
    rep = WIDTH // LANES
    cos = jnp.concatenate([cos_ref[...]] * rep, axis=1)
    sa = jnp.concatenate([sa_ref[...]] * rep, axis=1)
    sb = jnp.concatenate([sb_ref[...]] * rep, axis=1)
    half = ROT_DIM // 2
    for c in range(6):
        acc = jnp.dot(h, w_ref[:, c * WIDTH:(c + 1) * WIDTH], preferred_element_type=F32)
        if c in (0, 1):
            acc = (acc * cos + pltpu.roll(acc, WIDTH - half, 1) * sa
                   + pltpu.roll(acc, half, 1) * sb)
        if c == 0:
            acc = acc * (HEAD_DIM ** -0.5)
        if c == 3:
            acc = acc * (HEAD_DIM ** -0.5 * LOG2E)
        dst = a_ref if c < 3 else b_ref
        dst[:, (c % 3) * WIDTH:(c % 3 + 1) * WIDTH] = acc.astype(dst.dtype)


def _rotary_tables(seq):
    half = ROT_DIM // 2
    inv_freq = jnp.float32(ROPE_THETA) ** (-jnp.arange(0, ROT_DIM, 2, dtype=F32) / ROT_DIM)
    ang = jnp.arange(seq, dtype=jnp.int32).astype(F32)[:, None] * inv_freq[None, :]
    cos, sin = jnp.cos(ang), jnp.sin(ang)
    ones = jnp.ones((seq, HEAD_DIM - ROT_DIM), F32)
    zeros_h = jnp.zeros((seq, half), F32)
    zeros_r = jnp.zeros((seq, HEAD_DIM - ROT_DIM), F32)
    cos_t = jnp.concatenate([cos, cos, ones], axis=1)
    sa_t = jnp.concatenate([-sin, zeros_h, zeros_r], axis=1)
    sb_t = jnp.concatenate([zeros_h, sin, zeros_r], axis=1)
    rep = LANES // HEAD_DIM
    return tuple(jnp.tile(t, (1, rep)) for t in (cos_t, sa_t, sb_t))


def _proj(x2, g1, w_in, seq):
    n, d = x2.shape
    tm = PROJ_ROWS
    tabs = _rotary_tables(seq)
    per_seq = seq // tm
    tab_spec = pl.BlockSpec((tm, LANES), lambda i: (i % per_seq, 0))
    return pl.pallas_call(
        _proj_body,
        out_shape=(jax.ShapeDtypeStruct((n, QKV), F32), jax.ShapeDtypeStruct((n, QKV), BF16)),
        grid=(n // tm,),
        in_specs=[pl.BlockSpec((tm, d), lambda i: (i, 0)),
                  _const_spec((1, d)),
                  _const_spec((d, 2 * QKV)),
                  tab_spec, tab_spec, tab_spec],
        out_specs=(pl.BlockSpec((tm, QKV), lambda i: (i, 0)),
                   pl.BlockSpec((tm, QKV), lambda i: (i, 0))),
        compiler_params=_params("parallel"),
        name="proj",
    )(x2, g1, w_in, *tabs)


def _dil_body(q_ref, k_ref, v_ref, o_ref, os_ref, ls_ref, *, seq):
    heads = LANES // HEAD_DIM
    lane = lax.broadcasted_iota(I32, (1, LANES), 1)
    masks = [(lane >= h * HEAD_DIM) & (lane < (h + 1) * HEAD_DIM) for h in range(heads)]
    qi = lax.broadcasted_iota(I32, (BAND, 2 * BAND), 0)
    ki = lax.broadcasted_iota(I32, (BAND, 2 * BAND), 1)
    neg = jnp.float32(-1e30)
    band_bias = jnp.where((ki >= qi) & (ki <= qi + BAND), 0.0, neg)
    first_bias = jnp.where(ki >= BAND, 0.0, neg)
    nt = (((1,), (1,)), ((), ()))
    n_blocks = seq // BAND

    def rows(ref, start, dil):
        if dil == 1:
            return ref[0, pl.ds(start, BAND), :]
        return ref[0, pl.ds(start, BAND, stride=dil), :]

    for g, (_, dil) in enumerate(DILATED_GROUPS):
        nb = n_blocks // dil

        def block(n, u, last, g=g, dil=dil, nb=nb):
            r = n // nb
            i = n % nb
            cur = r + i * (BAND * dil)
            q = rows(q_ref, cur, dil).astype(BF16)
            k_cur = rows(k_ref, cur, dil).astype(BF16)
            v_cur = rows(v_ref, cur, dil).astype(BF16)
            if u % nb != 0:
                k_prev, v_prev = last
                bias = band_bias
            elif nb <= DIL_UNROLL:
                k_prev, v_prev = k_cur, v_cur
                bias = band_bias + first_bias
            else:
                prev = r + jnp.maximum(i - 1, 0) * (BAND * dil)
                k_prev = rows(k_ref, prev, dil).astype(BF16)
                v_prev = rows(v_ref, prev, dil).astype(BF16)
                bias = band_bias + jnp.where(i > 0, 0.0, 1.0) * first_bias
            kc = jnp.concatenate([k_prev, k_cur], axis=0)
            vc = jnp.concatenate([v_prev, v_cur], axis=0)
            o_t = jnp.zeros((BAND, LANES), F32)
            l_t = jnp.zeros((BAND, LANES), F32)
            for h in range(heads):
                qh = jnp.where(masks[h], q, jnp.zeros_like(q))
                s = lax.dot_general(qh, kc, nt, preferred_element_type=F32) + bias
                m = jnp.max(s, axis=-1, keepdims=True)
                p = jnp.exp(s - m)
                l = jnp.sum(p, axis=-1, keepdims=True)
                pv = jnp.dot(p.astype(BF16), vc, preferred_element_type=F32)
                o_t = jnp.where(masks[h], pv / l, o_t)
                l_t = jnp.where(masks[h], m + jnp.log(l), l_t)
            if dil == 1:
                os_ref[g, pl.ds(cur, BAND), :] = o_t
                ls_ref[g, pl.ds(cur, BAND), :] = l_t
            else:
                os_ref[g, pl.ds(cur, BAND, stride=dil), :] = o_t
                ls_ref[g, pl.ds(cur, BAND, stride=dil), :] = l_t
            return k_cur, v_cur

        def step(t, carry, block=block):
            last = None
            for u in range(DIL_UNROLL):
                last = block(t * DIL_UNROLL + u, u, last)
            return carry

        lax.fori_loop(0, n_blocks // DIL_UNROLL, step, 0)

    def mix(c, carry):
        r0 = pl.multiple_of(c * MIX_ROWS, MIX_ROWS)
        ls = [ls_ref[g, pl.ds(r0, MIX_ROWS), :] for g in range(len(DILATED_GROUPS))]
        m = functools.reduce(jnp.maximum, ls)
        ws = [jnp.exp(l - m) for l in ls]
        num = sum(os_ref[g, pl.ds(r0, MIX_ROWS), :] * w for g, w in enumerate(ws))
        o_ref[0, pl.ds(r0, MIX_ROWS), :] = (num / sum(ws)).astype(BF16)
        return carry

    lax.fori_loop(0, seq // MIX_ROWS, mix, 0)


def _dilated(qkv_a, bsz, seq):
    assert all(w == BAND * d for w, d in DILATED_GROUPS)
    n_blocks = seq // BAND
    assert seq % BAND == 0 and n_blocks % DIL_UNROLL == 0 and seq % MIX_ROWS == 0
    for _, d in DILATED_GROUPS:
        nb = n_blocks // d
        assert n_blocks % d == 0 and (nb % DIL_UNROLL == 0 or DIL_UNROLL % nb == 0)
    a3 = qkv_a.reshape(bsz, seq, QKV)
    hb = WIDTH // LANES
    n_groups = len(DILATED_GROUPS)
    spec = lambda off: pl.BlockSpec((1, seq, LANES), lambda b, h: (b, 0, off * hb + h))
    return pl.pallas_call(
        functools.partial(_dil_body, seq=seq),
        out_shape=jax.ShapeDtypeStruct((bsz, seq, WIDTH), BF16),
        grid=(bsz, hb),
        in_specs=[spec(0), spec(1), spec(2)],
        out_specs=pl.BlockSpec((1, seq, LANES), lambda b, h: (b, 0, h)),
        scratch_shapes=[pltpu.VMEM((n_groups, seq, LANES), F32),
                        pltpu.VMEM((n_groups, seq, LANES), F32)],
        compiler_params=_params("parallel", "parallel"),
        name="dilated",
    )(a3, a3, a3).reshape(bsz * seq, WIDTH)


def _stick_body(q_ref, k_ref, v_ref, o_ref, acc_ref, carry_ref, z_ref, a_ref, *, tq, tk):
    i = pl.program_id(2)
    heads = LANES // HEAD_DIM
    lane = lax.broadcasted_iota(I32, (1, LANES), 1)
    masks = [(lane >= h * HEAD_DIM) & (lane < (h + 1) * HEAD_DIM) for h in range(heads)]
    lr = lax.broadcasted_iota(I32, (tk, tk), 0)
    lc = lax.broadcasted_iota(I32, (tk, tk), 1)
    later = (lr > lc).astype(BF16)
    nt = (((1,), (1,)), ((), ()))
    q = q_ref[0]
    qh = [jnp.where(mk, q, jnp.zeros_like(q)) for mk in masks]
    acc_ref[...] = jnp.zeros_like(acc_ref)
    carry_ref[...] = jnp.zeros_like(carry_ref)
    nd = tq // tk
    n_before = i * nd

    def first_row(j):
        return pl.multiple_of(jnp.maximum(j, 0) * tk, tk)

    def logits(j, top=0):
        kj = k_ref[0, pl.ds(first_row(j), tk), :]
        return [lax.dot_general(qh[h][top:], kj, nt, preferred_element_type=F32)
                for h in range(heads)]

    def weights(zs, j, diag, top=0):
        rows = tq - top
        if diag:
            ti = lax.broadcasted_iota(I32, (rows, tk), 0)
            si = lax.broadcasted_iota(I32, (rows, tk), 1)
            causal = (first_row(j) + si) < (i * tq + top + ti)
        out = []
        for h, z in enumerate(zs):
            neg_abs = pltpu.bitcast(pltpu.bitcast(z, jnp.uint32) | jnp.uint32(0x80000000), F32)
            sp = jnp.maximum(z, 0.0) + jnp.log2(1.0 + jnp.exp2(neg_abs))
            if diag:
                sp = jnp.where(causal, sp, 0.0)
            c = carry_ref[h, top:, :]
            after = (jnp.dot(sp.astype(BF16), later, preferred_element_type=F32)
                     + jnp.concatenate([c] * (tk // LANES), axis=1))
            a = jnp.exp2(z - sp - after)
            if diag:
                a = jnp.where(causal, a, 0.0)
            carry_ref[h, top:, :] = jnp.broadcast_to((after + sp)[:, 0:1], (rows, LANES))
            out.append(a.astype(BF16))
        return out

    def values(a_s, j, top=0):
        vj = v_ref[0, pl.ds(first_row(j), tk), :]
        acc_ref[top:, :] += sum(
            jnp.dot(a_s[h], jnp.where(masks[h], vj, jnp.zeros_like(vj)), preferred_element_type=F32)
            for h in range(heads))

    def stage(slot, arrays, top=0):
        ref = z_ref if arrays[0].dtype == F32 else a_ref
        for h, arr in enumerate(arrays):
            ref[slot, h, top:, :] = arr

    def staged(ref, slot, top=0):
        return [ref[slot, h, top:, :] for h in range(heads)]

    upper, lower, top = n_before + 1, n_before, tk
    stage(0, logits(upper, top), top)
    stage(1, logits(lower))
    stage(0, weights(staged(z_ref, 0, top), upper, True, top), top)
    stage(0, logits(n_before - 1))
    stage(1, weights(staged(z_ref, 1), lower, True))
    values(staged(a_ref, 0, top), upper, top)

    def pipelined(t, p):
        j = n_before - 1 - t
        stage(1 - p, logits(j - 1))
        stage(p, weights(staged(z_ref, p), j, False))
        values(staged(a_ref, 1 - p), j + 1)

    def step(u, carry):
        for k in range(STICK_TRIP):
            pipelined(STICK_TRIP * u + k, k % 2)
        return carry

    lax.fori_loop(0, n_before // STICK_TRIP, step, 0)

    @pl.when(n_before % STICK_TRIP != 0)
    def _():
        for k in range(2):
            pipelined(n_before // STICK_TRIP * STICK_TRIP + k, k)

    values(staged(a_ref, 1), 0)

    o_ref[0] = acc_ref[...].astype(BF16)


def _stick(qkv_b, bsz, seq):
    tq, tk = STICK_ROWS, STICK_KEYS
    assert tq == 2 * tk and seq % tq == 0
    b3 = qkv_b.reshape(bsz, seq, QKV)
    hb = WIDTH // LANES
    heads = LANES // HEAD_DIM
    return pl.pallas_call(
        functools.partial(_stick_body, tq=tq, tk=tk),
        out_shape=jax.ShapeDtypeStruct((bsz, seq, WIDTH), BF16),
        grid=(bsz, hb, seq // tq),
        in_specs=[pl.BlockSpec((1, tq, LANES), lambda b, h, i: (b, i, h)),
                  pl.BlockSpec((1, seq, LANES), lambda b, h, i: (b, 0, hb + h)),
                  pl.BlockSpec((1, seq, LANES), lambda b, h, i: (b, 0, 2 * hb + h))],
        out_specs=pl.BlockSpec((1, tq, LANES), lambda b, h, i: (b, i, h)),
        scratch_shapes=[pltpu.VMEM((tq, LANES), F32),
                        pltpu.VMEM((heads, tq, LANES), F32),
                        pltpu.VMEM((2, heads, tq, tk), F32),
                        pltpu.VMEM((2, heads, tq, tk), BF16)],
        compiler_params=_params("parallel", "parallel", "parallel"),
        name="stick",
    )(b3, b3, b3).reshape(bsz * seq, WIDTH)


def _merge_body(x_ref, oa_ref, ob_ref,
                g1_ref, wg_ref, bg_ref, pa_ref, pb_ref, wo_ref, g2_ref, wrh_ref, wrl_ref, br_ref,
                x1_ref, h2_ref, slot_ref, gate_ref, cnt_ref, before_ref, carry_ref, prev_ref):
    tm, d = x_ref.shape
    step = pl.program_id(0)

    @pl.when(step == 0)
    def _():
        carry_ref[...] = jnp.zeros_like(carry_ref)
        prev_ref[...] = jnp.zeros_like(prev_ref)

    def dot(a, w_ref, lo, hi):
        return jnp.dot(a, w_ref[:, lo:hi], preferred_element_type=F32)

    routed = prev_ref[...]
    r_hi = routed.astype(BF16)
    r_lo = (routed - r_hi.astype(F32)).astype(BF16)
    work = (dot(r_hi, wrh_ref, 0, N_EXPERTS) + dot(r_lo, wrh_ref, 0, N_EXPERTS)
            + dot(r_hi, wrl_ref, 0, N_EXPERTS) + br_ref[...])
    eidx = lax.broadcasted_iota(I32, work.shape, 1)
    tops, hots = [], []
    for _ in range(TOP_K):
        mk = jnp.max(work, axis=-1, keepdims=True)
        ik = jnp.min(jnp.where(work == mk, eidx, N_EXPERTS), axis=-1, keepdims=True)
        hot = eidx == ik
        work = jnp.where(hot, -jnp.inf, work)
        tops.append(mk)
        hots.append(hot)
    exps = [jnp.exp(t - tops[0]) for t in tops]
    den = exps[0] + exps[1] + exps[2] + exps[3]
    gate_ref[...] = jnp.concatenate([e / den for e in exps], axis=1)
    real = jnp.where(step > 0, 1.0, 0.0)
    hots = [hot.astype(F32) * real for hot in hots]

    x = x_ref[...]
    h = _rms(x, g1_ref[...]).astype(BF16)
    gate_a = jax.nn.sigmoid(dot(h, wg_ref, 0, d) + bg_ref[:, :d])
    gate_b = jax.nn.sigmoid(dot(h, wg_ref, d, 2 * d) + bg_ref[:, d:])
    merged = (gate_a * dot(oa_ref[...], pa_ref, 0, d)
              + gate_b * dot(ob_ref[...], pb_ref, 0, d)).astype(BF16)
    x1 = x + jnp.concatenate([dot(merged, wo_ref, 0, d // 2), dot(merged, wo_ref, d // 2, d)],
                             axis=1)
    x1_ref[...] = x1
    h2 = _rms(x1, g2_ref[...])
    h2_ref[...] = h2
    prev_ref[...] = h2

    cnt = sum(hots)
    ri = lax.broadcasted_iota(I32, (tm, tm), 0)
    ci = lax.broadcasted_iota(I32, (tm, tm), 1)
    earlier = (ci < ri).astype(BF16)
    rank = jnp.dot(earlier, cnt.astype(BF16), preferred_element_type=F32)
    tile_cnt = jnp.sum(cnt, axis=0, keepdims=True)
    seg_len = jnp.ceil(tile_cnt * (1.0 / SEG_ALIGN)) * SEG_ALIGN
    ei = lax.broadcasted_iota(I32, (N_EXPERTS, N_EXPERTS), 0)
    ej = lax.broadcasted_iota(I32, (N_EXPERTS, N_EXPERTS), 1)
    seg_start = jnp.dot(jnp.broadcast_to(seg_len, (8, N_EXPERTS)).astype(BF16),
                        (ei < ej).astype(BF16), preferred_element_type=F32)[0:1]
    place = rank + seg_start
    slots = [jnp.sum(hot * place, axis=-1, keepdims=True) for hot in hots]
    slot_ref[...] = jnp.concatenate(slots, axis=1).astype(I32)
    before_ref[0] = carry_ref[...].astype(I32)
    total = carry_ref[...] + tile_cnt
    carry_ref[...] = total
    cnt_ref[...] = total.astype(I32)


def _merge(x2, o_a, o_b, g1, w_gate, b_gate, w_pa, w_pb, w_out, g2, w_r, b_r):
    n, d = x2.shape
    w_r_hi = w_r.astype(BF16)
    w_r_lo = (w_r - w_r_hi.astype(F32)).astype(BF16)
    tm = MERGE_ROWS
    n_tiles = n // tm
    row = lambda w: pl.BlockSpec((tm, w), lambda i: (jnp.minimum(i, n_tiles - 1), 0))
    late = lambda w: pl.BlockSpec((tm, w), lambda i: (jnp.maximum(i - 1, 0), 0))
    outs = pl.pallas_call(
        _merge_body,
        out_shape=(jax.ShapeDtypeStruct((n, d), F32),
                   jax.ShapeDtypeStruct((n, d), F32),
                   jax.ShapeDtypeStruct((n, TOP_K), I32),
                   jax.ShapeDtypeStruct((n, TOP_K), F32),
                   jax.ShapeDtypeStruct((1, N_EXPERTS), I32),
                   jax.ShapeDtypeStruct((n // tm, 1, N_EXPERTS), I32)),
        grid=(n_tiles + 1,),
        in_specs=[row(d), row(WIDTH), row(WIDTH),
                  _const_spec((1, d)), _const_spec((d, 2 * d)), _const_spec((1, 2 * d)),
                  _const_spec((WIDTH, d)), _const_spec((WIDTH, d)), _const_spec((d, d)),
                  _const_spec((1, d)), _const_spec((d, N_EXPERTS)), _const_spec((d, N_EXPERTS)),
                  _const_spec((1, N_EXPERTS))],
        out_specs=(row(d), row(d), late(TOP_K), late(TOP_K),
                   pl.BlockSpec((1, N_EXPERTS), lambda i: (0, 0)),
                   pl.BlockSpec((1, 1, N_EXPERTS), lambda i: (jnp.maximum(i - 1, 0), 0, 0))),
        scratch_shapes=[pltpu.VMEM((1, N_EXPERTS), F32), pltpu.VMEM((tm, d), F32)],
        compiler_params=_params("arbitrary"),
        name="merge",
    )(x2, o_a, o_b, g1, w_gate, b_gate, w_pa, w_pb, w_out, g2, w_r_hi, w_r_lo, b_r)
    return outs


def _segment_copies(len_ref, src_ref, dst_ref, tile, make_copy):
    for e in range(N_EXPERTS):
        base = tile * N_EXPERTS + e
        length, src, dst = len_ref[base], src_ref[base], dst_ref[base]
        size = MOE_TILE
        while size >= SEG_ALIGN:
            done = (length // (2 * size)) * (2 * size)

            @pl.when((length & size) != 0)
            def _(size=size, done=done):
                make_copy(pl.multiple_of(src + done, SEG_ALIGN),
                          pl.multiple_of(dst + done, SEG_ALIGN), size).start()

            size //= 2


def _segment_wait(len_ref, tile, make_copy):
    total = len_ref[tile * N_EXPERTS]
    for e in range(1, N_EXPERTS):
        total = total + len_ref[tile * N_EXPERTS + e]
    size = 1 << (STAGE_ROWS.bit_length() - 1)
    while size >= SEG_ALIGN:
        @pl.when((total & size) != 0)
        def _(size=size):
            make_copy(0, 0, size).wait()

        size //= 2


def _dispatch_body(len_ref, src_ref, dst_ref, pad_ref, nb_ref, slot_ref, h_ref,
                   xs_hbm, stage_ref, zero_ref, sem, zero_sem):
    tm = h_ref.shape[0]
    n_blocks = xs_hbm.shape[0] // EXPERT_ROWS
    tile = pl.program_id(0)
    buf = tile % 2

    @pl.when(tile == 0)
    def _():
        zero_ref[...] = jnp.zeros_like(zero_ref)
        for e in range(N_EXPERTS):
            r0 = pl.multiple_of(pad_ref[e], EXPERT_ROWS)
            pltpu.make_async_copy(zero_ref, xs_hbm.at[pl.ds(r0, EXPERT_ROWS)], zero_sem).start()
        for e in range(N_EXPERTS):
            pltpu.make_async_copy(zero_ref, xs_hbm.at[pl.ds(0, EXPERT_ROWS)], zero_sem).wait()

        def fill(b, carry):
            r0 = pl.multiple_of(b * EXPERT_ROWS, EXPERT_ROWS)
            cp = pltpu.make_async_copy(zero_ref, xs_hbm.at[pl.ds(r0, EXPERT_ROWS)], zero_sem)
            cp.start()
            cp.wait()
            return carry

        lax.fori_loop(nb_ref[0], n_blocks, fill, 0)

    slots = slot_ref[...]
    col = lax.broadcasted_iota(I32, (tm, STAGE_ROWS), 1)
    hit = col == slots[:, 0:1]
    for k in range(1, TOP_K):
        hit = hit | (col == slots[:, k:k + 1])
    pick = jnp.where(hit, 1.0, 0.0).astype(BF16)
    stage_ref[buf] = lax.dot_general(pick, h_ref[...].astype(BF16), (((0,), (0,)), ((), ())),
                                     preferred_element_type=F32)

    def copy_from(b):
        return lambda src, dst, size: pltpu.make_async_copy(
            stage_ref.at[b, pl.ds(src, size)], xs_hbm.at[pl.ds(dst, size)], sem)

    @pl.when(tile > 0)
    def _():
        _segment_wait(len_ref, tile - 1, copy_from(1 - buf))

    _segment_copies(len_ref, src_ref, dst_ref, tile, copy_from(buf))

    @pl.when(tile == pl.num_programs(0) - 1)
    def _():
        _segment_wait(len_ref, tile, copy_from(buf))


def _dispatch(seg_len, seg_src, seg_dst, pad_start, n_used, slots, h2, n_rows):
    n, d = h2.shape
    tm = MOE_TILE
    tiled = lambda w: pl.BlockSpec((tm, w), lambda i, *_: (i, 0))
    return pl.pallas_call(
        _dispatch_body,
        out_shape=jax.ShapeDtypeStruct((n_rows, d), F32),
        grid_spec=pltpu.PrefetchScalarGridSpec(
            num_scalar_prefetch=5,
            grid=(n // tm,),
            in_specs=[tiled(TOP_K), tiled(d)],
            out_specs=pl.BlockSpec(memory_space=pl.ANY),
            scratch_shapes=[pltpu.VMEM((2, STAGE_ROWS, d), F32),
                            pltpu.VMEM((EXPERT_ROWS, d), F32),
                            pltpu.SemaphoreType.DMA,
                            pltpu.SemaphoreType.DMA]),
        compiler_params=_params("arbitrary"),
        name="dispatch",
    )(seg_len, seg_src, seg_dst, pad_start, n_used, slots, h2)


def _expert_body(be_ref, nb_ref, x_ref, w1_ref, b1_ref, w2_ref, b2_ref, y_ref, w1b_ref, w2b_ref):
    f = w2_ref.shape[1]
    b = pl.program_id(0)

    @pl.when((b == 0) | (be_ref[b] != be_ref[jnp.maximum(b - 1, 0)]))
    def _():
        w1b_ref[...] = w1_ref[0].astype(BF16)
        w2b_ref[...] = w2_ref[0].astype(BF16)

    @pl.when(b < nb_ref[0])
    def _():
        xb = x_ref[...].astype(BF16)
        hid = jnp.dot(xb, w1b_ref[...], preferred_element_type=F32) + b1_ref[0]
        glu = jnp.minimum(hid[:, :f], SWIGLU_LIMIT)
        lin = jnp.clip(hid[:, f:], -SWIGLU_LIMIT, SWIGLU_LIMIT)
        act = glu * jax.nn.sigmoid(SWIGLU_ALPHA * glu) * (lin + 1.0)
        y_ref[...] = jnp.dot(act.astype(BF16), w2b_ref[...], preferred_element_type=F32) + b2_ref[0]

    @pl.when(b >= nb_ref[0])
    def _():
        y_ref[...] = jnp.zeros_like(y_ref)


def _experts(block_e, n_used, xs, w1, b1, w2, b2, n_blocks):
    d = xs.shape[1]
    f = w2.shape[1]
    br = EXPERT_ROWS
    blk = lambda b, be, nb: (jnp.minimum(b, nb[0] - 1), 0)
    per_e = lambda b, be, nb: (be[b], 0, 0)
    return pl.pallas_call(
        _expert_body,
        out_shape=jax.ShapeDtypeStruct((n_blocks * br, d), F32),
        grid_spec=pltpu.PrefetchScalarGridSpec(
            num_scalar_prefetch=2,
            grid=(n_blocks,),
            in_specs=[pl.BlockSpec((br, d), blk),
                      pl.BlockSpec((1, d, 2 * f), per_e),
                      pl.BlockSpec((1, 1, 2 * f), per_e),
                      pl.BlockSpec((1, f, d), per_e),
                      pl.BlockSpec((1, 1, d), per_e)],
            out_specs=pl.BlockSpec((br, d), lambda b, be, nb: (b, 0)),
            scratch_shapes=[pltpu.VMEM((d, 2 * f), BF16), pltpu.VMEM((f, d), BF16)]),
        compiler_params=_params("arbitrary"),
        name="experts",
    )(block_e, n_used, xs, w1, b1, w2, b2)


def _combine_body(len_ref, src_ref, dst_ref, slot_ref, x1_ref, gate_ref, g_ref,
                  y_hbm, o_ref, stage_ref, sem, *, final_norm):
    tm = x1_ref.shape[0]
    tile = pl.program_id(0)
    buf = tile % 2

    def copy_into(b):
        return lambda src, dst, size: pltpu.make_async_copy(
            y_hbm.at[pl.ds(dst, size)], stage_ref.at[b, pl.ds(src, size)], sem)

    @pl.when(tile == 0)
    def _():
        stage_ref[...] = jnp.zeros_like(stage_ref)
        _segment_copies(len_ref, src_ref, dst_ref, tile, copy_into(0))

    _segment_wait(len_ref, tile, copy_into(buf))

    @pl.when(tile + 1 < pl.num_programs(0))
    def _():
        _segment_copies(len_ref, src_ref, dst_ref, tile + 1, copy_into(1 - buf))

    slots = slot_ref[...]
    gate = gate_ref[...]
    col = lax.broadcasted_iota(I32, (tm, STAGE_ROWS), 1)
    weight = jnp.zeros((tm, STAGE_ROWS), F32)
    for k in range(TOP_K):
        weight = jnp.where(col == slots[:, k:k + 1], gate[:, k:k + 1], weight)
    moe = jnp.dot(weight.astype(BF16), stage_ref[buf].astype(BF16), preferred_element_type=F32)
    acc = x1_ref[...] + moe
    o_ref[...] = _rms(acc, g_ref[...]) if final_norm else acc


def _combine(seg_len, seg_src, seg_dst, slots, x1, gates, g_f, y, final_norm):
    n, d = x1.shape
    tm = MOE_TILE
    tiled = lambda w: pl.BlockSpec((tm, w), lambda i, *_: (i, 0))
    return pl.pallas_call(
        functools.partial(_combine_body, final_norm=final_norm),
        out_shape=jax.ShapeDtypeStruct((n, d), F32),
        grid_spec=pltpu.PrefetchScalarGridSpec(
            num_scalar_prefetch=3,
            grid=(n // tm,),
            in_specs=[tiled(TOP_K), tiled(d), tiled(TOP_K),
                      pl.BlockSpec((1, d), lambda i, *_: (0, 0)),
                      pl.BlockSpec(memory_space=pl.ANY)],
            out_specs=tiled(d),
            scratch_shapes=[pltpu.VMEM((2, STAGE_ROWS, d), F32),
                            pltpu.SemaphoreType.DMA]),
        compiler_params=_params("arbitrary"),
        name="combine",
    )(seg_len, seg_src, seg_dst, slots, x1, gates, g_f, y)


def _layer(x2, bsz, seq, p):
    n, d = x2.shape
    qkv_a, qkv_b = _proj(x2, p["g1"], p["w_in"], seq)
    o_a = _dilated(qkv_a, bsz, seq)
    o_b = _stick(qkv_b, bsz, seq)
    x1, h2, slots, gates, counts, before = _merge(
        x2, o_a, o_b, p["g1"], p["w_gate"], p["b_gate"], p["w_pa"], p["w_pb"], p["w_out"],
        p["g2"], p["w_r"], p["b_r"])

    br = EXPERT_ROWS
    n_tiles = n // MOE_TILE
    before = before.reshape(n_tiles, N_EXPERTS)
    seg_cnt = jnp.concatenate([before[1:], counts], axis=0) - before
    seg_len = (seg_cnt + SEG_ALIGN - 1) // SEG_ALIGN * SEG_ALIGN
    seg_src = jnp.cumsum(seg_len, axis=1) - seg_len
    rows_e = jnp.sum(seg_len, axis=0)
    padded = (rows_e + br - 1) // br * br
    ends = jnp.cumsum(padded)
    starts = ends - padded
    seg_dst = starts[None, :] + jnp.cumsum(seg_len, axis=0) - seg_len
    n_blocks = (n * TOP_K + n_tiles * N_EXPERTS * (SEG_ALIGN - 1)) // br + N_EXPERTS
    block_row = jnp.arange(n_blocks, dtype=I32) * br
    block_e = jnp.minimum(jnp.sum(ends[None, :] <= block_row[:, None], axis=1),
                          N_EXPERTS - 1).astype(I32)
    n_used = (ends[-1:] // br).astype(I32)
    pad_start = (starts + rows_e // br * br).astype(I32)
    tables = [t.astype(I32).reshape(n_tiles * N_EXPERTS) for t in (seg_len, seg_src, seg_dst)]

    xs = _dispatch(*tables, pad_start, n_used, slots, h2, n_blocks * br)
    y = _experts(block_e, n_used, xs, p["w1"], p["b1"], p["w2"], p["b2"], n_blocks)
    return (*tables, slots), x1, gates, y


def kernel(x, norm1_g, w_in, w_proj_a, w_proj_b, w_gate, b_gate, w_out, norm2_g,
           w_router, b_router, w1, b1, w2, b2, norm_f_g):
    bsz, seq, d = x.shape
    depth = w_in.shape[0]
    x2 = x.reshape(bsz * seq, d)
    for l in range(depth):
        p = dict(
            g1=norm1_g[l].reshape(1, d), w_in=w_in[l].astype(BF16),
            w_gate=w_gate[l].astype(BF16), b_gate=b_gate[l].reshape(1, 2 * d),
            w_pa=w_proj_a[l].astype(BF16), w_pb=w_proj_b[l].astype(BF16),
            w_out=w_out[l].astype(BF16), g2=norm2_g[l].reshape(1, d),
            w_r=w_router[l], b_r=b_router[l].reshape(1, N_EXPERTS),
            w1=w1[l], b1=b1[l].reshape(N_EXPERTS, 1, -1),
            w2=w2[l], b2=b2[l].reshape(N_EXPERTS, 1, -1))
        route, x1, gates, y = _layer(x2, bsz, seq, p)
        x2 = _combine(*route, x1, gates, norm_f_g.reshape(1, d), y, final_norm=(l == depth - 1))
    return x2.reshape(bsz, seq, d)
```

```python
import functools

import jax
import jax.numpy as jnp
from jax import lax
from jax.experimental import pallas as pl
from jax.experimental.pallas import tpu as pltpu

F32 = jnp.float32
BF16 = jnp.bfloat16
I32 = jnp.int32

HEAD_DIM = 64
WIDTH = 512
QKV = 3 * WIDTH
DILATED_GROUPS = ((128, 1), (512, 4), (2048, 16))
BAND = 128
ROPE_THETA = 500000.0
ROT_DIM = HEAD_DIM // 4
N_EXPERTS = 32
TOP_K = 4
SWIGLU_ALPHA = 1.702
SWIGLU_LIMIT = 7.0
NORM_EPS = 1e-5

LANES = 128
VMEM_LIMIT = 56 * 1024 * 1024

LOG2E = 1.4426950408889634

PROJ_ROWS = 512
MERGE_ROWS = 512
STICK_ROWS = 512
STICK_KEYS = 256
STICK_TRIP = 4
EXPERT_ROWS = 512
MOE_TILE = MERGE_ROWS
SEG_ALIGN = 8
STAGE_ROWS = MOE_TILE * TOP_K + N_EXPERTS * SEG_ALIGN
DIL_UNROLL = 32
MIX_ROWS = 512


def _rms(x, g):
    return x * lax.rsqrt(jnp.mean(x * x, axis=-1, keepdims=True) + NORM_EPS) * g


def _params(*sem):
    return pltpu.CompilerParams(dimension_semantics=sem, vmem_limit_bytes=VMEM_LIMIT)


def _const_spec(shape):
    return pl.BlockSpec(shape, lambda *_: (0,) * len(shape), pipeline_mode=pl.Buffered(1))


def _proj_body(x_ref, g_ref, w_ref, cos_ref, sa_ref, sb_ref, a_ref, b_ref):
    h = _rms(x_ref[...], g_ref[...]).astype(BF16)
    rep = WIDTH // LANES
    cos = jnp.concatenate([cos_ref[...]] * rep, axis=1)
    sa = jnp.concatenate([sa_ref[...]] * rep, axis=1)
    sb = jnp.concatenate([sb_ref[...]] * rep, axis=1)
    half = ROT_DIM // 2
    for c in range(6):
        acc = jnp.dot(h, w_ref[:, c * WIDTH:(c + 1) * WIDTH], preferred_element_type=F32)
        if c in (0, 1):
            acc = (acc * cos + pltpu.roll(acc, WIDTH - half, 1) * sa
                   + pltpu.roll(acc, half, 1) * sb)
        if c == 0:
            acc = acc * (HEAD_DIM ** -0.5)
        if c == 3:
            acc = acc * (HEAD_DIM ** -0.5 * LOG2E)
        dst = a_ref if c < 3 else b_ref
        dst[:, (c % 3) * WIDTH:(c % 3 + 1) * WIDTH] = acc.astype(dst.dtype)


def _rotary_tables(seq):
    half = ROT_DIM // 2
    inv_freq = jnp.float32(ROPE_THETA) ** (-jnp.arange(0, ROT_DIM, 2, dtype=F32) / ROT_DIM)
    ang = jnp.arange(seq, dtype=jnp.int32).astype(F32)[:, None] * inv_freq[None, :]
    cos, sin = jnp.cos(ang), jnp.sin(ang)
    ones = jnp.ones((seq, HEAD_DIM - ROT_DIM), F32)
    zeros_h = jnp.zeros((seq, half), F32)
    zeros_r = jnp.zeros((seq, HEAD_DIM - ROT_DIM), F32)
    cos_t = jnp.concatenate([cos, cos, ones], axis=1)
    sa_t = jnp.concatenate([-sin, zeros_h, zeros_r], axis=1)
    sb_t = jnp.concatenate([zeros_h, sin, zeros_r], axis=1)
    rep = LANES // HEAD_DIM
    return tuple(jnp.tile(t, (1, rep)) for t in (cos_t, sa_t, sb_t))


def _proj(x2, g1, w_in, seq):
    n, d = x2.shape
    tm = PROJ_ROWS
    tabs = _rotary_tables(seq)
    per_seq = seq // tm
    tab_spec = pl.BlockSpec((tm, LANES), lambda i: (i % per_seq, 0))
    return pl.pallas_call(
        _proj_body,
        out_shape=(jax.ShapeDtypeStruct((n, QKV), F32), jax.ShapeDtypeStruct((n, QKV), BF16)),
        grid=(n // tm,),
        in_specs=[pl.BlockSpec((tm, d), lambda i: (i, 0)),
                  _const_spec((1, d)),
                  _const_spec((d, 2 * QKV)),
                  tab_spec, tab_spec, tab_spec],
        out_specs=(pl.BlockSpec((tm, QKV), lambda i: (i, 0)),
                   pl.BlockSpec((tm, QKV), lambda i: (i, 0))),
        compiler_params=_params("parallel"),
        name="proj",
    )(x2, g1, w_in, *tabs)


def _dil_body(q_ref, k_ref, v_ref, o_ref, os_ref, ls_ref, *, seq):
    heads = LANES // HEAD_DIM
    lane = lax.broadcasted_iota(I32, (1, LANES), 1)
    masks = [(lane >= h * HEAD_DIM) & (lane < (h + 1) * HEAD_DIM) for h in range(heads)]
    qi = lax.broadcasted_iota(I32, (BAND, 2 * BAND), 0)
    ki = lax.broadcasted_iota(I32, (BAND, 2 * BAND), 1)
    neg = jnp.float32(-1e30)
    band_bias = jnp.where((ki >= qi) & (ki <= qi + BAND), 0.0, neg)
    first_bias = jnp.where(ki >= BAND, 0.0, neg)
    nt = (((1,), (1,)), ((), ()))
    n_blocks = seq // BAND

    def rows(ref, start, dil):
        if dil == 1:
            return ref[0, pl.ds(start, BAND), :]
        return ref[0, pl.ds(start, BAND, stride=dil), :]

    for g, (_, dil) in enumerate(DILATED_GROUPS):
        nb = n_blocks // dil

        def block(n, u, last, g=g, dil=dil, nb=nb):
            r = n // nb
            i = n % nb
            cur = r + i * (BAND * dil)
            q = rows(q_ref, cur, dil).astype(BF16)
            k_cur = rows(k_ref, cur, dil).astype(BF16)
            v_cur = rows(v_ref, cur, dil).astype(BF16)
            if u % nb != 0:
                k_prev, v_prev = last
                bias = band_bias
            elif nb <= DIL_UNROLL:
                k_prev, v_prev = k_cur, v_cur
                bias = band_bias + first_bias
            else:
                prev = r + jnp.maximum(i - 1, 0) * (BAND * dil)
                k_prev = rows(k_ref, prev, dil).astype(BF16)
                v_prev = rows(v_ref, prev, dil).astype(BF16)
                bias = band_bias + jnp.where(i > 0, 0.0, 1.0) * first_bias
            kc = jnp.concatenate([k_prev, k_cur], axis=0)
            vc = jnp.concatenate([v_prev, v_cur], axis=0)
            o_t = jnp.zeros((BAND, LANES), F32)
            l_t = jnp.zeros((BAND, LANES), F32)
            for h in range(heads):
                qh = jnp.where(masks[h], q, jnp.zeros_like(q))
                s = lax.dot_general(qh, kc, nt, preferred_element_type=F32) + bias
                m = jnp.max(s, axis=-1, keepdims=True)
                p = jnp.exp(s - m)
                l = jnp.sum(p, axis=-1, keepdims=True)
                pv = jnp.dot(p.astype(BF16), vc, preferred_element_type=F32)
                o_t = jnp.where(masks[h], pv / l, o_t)
                l_t = jnp.where(masks[h], m + jnp.log(l), l_t)
            if dil == 1:
                os_ref[g, pl.ds(cur, BAND), :] = o_t
                ls_ref[g, pl.ds(cur, BAND), :] = l_t
            else:
                os_ref[g, pl.ds(cur, BAND, stride=dil), :] = o_t
                ls_ref[g, pl.ds(cur, BAND, stride=dil), :] = l_t
            return k_cur, v_cur

        def step(t, carry, block=block):
            last = None
            for u in range(DIL_UNROLL):
                last = block(t * DIL_UNROLL + u, u, last)
            return carry

        lax.fori_loop(0, n_blocks // DIL_UNROLL, step, 0)

    def mix(c, carry):
        r0 = pl.multiple_of(c * MIX_ROWS, MIX_ROWS)
        ls = [ls_ref[g, pl.ds(r0, MIX_ROWS), :] for g in range(len(DILATED_GROUPS))]
        m = functools.reduce(jnp.maximum, ls)
        ws = [jnp.exp(l - m) for l in ls]
        num = sum(os_ref[g, pl.ds(r0, MIX_ROWS), :] * w for g, w in enumerate(ws))
        o_ref[0, pl.ds(r0, MIX_ROWS), :] = (num / sum(ws)).astype(BF16)
        return carry

    lax.fori_loop(0, seq // MIX_ROWS, mix, 0)


def _dilated(qkv_a, bsz, seq):
    assert all(w == BAND * d for w, d in DILATED_GROUPS)
    n_blocks = seq // BAND
    assert seq % BAND == 0 and n_blocks % DIL_UNROLL == 0 and seq % MIX_ROWS == 0
    for _, d in DILATED_GROUPS:
        nb = n_blocks // d
        assert n_blocks % d == 0 and (nb % DIL_UNROLL == 0 or DIL_UNROLL % nb == 0)
    a3 = qkv_a.reshape(bsz, seq, QKV)
    hb = WIDTH // LANES
    n_groups = len(DILATED_GROUPS)
    spec = lambda off: pl.BlockSpec((1, seq, LANES), lambda b, h: (b, 0, off * hb + h))
    return pl.pallas_call(
        functools.partial(_dil_body, seq=seq),
        out_shape=jax.ShapeDtypeStruct((bsz, seq, WIDTH), BF16),
        grid=(bsz, hb),
        in_specs=[spec(0), spec(1), spec(2)],
        out_specs=pl.BlockSpec((1, seq, LANES), lambda b, h: (b, 0, h)),
        scratch_shapes=[pltpu.VMEM((n_groups, seq, LANES), F32),
                        pltpu.VMEM((n_groups, seq, LANES), F32)],
        compiler_params=_params("parallel", "parallel"),
        name="dilated",
    )(a3, a3, a3).reshape(bsz * seq, WIDTH)


def _stick_body(q_ref, k_ref, v_ref, o_ref, acc_ref, carry_ref, z_ref, a_ref, *, tq, tk):
    i = pl.program_id(2)
    heads = LANES // HEAD_DIM
    lane = lax.broadcasted_iota(I32, (1, LANES), 1)
    masks = [(lane >= h * HEAD_DIM) & (lane < (h + 1) * HEAD_DIM) for h in range(heads)]
    lr = lax.broadcasted_iota(I32, (tk, tk), 0)
    lc = lax.broadcasted_iota(I32, (tk, tk), 1)
    later = (lr > lc).astype(BF16)
    nt = (((1,), (1,)), ((), ()))
    q = q_ref[0]
    qh = [jnp.where(mk, q, jnp.zeros_like(q)) for mk in masks]
    acc_ref[...] = jnp.zeros_like(acc_ref)
    carry_ref[...] = jnp.zeros_like(carry_ref)
    nd = tq // tk
    n_before = i * nd

    def first_row(j):
        return pl.multiple_of(jnp.maximum(j, 0) * tk, tk)

    def logits(j, top=0):
        kj = k_ref[0, pl.ds(first_row(j), tk), :]
        return [lax.dot_general(qh[h][top:], kj, nt, preferred_element_type=F32)
                for h in range(heads)]

    def weights(zs, j, diag, top=0):
        rows = tq - top
        if diag:
            ti = lax.broadcasted_iota(I32, (rows, tk), 0)
            si = lax.broadcasted_iota(I32, (rows, tk), 1)
            causal = (first_row(j) + si) < (i * tq + top + ti)
        out = []
        for h, z in enumerate(zs):
            neg_abs = pltpu.bitcast(pltpu.bitcast(z, jnp.uint32) | jnp.uint32(0x80000000), F32)
            sp = jnp.maximum(z, 0.0) + jnp.log2(1.0 + jnp.exp2(neg_abs))
            if diag:
                sp = jnp.where(causal, sp, 0.0)
            c = carry_ref[h, top:, :]
            after = (jnp.dot(sp.astype(BF16), later, preferred_element_type=F32)
                     + jnp.concatenate([c] * (tk // LANES), axis=1))
            a = jnp.exp2(z - sp - after)
            if diag:
                a = jnp.where(causal, a, 0.0)
            carry_ref[h, top:, :] = jnp.broadcast_to((after + sp)[:, 0:1], (rows, LANES))
            out.append(a.astype(BF16))
        return out

    def values(a_s, j, top=0):
        vj = v_ref[0, pl.ds(first_row(j), tk), :]
        acc_ref[top:, :] += sum(
            jnp.dot(a_s[h], jnp.where(masks[h], vj, jnp.zeros_like(vj)), preferred_element_type=F32)
            for h in range(heads))

    def stage(slot, arrays, top=0):
        ref = z_ref if arrays[0].dtype == F32 else a_ref
        for h, arr in enumerate(arrays):
            ref[slot, h, top:, :] = arr

    def staged(ref, slot, top=0):
        return [ref[slot, h, top:, :] for h in range(heads)]

    upper, lower, top = n_before + 1, n_before, tk
    stage(0, logits(upper, top), top)
    stage(1, logits(lower))
    stage(0, weights(staged(z_ref, 0, top), upper, True, top), top)
    stage(0, logits(n_before - 1))
    stage(1, weights(staged(z_ref, 1), lower, True))
    values(staged(a_ref, 0, top), upper, top)

    def pipelined(t, p):
        j = n_before - 1 - t
        stage(1 - p, logits(j - 1))
        stage(p, weights(staged(z_ref, p), j, False))
        values(staged(a_ref, 1 - p), j + 1)

    def step(u, carry):
        for k in range(STICK_TRIP):
            pipelined(STICK_TRIP * u + k, k % 2)
        return carry

    lax.fori_loop(0, n_before // STICK_TRIP, step, 0)

    @pl.when(n_before % STICK_TRIP != 0)
    def _():
        for k in range(2):
            pipelined(n_before // STICK_TRIP * STICK_TRIP + k, k)

    values(staged(a_ref, 1), 0)

    o_ref[0] = acc_ref[...].astype(BF16)


def _stick(qkv_b, bsz, seq):
    tq, tk = STICK_ROWS, STICK_KEYS
    assert tq == 2 * tk and seq % tq == 0
    b3 = qkv_b.reshape(bsz, seq, QKV)
    hb = WIDTH // LANES
    heads = LANES // HEAD_DIM
    return pl.pallas_call(
        functools.partial(_stick_body, tq=tq, tk=tk),
        out_shape=jax.ShapeDtypeStruct((bsz, seq, WIDTH), BF16),
        grid=(bsz, hb, seq // tq),
        in_specs=[pl.BlockSpec((1, tq, LANES), lambda b, h, i: (b, i, h)),
                  pl.BlockSpec((1, seq, LANES), lambda b, h, i: (b, 0, hb + h)),
                  pl.BlockSpec((1, seq, LANES), lambda b, h, i: (b, 0, 2 * hb + h))],
        out_specs=pl.BlockSpec((1, tq, LANES), lambda b, h, i: (b, i, h)),
        scratch_shapes=[pltpu.VMEM((tq, LANES), F32),
                        pltpu.VMEM((heads, tq, LANES), F32),
                        pltpu.VMEM((2, heads, tq, tk), F32),
                        pltpu.VMEM((2, heads, tq, tk), BF16)],
        compiler_params=_params("parallel", "parallel", "parallel"),
        name="stick",
    )(b3, b3, b3).reshape(bsz * seq, WIDTH)


def _merge_body(x_ref, oa_ref, ob_ref,
                g1_ref, wg_ref, bg_ref, pa_ref, pb_ref, wo_ref, g2_ref, wrh_ref, wrl_ref, br_ref,
                x1_ref, h2_ref, slot_ref, gate_ref, cnt_ref, before_ref, carry_ref, prev_ref):
    tm, d = x_ref.shape
    step = pl.program_id(0)

    @pl.when(step == 0)
    def _():
        carry_ref[...] = jnp.zeros_like(carry_ref)
        prev_ref[...] = jnp.zeros_like(prev_ref)

    def dot(a, w_ref, lo, hi):
        return jnp.dot(a, w_ref[:, lo:hi], preferred_element_type=F32)

    routed = prev_ref[...]
    r_hi = routed.astype(BF16)
    r_lo = (routed - r_hi.astype(F32)).astype(BF16)
    work = (dot(r_hi, wrh_ref, 0, N_EXPERTS) + dot(r_lo, wrh_ref, 0, N_EXPERTS)
            + dot(r_hi, wrl_ref, 0, N_EXPERTS) + br_ref[...])
    eidx = lax.broadcasted_iota(I32, work.shape, 1)
    tops, hots = [], []
    for _ in range(TOP_K):
        mk = jnp.max(work, axis=-1, keepdims=True)
        ik = jnp.min(jnp.where(work == mk, eidx, N_EXPERTS), axis=-1, keepdims=True)
        hot = eidx == ik
        work = jnp.where(hot, -jnp.inf, work)
        tops.append(mk)
        hots.append(hot)
    exps = [jnp.exp(t - tops[0]) for t in tops]
    den = exps[0] + exps[1] + exps[2] + exps[3]
    gate_ref[...] = jnp.concatenate([e / den for e in exps], axis=1)
    real = jnp.where(step > 0, 1.0, 0.0)
    hots = [hot.astype(F32) * real for hot in hots]

    x = x_ref[...]
    h = _rms(x, g1_ref[...]).astype(BF16)
    gate_a = jax.nn.sigmoid(dot(h, wg_ref, 0, d) + bg_ref[:, :d])
    gate_b = jax.nn.sigmoid(dot(h, wg_ref, d, 2 * d) + bg_ref[:, d:])
    merged = (gate_a * dot(oa_ref[...], pa_ref, 0, d)
              + gate_b * dot(ob_ref[...], pb_ref, 0, d)).astype(BF16)
    x1 = x + jnp.concatenate([dot(merged, wo_ref, 0, d // 2), dot(merged, wo_ref, d // 2, d)],
                             axis=1)
    x1_ref[...] = x1
    h2 = _rms(x1, g2_ref[...])
    h2_ref[...] = h2
    prev_ref[...] = h2

    cnt = sum(hots)
    ri = lax.broadcasted_iota(I32, (tm, tm), 0)
    ci = lax.broadcasted_iota(I32, (tm, tm), 1)
    earlier = (ci < ri).astype(BF16)
    rank = jnp.dot(earlier, cnt.astype(BF16), preferred_element_type=F32)
    tile_cnt = jnp.sum(cnt, axis=0, keepdims=True)
    seg_len = jnp.ceil(tile_cnt * (1.0 / SEG_ALIGN)) * SEG_ALIGN
    ei = lax.broadcasted_iota(I32, (N_EXPERTS, N_EXPERTS), 0)
    ej = lax.broadcasted_iota(I32, (N_EXPERTS, N_EXPERTS), 1)
    seg_start = jnp.dot(jnp.broadcast_to(seg_len, (8, N_EXPERTS)).astype(BF16),
                        (ei < ej).astype(BF16), preferred_element_type=F32)[0:1]
    place = rank + seg_start
    slots = [jnp.sum(hot * place, axis=-1, keepdims=True) for hot in hots]
    slot_ref[...] = jnp.concatenate(slots, axis=1).astype(I32)
    before_ref[0] = carry_ref[...].astype(I32)
    total = carry_ref[...] + tile_cnt
    carry_ref[...] = total
    cnt_ref[...] = total.astype(I32)


def _merge(x2, o_a, o_b, g1, w_gate, b_gate, w_pa, w_pb, w_out, g2, w_r, b_r):
    n, d = x2.shape
    w_r_hi = w_r.astype(BF16)
    w_r_lo = (w_r - w_r_hi.astype(F32)).astype(BF16)
    tm = MERGE_ROWS
    n_tiles = n // tm
    row = lambda w: pl.BlockSpec((tm, w), lambda i: (jnp.minimum(i, n_tiles - 1), 0))
    late = lambda w: pl.BlockSpec((tm, w), lambda i: (jnp.maximum(i - 1, 0), 0))
    outs = pl.pallas_call(
        _merge_body,
        out_shape=(jax.ShapeDtypeStruct((n, d), F32),
                   jax.ShapeDtypeStruct((n, d), F32),
                   jax.ShapeDtypeStruct((n, TOP_K), I32),
                   jax.ShapeDtypeStruct((n, TOP_K), F32),
                   jax.ShapeDtypeStruct((1, N_EXPERTS), I32),
                   jax.ShapeDtypeStruct((n // tm, 1, N_EXPERTS), I32)),
        grid=(n_tiles + 1,),
        in_specs=[row(d), row(WIDTH), row(WIDTH),
                  _const_spec((1, d)), _const_spec((d, 2 * d)), _const_spec((1, 2 * d)),
                  _const_spec((WIDTH, d)), _const_spec((WIDTH, d)), _const_spec((d, d)),
                  _const_spec((1, d)), _const_spec((d, N_EXPERTS)), _const_spec((d, N_EXPERTS)),
                  _const_spec((1, N_EXPERTS))],
        out_specs=(row(d), row(d), late(TOP_K), late(TOP_K),
                   pl.BlockSpec((1, N_EXPERTS), lambda i: (0, 0)),
                   pl.BlockSpec((1, 1, N_EXPERTS), lambda i: (jnp.maximum(i - 1, 0), 0, 0))),
        scratch_shapes=[pltpu.VMEM((1, N_EXPERTS), F32), pltpu.VMEM((tm, d), F32)],
        compiler_params=_params("arbitrary"),
        name="merge",
    )(x2, o_a, o_b, g1, w_gate, b_gate, w_pa, w_pb, w_out, g2, w_r_hi, w_r_lo, b_r)
    return outs


def _segment_copies(len_ref, src_ref, dst_ref, tile, make_copy):
    for e in range(N_EXPERTS):
        base = tile * N_EXPERTS + e
        length, src, dst = len_ref[base], src_ref[base], dst_ref[base]
        size = MOE_TILE
        while size >= SEG_ALIGN:
            done = (length // (2 * size)) * (2 * size)

            @pl.when((length & size) != 0)
            def _(size=size, done=done):
                make_copy(pl.multiple_of(src + done, SEG_ALIGN),
                          pl.multiple_of(dst + done, SEG_ALIGN), size).start(priority=e % 2)

            size //= 2


def _segment_wait(len_ref, tile, make_copy):
    total = len_ref[tile * N_EXPERTS]
    for e in range(1, N_EXPERTS):
        total = total + len_ref[tile * N_EXPERTS + e]
    size = 1 << (STAGE_ROWS.bit_length() - 1)
    while size >= SEG_ALIGN:
        @pl.when((total & size) != 0)
        def _(size=size):
            make_copy(0, 0, size).wait()

        size //= 2


def _dispatch_body(len_ref, src_ref, dst_ref, pad_ref, nb_ref, slot_ref, h_ref,
                   xs_hbm, stage_ref, zero_ref, sem, zero_sem):
    tm = h_ref.shape[0]
    n_blocks = xs_hbm.shape[0] // EXPERT_ROWS
    tile = pl.program_id(0)
    buf = tile % 2

    @pl.when(tile == 0)
    def _():
        zero_ref[...] = jnp.zeros_like(zero_ref)
        for e in range(N_EXPERTS):
            r0 = pl.multiple_of(pad_ref[e], EXPERT_ROWS)
            pltpu.make_async_copy(zero_ref, xs_hbm.at[pl.ds(r0, EXPERT_ROWS)], zero_sem).start()
        for e in range(N_EXPERTS):
            pltpu.make_async_copy(zero_ref, xs_hbm.at[pl.ds(0, EXPERT_ROWS)], zero_sem).wait()

        def fill(b, carry):
            r0 = pl.multiple_of(b * EXPERT_ROWS, EXPERT_ROWS)
            cp = pltpu.make_async_copy(zero_ref, xs_hbm.at[pl.ds(r0, EXPERT_ROWS)], zero_sem)
            cp.start()
            cp.wait()
            return carry

        lax.fori_loop(nb_ref[0], n_blocks, fill, 0)

    slots = slot_ref[...]
    col = lax.broadcasted_iota(I32, (tm, STAGE_ROWS), 1)
    hit = col == slots[:, 0:1]
    for k in range(1, TOP_K):
        hit = hit | (col == slots[:, k:k + 1])
    pick = jnp.where(hit, 1.0, 0.0).astype(BF16)
    stage_ref[buf] = lax.dot_general(pick, h_ref[...].astype(BF16), (((0,), (0,)), ((), ())),
                                     preferred_element_type=F32)

    def copy_from(b):
        return lambda src, dst, size: pltpu.make_async_copy(
            stage_ref.at[b, pl.ds(src, size)], xs_hbm.at[pl.ds(dst, size)], sem)

    @pl.when(tile > 0)
    def _():
        _segment_wait(len_ref, tile - 1, copy_from(1 - buf))

    _segment_copies(len_ref, src_ref, dst_ref, tile, copy_from(buf))

    @pl.when(tile == pl.num_programs(0) - 1)
    def _():
        _segment_wait(len_ref, tile, copy_from(buf))


def _dispatch(seg_len, seg_src, seg_dst, pad_start, n_used, slots, h2, n_rows):
    n, d = h2.shape
    tm = MOE_TILE
    tiled = lambda w: pl.BlockSpec((tm, w), lambda i, *_: (i, 0))
    return pl.pallas_call(
        _dispatch_body,
        out_shape=jax.ShapeDtypeStruct((n_rows, d), F32),
        grid_spec=pltpu.PrefetchScalarGridSpec(
            num_scalar_prefetch=5,
            grid=(n // tm,),
            in_specs=[tiled(TOP_K), tiled(d)],
            out_specs=pl.BlockSpec(memory_space=pl.ANY),
            scratch_shapes=[pltpu.VMEM((2, STAGE_ROWS, d), F32),
                            pltpu.VMEM((EXPERT_ROWS, d), F32),
                            pltpu.SemaphoreType.DMA,
                            pltpu.SemaphoreType.DMA]),
        compiler_params=_params("arbitrary"),
        name="dispatch",
    )(seg_len, seg_src, seg_dst, pad_start, n_used, slots, h2)


def _expert_body(be_ref, nb_ref, x_ref, w1_ref, b1_ref, w2_ref, b2_ref, y_ref, w1b_ref, w2b_ref):
    f = w2_ref.shape[1]
    b = pl.program_id(0)

    @pl.when((b == 0) | (be_ref[b] != be_ref[jnp.maximum(b - 1, 0)]))
    def _():
        w1b_ref[...] = w1_ref[0].astype(BF16)
        w2b_ref[...] = w2_ref[0].astype(BF16)

    @pl.when(b < nb_ref[0])
    def _():
        xb = x_ref[...].astype(BF16)
        hid = jnp.dot(xb, w1b_ref[...], preferred_element_type=F32) + b1_ref[0]
        glu = jnp.minimum(hid[:, :f], SWIGLU_LIMIT)
        lin = jnp.clip(hid[:, f:], -SWIGLU_LIMIT, SWIGLU_LIMIT)
        act = glu * jax.nn.sigmoid(SWIGLU_ALPHA * glu) * (lin + 1.0)
        y_ref[...] = jnp.dot(act.astype(BF16), w2b_ref[...], preferred_element_type=F32) + b2_ref[0]

    @pl.when(b >= nb_ref[0])
    def _():
        y_ref[...] = jnp.zeros_like(y_ref)


def _experts(block_e, n_used, xs, w1, b1, w2, b2, n_blocks):
    d = xs.shape[1]
    f = w2.shape[1]
    br = EXPERT_ROWS
    blk = lambda b, be, nb: (jnp.minimum(b, nb[0] - 1), 0)
    per_e = lambda b, be, nb: (be[b], 0, 0)
    return pl.pallas_call(
        _expert_body,
        out_shape=jax.ShapeDtypeStruct((n_blocks * br, d), F32),
        grid_spec=pltpu.PrefetchScalarGridSpec(
            num_scalar_prefetch=2,
            grid=(n_blocks,),
            in_specs=[pl.BlockSpec((br, d), blk),
                      pl.BlockSpec((1, d, 2 * f), per_e),
                      pl.BlockSpec((1, 1, 2 * f), per_e),
                      pl.BlockSpec((1, f, d), per_e),
                      pl.BlockSpec((1, 1, d), per_e)],
            out_specs=pl.BlockSpec((br, d), lambda b, be, nb: (b, 0)),
            scratch_shapes=[pltpu.VMEM((d, 2 * f), BF16), pltpu.VMEM((f, d), BF16)]),
        compiler_params=_params("arbitrary"),
        name="experts",
    )(block_e, n_used, xs, w1, b1, w2, b2)


def _combine_body(len_ref, src_ref, dst_ref, slot_ref, x1_ref, gate_ref, g_ref,
                  y_hbm, o_ref, stage_ref, sem, *, final_norm):
    tm = x1_ref.shape[0]
    tile = pl.program_id(0)
    buf = tile % 2

    def copy_into(b):
        return lambda src, dst, size: pltpu.make_async_copy(
            y_hbm.at[pl.ds(dst, size)], stage_ref.at[b, pl.ds(src, size)], sem)

    @pl.when(tile == 0)
    def _():
        stage_ref[...] = jnp.zeros_like(stage_ref)
        _segment_copies(len_ref, src_ref, dst_ref, tile, copy_into(0))

    _segment_wait(len_ref, tile, copy_into(buf))

    @pl.when(tile + 1 < pl.num_programs(0))
    def _():
        _segment_copies(len_ref, src_ref, dst_ref, tile + 1, copy_into(1 - buf))

    slots = slot_ref[...]
    gate = gate_ref[...]
    col = lax.broadcasted_iota(I32, (tm, STAGE_ROWS), 1)
    weight = jnp.zeros((tm, STAGE_ROWS), F32)
    for k in range(TOP_K):
        weight = jnp.where(col == slots[:, k:k + 1], gate[:, k:k + 1], weight)
    moe = jnp.dot(weight.astype(BF16), stage_ref[buf].astype(BF16), preferred_element_type=F32)
    acc = x1_ref[...] + moe
    o_ref[...] = _rms(acc, g_ref[...]) if final_norm else acc


def _combine(seg_len, seg_src, seg_dst, slots, x1, gates, g_f, y, final_norm):
    n, d = x1.shape
    tm = MOE_TILE
    tiled = lambda w: pl.BlockSpec((tm, w), lambda i, *_: (i, 0))
    return pl.pallas_call(
        functools.partial(_combine_body, final_norm=final_norm),
        out_shape=jax.ShapeDtypeStruct((n, d), F32),
        grid_spec=pltpu.PrefetchScalarGridSpec(
            num_scalar_prefetch=3,
            grid=(n // tm,),
            in_specs=[tiled(TOP_K), tiled(d), tiled(TOP_K),
                      pl.BlockSpec((1, d), lambda i, *_: (0, 0)),
                      pl.BlockSpec(memory_space=pl.ANY)],
            out_specs=tiled(d),
            scratch_shapes=[pltpu.VMEM((2, STAGE_ROWS, d), F32),
                            pltpu.SemaphoreType.DMA]),
        compiler_params=_params("arbitrary"),
        name="combine",
    )(seg_len, seg_src, seg_dst, slots, x1, gates, g_f, y)


def _layer(x2, bsz, seq, p):
    n, d = x2.shape
    qkv_a, qkv_b = _proj(x2, p["g1"], p["w_in"], seq)
    o_a = _dilated(qkv_a, bsz, seq)
    o_b = _stick(qkv_b, bsz, seq)
    x1, h2, slots, gates, counts, before = _merge(
        x2, o_a, o_b, p["g1"], p["w_gate"], p["b_gate"], p["w_pa"], p["w_pb"], p["w_out"],
        p["g2"], p["w_r"], p["b_r"])

    br = EXPERT_ROWS
    n_tiles = n // MOE_TILE
    before = before.reshape(n_tiles, N_EXPERTS)
    seg_cnt = jnp.concatenate([before[1:], counts], axis=0) - before
    seg_len = (seg_cnt + SEG_ALIGN - 1) // SEG_ALIGN * SEG_ALIGN
    seg_src = jnp.cumsum(seg_len, axis=1) - seg_len
    rows_e = jnp.sum(seg_len, axis=0)
    padded = (rows_e + br - 1) // br * br
    ends = jnp.cumsum(padded)
    starts = ends - padded
    seg_dst = starts[None, :] + jnp.cumsum(seg_len, axis=0) - seg_len
    n_blocks = (n * TOP_K + n_tiles * N_EXPERTS * (SEG_ALIGN - 1)) // br + N_EXPERTS
    block_row = jnp.arange(n_blocks, dtype=I32) * br
    block_e = jnp.minimum(jnp.sum(ends[None, :] <= block_row[:, None], axis=1),
                          N_EXPERTS - 1).astype(I32)
    n_used = (ends[-1:] // br).astype(I32)
    pad_start = (starts + rows_e // br * br).astype(I32)
    tables = [t.astype(I32).reshape(n_tiles * N_EXPERTS) for t in (seg_len, seg_src, seg_dst)]

    xs = _dispatch(*tables, pad_start, n_used, slots, h2, n_blocks * br)
    y = _experts(block_e, n_used, xs, p["w1"], p["b1"], p["w2"], p["b2"], n_blocks)
    return (*tables, slots), x1, gates, y


def kernel(x, norm1_g, w_in, w_proj_a, w_proj_b, w_gate, b_gate, w_out, norm2_g,
           w_router, b_router, w1, b1, w2, b2, norm_f_g):
    bsz, seq, d = x.shape
    depth = w_in.shape[0]
    x2 = x.reshape(bsz * seq, d)
    for l in range(depth):
        p = dict(
            g1=norm1_g[l].reshape(1, d), w_in=w_in[l].astype(BF16),
            w_gate=w_gate[l].astype(BF16), b_gate=b_gate[l].reshape(1, 2 * d),
            w_pa=w_proj_a[l].astype(BF16), w_pb=w_proj_b[l].astype(BF16),
            w_out=w_out[l].astype(BF16), g2=norm2_g[l].reshape(1, d),
            w_r=w_router[l], b_r=b_router[l].reshape(1, N_EXPERTS),
            w1=w1[l], b1=b1[l].reshape(N_EXPERTS, 1, -1),
            w2=w2[l], b2=b2[l].reshape(N_EXPERTS, 1, -1))
        route, x1, gates, y = _layer(x2, bsz, seq, p)
        x2 = _combine(*route, x1, gates, norm_f_g.reshape(1, d), y, final_norm=(l == depth - 1))
    return x2.reshape(bsz, seq, d)
```
